```python
import math
import jax, jax.numpy as jnp
from jax import lax
import numpy as np

D_MODEL = 2048
BATCH = 1
SEQ = 8192
DEPTH = 2

GRID_W = 64
CTX_LEN = 256
EPS = 1e-6
N_ADA = 6
N_BRANCHES = 2

HY_WIDTH = D_MODEL
HY_SHORT = 3
HY_EMB_BANDS = 16
HY_EMB_DIM = 1 + 2 * HY_EMB_BANDS
HY_FILTER_HIDDEN = 64
HY_DECAY_TARGET = 1e-2
HY_FAST_DECAY = 0.3
HY_SLOW_DECAY = 1.5
HY_MIN_DECAY = math.log(HY_DECAY_TARGET) / HY_SLOW_DECAY
HY_MAX_DECAY = math.log(HY_DECAY_TARGET) / HY_FAST_DECAY

GLA_HEADS = 4
GLA_QK_WIDTH = D_MODEL // 2
GLA_V_WIDTH = D_MODEL
GLA_DK = GLA_QK_WIDTH // GLA_HEADS
GLA_DV = GLA_V_WIDTH // GLA_HEADS
GLA_GATE_RANK = 16
GLA_GATE_TEMP = 16.0
GLA_CHUNK = 64

COL_K = 0
COL_V = COL_K + GLA_QK_WIDTH
COL_AF = COL_V + GLA_V_WIDTH
COL_AB = COL_AF + GLA_GATE_RANK
STATE_COLS = COL_AB + GLA_GATE_RANK
COL_Q = STATE_COLS
COL_G = COL_Q + GLA_QK_WIDTH
COL_HY = COL_G + GLA_V_WIDTH
COL_GATE = COL_HY + 3 * HY_WIDTH
N_IN_COLS = COL_GATE + N_BRANCHES * D_MODEL

D_FF_DENSE = 128 * ((8 * D_MODEL // 3 + 127) // 128)
N_EXPERTS = 8
TOP_K = 2
D_FF_EXPERT = 7 * D_MODEL // 2

kernel_name = "hybrid_gla_hyena_moe_dit"

F32 = jnp.float32


def _rmsnorm(x, g):
    xf = x.astype(F32)
    xf = xf * lax.rsqrt(jnp.mean(xf * xf, axis=-1, keepdims=True) + EPS)
    return (xf * g.astype(F32)).astype(x.dtype)


def _modulation(cond, w, b):
    m = jax.nn.silu(cond) @ w + b
    return jnp.split(m, N_ADA, axis=-1)


def _modulate(x, g, shift, scale):
    return _rmsnorm(x, g) * (1 + scale) + shift


def _flip(a):
    return jnp.flip(a, axis=1)


def _gla_log_decay(a_low, w, b):
    B, L, _ = a_low.shape
    z = (a_low @ w + b).astype(F32)
    return (jax.nn.log_sigmoid(z) / GLA_GATE_TEMP).reshape(B, L, GLA_HEADS, GLA_DK)


def _gla_kv_decay(proj, p):
    B, L, _ = proj.shape
    k = proj[..., COL_K:COL_V].astype(F32).reshape(B, L, GLA_HEADS, GLA_DK)
    v = proj[..., COL_V:COL_AF].astype(F32).reshape(B, L, GLA_HEADS, GLA_DV)
    la_f = _gla_log_decay(proj[..., COL_AF:COL_AB], p["gla_aw_f"], p["gla_ab_f"])
    la_b = _gla_log_decay(proj[..., COL_AB:STATE_COLS], p["gla_aw_b"], p["gla_ab_b"])
    return k, v, la_f, la_b


def _gla_scan(q, k, v, loga, s0):
    B, L = q.shape[:2]
    n_chunks = L // GLA_CHUNK

    def to_chunks(a):
        return a.reshape(B, n_chunks, GLA_CHUNK, GLA_HEADS, -1).transpose(1, 0, 3, 2, 4)

    mask = jnp.tril(jnp.ones((GLA_CHUNK, GLA_CHUNK), bool))[:, :, None]

    def step(S, xs):
        qc, kc, vc, gc = xs
        b = jnp.cumsum(gc, axis=2)
        diff = b[:, :, :, None, :] - b[:, :, None, :, :]
        decay = jnp.exp(jnp.where(mask, diff, -jnp.inf))
        att = jnp.einsum("bhid,bhijd->bhij", qc, decay * kc[:, :, None])
        o = jnp.einsum("bhij,bhjv->bhiv", att, vc) + jnp.einsum("bhid,bhdv->bhiv", qc * jnp.exp(b), S)
        b_last = b[:, :, -1:, :]
        S_new = jnp.exp(b_last[:, :, 0, :])[..., None] * S + jnp.einsum(
            "bhjd,bhjv->bhdv", kc * jnp.exp(b_last - b), vc)
        return S_new, o

    s_fin, o = lax.scan(step, s0, (to_chunks(q), to_chunks(k), to_chunks(v), to_chunks(loga)))
    o = o.transpose(1, 0, 3, 2, 4).reshape(B, L, GLA_HEADS, GLA_DV)
    return o, s_fin


def _gla_final_state(k, v, loga):
    b = jnp.cumsum(loga, axis=1)
    return jnp.einsum("blhd,blhv->bhdv", k * jnp.exp(b[:, -1:] - b), v)


def _short_conv(u, w, b):
    L = u.shape[1]
    pad = HY_SHORT // 2
    up = jnp.pad(u, ((0, 0), (pad, pad), (0, 0)))
    out = b
    for j in range(HY_SHORT):
        out = out + up[:, j:j + L] * w[j]
    return out


def _hyena_filter(L, p):
    pos = jnp.arange(L, dtype=F32)
    t = pos / max(L - 1, 1)
    bands = jnp.linspace(1e-4, HY_EMB_BANDS - 1, HY_EMB_BANDS, dtype=F32)
    ang = (2.0 * math.pi / L) * pos[:, None] * bands[None, :]
    z = jnp.concatenate([t[:, None], jnp.cos(ang), -jnp.sin(ang)], axis=-1)
    freq = p["hy_freq"].astype(F32)
    h = jnp.sin(freq * (z @ p["hy_w1"].astype(F32) + p["hy_b1"].astype(F32)))
    h = jnp.sin(freq * (h @ p["hy_w2"].astype(F32) + p["hy_b2"].astype(F32)))
    h = jnp.sin(freq * (h @ p["hy_w3"].astype(F32) + p["hy_b3"].astype(F32)))
    h = h @ p["hy_w4"].astype(F32)
    deltas = jnp.abs(jnp.linspace(HY_MIN_DECAY, HY_MAX_DECAY, HY_WIDTH, dtype=F32))
    window = jnp.exp(-t[:, None] * deltas[None, :])
    h_fwd = h[:, :HY_WIDTH] * window
    h_bwd = h[:, HY_WIDTH:] * window
    filt = jnp.concatenate([h_fwd, jnp.zeros((1, HY_WIDTH), F32), h_bwd[:0:-1]], axis=0)
    return filt * lax.rsqrt(jnp.sum(filt * filt, axis=0, keepdims=True) + EPS)


def _hyena(u, p):
    B, L, _ = u.shape
    dt = u.dtype
    u = _short_conv(u, p["hy_conv_w"], p["hy_conv_b"])
    x0, x1, v = jnp.split(u, 3, axis=-1)
    filt = _hyena_filter(L, p)
    z = (x1 * v).astype(F32)
    zf = jnp.fft.rfft(z, n=2 * L, axis=1)
    hf = jnp.fft.rfft(filt, n=2 * L, axis=0)
    y = jnp.fft.irfft(zf * hf[None], n=2 * L, axis=1)[:, :L] + z * p["hy_bias"].astype(F32)
    return x0 * y.astype(dt)


def _token_mixer(h, s0_f, s0_b, p):
    B, L, _ = h.shape
    dt = h.dtype
    proj = h @ p["w_in"]
    k, v, la_f, la_b = _gla_kv_decay(proj, p)
    q = proj[..., COL_Q:COL_G].astype(F32).reshape(B, L, GLA_HEADS, GLA_DK) * (GLA_DK ** -0.5)
    o_f, s_f = _gla_scan(q, k, v, la_f, s0_f)
    o_b, s_b = _gla_scan(_flip(q), _flip(k), _flip(v), _flip(la_b), s0_b)
    o = _rmsnorm(o_f + _flip(o_b), p["gla_norm_g"]).astype(dt).reshape(B, L, GLA_V_WIDTH)
    o_gla = o * jax.nn.silu(proj[..., COL_G:COL_HY])
    o_hy = _hyena(proj[..., COL_HY:COL_GATE], p)
    gate_hy, gate_gla = jnp.split(jax.nn.sigmoid(proj[..., COL_GATE:]), N_BRANCHES, axis=-1)
    merged = gate_hy * (o_hy @ p["w_up_hy"]) + gate_gla * (o_gla @ p["w_up_gla"])
    return merged @ p["w_out"], s_f, s_b


def _context_states(h, p):
    proj = h @ p["w_in"][:, :STATE_COLS]
    k, v, la_f, la_b = _gla_kv_decay(proj, p)
    return _gla_final_state(k, v, la_f), _gla_final_state(_flip(k), _flip(v), _flip(la_b))


def _swiglu(h, wg, wu, wd):
    return (jax.nn.silu(h @ wg) * (h @ wu)) @ wd


def _moe(h, router_w, wg, wu, wd):
    logits = (h @ router_w).astype(F32)
    top_val, top_idx = lax.top_k(logits, TOP_K)
    top_w = jax.nn.softmax(top_val, axis=-1)
    gates = jnp.sum(jax.nn.one_hot(top_idx, N_EXPERTS, dtype=F32) * top_w[..., None], axis=-2).astype(h.dtype)
    y = jnp.zeros_like(h)
    for e in range(N_EXPERTS):
        y = y + gates[..., e:e + 1] * _swiglu(h, wg[e], wu[e], wd[e])
    return y


def setup_inputs(seed: int = 0) -> dict:
    key = jax.random.key(seed)
    ks = iter(jax.random.split(key, 48))

    def nrm(shape, scale):
        return jax.random.normal(next(ks), shape, jnp.float32) * scale

    D = D_MODEL
    W = HY_WIDTH
    n_dense = (DEPTH + 1) // 2
    n_moe = DEPTH // 2
    return {
        "x": nrm((BATCH, SEQ, D), 1.0),
        "c": nrm((BATCH, D), 1.0),
        "ctx": nrm((BATCH, CTX_LEN, D), 1.0),
        "c_ctx": nrm((D,), 1.0),
        "ada_w": nrm((DEPTH, D, N_ADA * D), 0.5 * D ** -0.5),
        "ada_b": nrm((DEPTH, N_ADA * D), 0.02),
        "norm_mix_g": 1.0 + nrm((DEPTH, D), 0.02),
        "norm_ffn_g": 1.0 + nrm((DEPTH, D), 0.02),
        "w_in": nrm((DEPTH, D, N_IN_COLS), D ** -0.5),
        "hy_conv_w": nrm((DEPTH, HY_SHORT, 3 * W), HY_SHORT ** -0.5),
        "hy_conv_b": nrm((DEPTH, 3 * W), 0.02),
        "hy_w1": nrm((DEPTH, HY_EMB_DIM, HY_FILTER_HIDDEN), HY_EMB_DIM ** -0.5),
        "hy_b1": nrm((DEPTH, HY_FILTER_HIDDEN), 0.02),
        "hy_w2": nrm((DEPTH, HY_FILTER_HIDDEN, HY_FILTER_HIDDEN), HY_FILTER_HIDDEN ** -0.5),
        "hy_b2": nrm((DEPTH, HY_FILTER_HIDDEN), 0.02),
        "hy_w3": nrm((DEPTH, HY_FILTER_HIDDEN, HY_FILTER_HIDDEN), HY_FILTER_HIDDEN ** -0.5),
        "hy_b3": nrm((DEPTH, HY_FILTER_HIDDEN), 0.02),
        "hy_w4": nrm((DEPTH, HY_FILTER_HIDDEN, 2 * W), HY_FILTER_HIDDEN ** -0.5),
        "hy_freq": 1.0 + nrm((DEPTH, HY_FILTER_HIDDEN), 0.02),
        "hy_bias": nrm((DEPTH, W), 0.5),
        "gla_aw_f": nrm((DEPTH, GLA_GATE_RANK, GLA_QK_WIDTH), GLA_GATE_RANK ** -0.5),
        "gla_ab_f": nrm((DEPTH, GLA_QK_WIDTH), 0.02),
        "gla_aw_b": nrm((DEPTH, GLA_GATE_RANK, GLA_QK_WIDTH), GLA_GATE_RANK ** -0.5),
        "gla_ab_b": nrm((DEPTH, GLA_QK_WIDTH), 0.02),
        "gla_norm_g": 1.0 + nrm((DEPTH, GLA_DV), 0.02),
        "w_up_hy": nrm((DEPTH, W, D), W ** -0.5),
        "w_up_gla": nrm((DEPTH, GLA_V_WIDTH, D), GLA_V_WIDTH ** -0.5),
        "w_out": nrm((DEPTH, D, D), D ** -0.5),
        "ffn_w_gate": nrm((n_dense, D, D_FF_DENSE), D ** -0.5),
        "ffn_w_up": nrm((n_dense, D, D_FF_DENSE), D ** -0.5),
        "ffn_w_down": nrm((n_dense, D_FF_DENSE, D), D_FF_DENSE ** -0.5),
        "router_w": nrm((n_moe, D, N_EXPERTS), D ** -0.5),
        "exp_w_gate": nrm((n_moe, N_EXPERTS, D, D_FF_EXPERT), D ** -0.5),
        "exp_w_up": nrm((n_moe, N_EXPERTS, D, D_FF_EXPERT), D ** -0.5),
        "exp_w_down": nrm((n_moe, N_EXPERTS, D_FF_EXPERT, D), D_FF_EXPERT ** -0.5),
        "final_norm_g": 1.0 + nrm((D,), 0.02),
    }


def reference(x, c, ctx, c_ctx, ada_w, ada_b, norm_mix_g, norm_ffn_g, w_in, hy_conv_w, hy_conv_b,
              hy_w1, hy_b1, hy_w2, hy_b2, hy_w3, hy_b3, hy_w4, hy_freq, hy_bias,
              gla_aw_f, gla_ab_f, gla_aw_b, gla_ab_b, gla_norm_g, w_up_hy, w_up_gla, w_out,
              ffn_w_gate, ffn_w_up, ffn_w_down, router_w, exp_w_gate, exp_w_up, exp_w_down,
              final_norm_g):
    B = x.shape[0]
    x_lat, x_ctx = x, ctx
    for l in range(DEPTH):
        last = l == DEPTH - 1
        p = {
            "w_in": w_in[l], "hy_conv_w": hy_conv_w[l], "hy_conv_b": hy_conv_b[l],
            "hy_w1": hy_w1[l], "hy_b1": hy_b1[l], "hy_w2": hy_w2[l], "hy_b2": hy_b2[l],
            "hy_w3": hy_w3[l], "hy_b3": hy_b3[l], "hy_w4": hy_w4[l], "hy_freq": hy_freq[l],
            "hy_bias": hy_bias[l], "gla_aw_f": gla_aw_f[l], "gla_ab_f": gla_ab_f[l],
            "gla_aw_b": gla_aw_b[l], "gla_ab_b": gla_ab_b[l], "gla_norm_g": gla_norm_g[l],
            "w_up_hy": w_up_hy[l], "w_up_gla": w_up_gla[l], "w_out": w_out[l],
        }
        lat_mod = [m[:, None, :] for m in _modulation(c, ada_w[l], ada_b[l])]
        ctx_mod = _modulation(c_ctx, ada_w[l], ada_b[l])

        h_lat = _modulate(x_lat, norm_mix_g[l], lat_mod[0], lat_mod[1])
        h_ctx = _modulate(x_ctx, norm_mix_g[l], ctx_mod[0], ctx_mod[1])
        if last:
            s_f, s_b = _context_states(h_ctx, p)
        else:
            zero_state = jnp.zeros((B, GLA_HEADS, GLA_DK, GLA_DV), F32)
            o_ctx, s_f, s_b = _token_mixer(h_ctx, zero_state, zero_state, p)
            x_ctx = x_ctx + ctx_mod[2] * o_ctx
        o_lat, _, _ = _token_mixer(h_lat, s_f, s_b, p)
        x_lat = x_lat + lat_mod[2] * o_lat

        if l % 2 == 0:
            i = l // 2
            ffn = lambda h, i=i: _swiglu(h, ffn_w_gate[i], ffn_w_up[i], ffn_w_down[i])
        else:
            i = l // 2
            ffn = lambda h, i=i: _moe(h, router_w[i], exp_w_gate[i], exp_w_up[i], exp_w_down[i])
        x_lat = x_lat + lat_mod[5] * ffn(_modulate(x_lat, norm_ffn_g[l], lat_mod[3], lat_mod[4]))
        if not last:
            x_ctx = x_ctx + ctx_mod[5] * ffn(_modulate(x_ctx, norm_ffn_g[l], ctx_mod[3], ctx_mod[4]))
    return _rmsnorm(x_lat, final_norm_g)
```

```python
import functools
import math

import numpy as np
import jax
import jax.numpy as jnp
from jax import lax
from jax.experimental import pallas as pl
from jax.experimental.pallas import tpu as pltpu

F32 = jnp.float32
BF16 = jnp.bfloat16
HIGHEST = lax.Precision.HIGHEST

EPS = 1e-6
N_ADA = 6
LANES = 128
SUBLANES = 8
VMEM_LIMIT_CAP = 60 * 1024 * 1024

GLA_HEADS = 4
GLA_GATE_RANK = 16
GLA_GATE_TEMP = 16.0
HY_SHORT = 3
HY_EMB_BANDS = 16
HY_FILTER_HIDDEN = 64
HY_DECAY_TARGET = 1e-2
HY_FAST_DECAY = 0.3
HY_SLOW_DECAY = 1.5
N_EXPERTS = 8
TOP_K = 2


def _cparams(sem, vmem_bytes):
    limit = int(min(max(vmem_bytes * 5 // 4 + (2 << 20), 16 << 20), VMEM_LIMIT_CAP))
    return pltpu.CompilerParams(dimension_semantics=sem, vmem_limit_bytes=limit)


def _nbytes(shape, dtype):
    return int(np.prod(shape)) * jnp.dtype(dtype).itemsize


def _ada_body(c_ref, w_ref, b_ref, o_ref):
    tn = o_ref.shape[-1]
    d = c_ref.shape[1]
    rows = []
    for r in range(2):
        s = c_ref[r]
        s = s * jax.nn.sigmoid(s)
        chunks = []
        for n0 in range(0, tn, LANES):
            p = w_ref[:, n0:n0 + LANES] * s
            acc = p.reshape(d // SUBLANES, SUBLANES, LANES).sum(axis=0)
            chunks.append(acc.sum(axis=0, keepdims=True))
        rows.append(jnp.concatenate(chunks, axis=1))
    o_ref[...] = jnp.concatenate(rows, axis=0) + b_ref[...]


def _ada_modulation(cond2, ada_w, ada_b):
    depth, d, n = ada_w.shape
    tn = 1536 if n % 1536 == 0 else LANES
    cb = jnp.broadcast_to(cond2[:, :, None], (2, d, LANES))
    vm = 2 * _nbytes((d, tn), F32) + 2 * _nbytes((2, d, LANES), F32)
    return pl.pallas_call(
        _ada_body,
        grid=(depth, n // tn),
        in_specs=[
            pl.BlockSpec((2, d, LANES), lambda l, j: (0, 0, 0)),
            pl.BlockSpec((None, d, tn), lambda l, j: (l, 0, j)),
            pl.BlockSpec((None, 1, tn), lambda l, j: (l, 0, j)),
        ],
        out_specs=pl.BlockSpec((None, 2, tn), lambda l, j: (l, 0, j)),
        out_shape=jax.ShapeDtypeStruct((depth, 2, n), F32),
        compiler_params=_cparams(("parallel", "parallel"), vm),
        name="ada_modulation",
    )(cb, ada_w, ada_b.reshape(depth, 1, n))


def _modnorm_body(x_ref, g_ref, sh_ref, sc_ref, *o_refs):
    x = x_ref[...]
    ms = jnp.mean(x * x, axis=-1, keepdims=True)
    y = x * lax.rsqrt(ms + EPS) * g_ref[...]
    y = y * (1.0 + sc_ref[...]) + sh_ref[...]
    for o in o_refs:
        o[...] = y.astype(o.dtype)


def _modnorm(x, g, shift, scale, out_dtypes, tm=256):
    m, d = x.shape
    tm = min(tm, m)
    row = pl.BlockSpec((1, d), lambda i: (0, 0))
    tile = pl.BlockSpec((tm, d), lambda i: (i, 0))
    vm = 2 * _nbytes((tm, d), F32) * (1 + len(out_dtypes))
    outs = pl.pallas_call(
        _modnorm_body,
        grid=(m // tm,),
        in_specs=[tile, row, row, row],
        out_specs=[tile] * len(out_dtypes),
        out_shape=[jax.ShapeDtypeStruct((m, d), dt) for dt in out_dtypes],
        compiler_params=_cparams(("parallel",), vm),
        name="modnorm",
    )(x, g.reshape(1, d), shift.reshape(1, d), scale.reshape(1, d))
    return outs


def _mm_body(*refs, n_w, n_e, epi, precision):
    x_ref = refs[0]
    w_refs = refs[1:1 + n_w]
    e_refs = refs[1 + n_w:1 + n_w + n_e]
    o_refs = refs[1 + n_w + n_e:]
    x = x_ref[...]
    if precision is None:
        x = x.astype(BF16)
    accs = [jnp.dot(x, w[...], preferred_element_type=F32, precision=precision) for w in w_refs]
    outs = epi(accs, [e[...] for e in e_refs])
    for o, v in zip(o_refs, outs):
        o[...] = v.astype(o.dtype)


def _mm(x, ws, epi, out_dtypes, extras=(), tm=512, tn=512, precision=None, name="mm"):
    m, k = x.shape
    n = ws[0].shape[1]
    tm = min(tm, m)
    tn = min(tn, n)
    assert m % tm == 0 and n % tn == 0, (m, tm, n, tn)
    in_specs = [pl.BlockSpec((tm, k), lambda j, i: (i, 0))]
    in_specs += [pl.BlockSpec((k, tn), lambda j, i: (0, j))] * len(ws)
    vm = 2 * _nbytes((tm, k), x.dtype) + 2 * len(ws) * _nbytes((k, tn), ws[0].dtype)
    for arr, kind, col in extras:
        assert col % tn == 0
        if kind == "tile":
            in_specs.append(pl.BlockSpec((tm, tn), lambda j, i, c=col // tn: (i, j + c)))
            vm += 2 * _nbytes((tm, tn), arr.dtype)
        else:
            in_specs.append(pl.BlockSpec((1, tn), lambda j, i, c=col // tn: (0, j + c)))
    vm += sum(2 * _nbytes((tm, tn), dt) for dt in out_dtypes) + (1 + len(ws)) * _nbytes((tm, tn), F32)
    return pl.pallas_call(
        functools.partial(_mm_body, n_w=len(ws), n_e=len(extras), epi=epi, precision=precision),
        grid=(n // tn, m // tm),
        in_specs=in_specs,
        out_specs=[pl.BlockSpec((tm, tn), lambda j, i: (i, j))] * len(out_dtypes),
        out_shape=[jax.ShapeDtypeStruct((m, n), dt) for dt in out_dtypes],
        compiler_params=_cparams(("parallel", "parallel"), vm),
        name=name,
    )(x, *ws, *[a for a, _, _ in extras])


def _log_sigmoid(z):
    return jnp.minimum(z, 0.0) - jnp.log1p(jnp.exp(-jnp.abs(z)))


def _gla_decay_body(a_ref, awf_ref, awb_ref, abf_ref, abb_ref, ef_ref, eb_ref):
    c = a_ref.shape[0]
    a = a_ref[...]
    r = lax.broadcasted_iota(jnp.int32, (c, c), 0)
    s = lax.broadcasted_iota(jnp.int32, (c, c), 1)
    lower = (s <= r).astype(F32)
    upper = (s >= r).astype(F32)
    zf = jnp.dot(a, awf_ref[...], preferred_element_type=F32, precision=HIGHEST) + abf_ref[...]
    zb = jnp.dot(a, awb_ref[...], preferred_element_type=F32, precision=HIGHEST) + abb_ref[...]
    gf = _log_sigmoid(zf) * (1.0 / GLA_GATE_TEMP)
    gb = _log_sigmoid(zb) * (1.0 / GLA_GATE_TEMP)
    ef_ref[...] = jnp.dot(lower, gf, preferred_element_type=F32, precision=HIGHEST)
    eb_ref[...] = jnp.dot(upper, gb, preferred_element_type=F32, precision=HIGHEST)


def _gla_decay(a, awf, awb, abf, abb, chunk, tn=512):
    l = a.shape[0]
    n = awf.shape[1]
    tn = min(tn, n)
    col = pl.BlockSpec((a.shape[1], tn), lambda i, j: (0, j))
    row = pl.BlockSpec((1, tn), lambda i, j: (0, j))
    out = pl.BlockSpec((chunk, tn), lambda i, j: (i, j))
    vm = 4 * _nbytes((chunk, tn), F32) * 3 + 4 * _nbytes((a.shape[1], tn), F32)
    return pl.pallas_call(
        _gla_decay_body,
        grid=(l // chunk, n // tn),
        in_specs=[pl.BlockSpec((chunk, a.shape[1]), lambda i, j: (i, 0)), col, col, row, row],
        out_specs=[out, out],
        out_shape=[jax.ShapeDtypeStruct((l, n), F32)] * 2,
        compiler_params=_cparams(("parallel", "parallel"), vm),
        name="gla_decay",
    )(a, awf, awb, abf, abb)


def _dot_nt(a, b):
    return lax.dot_general(a, b, (((1,), (1,)), ((), ())), preferred_element_type=F32)


def _dot_tn(a, b):
    return lax.dot_general(a, b, (((0,), (0,)), ((), ())), preferred_element_type=F32)


def _bcast_rows(e, group, row):
    c, w = e.shape
    e3 = e.reshape(c // group, group, w)
    return jnp.broadcast_to(e3[:, row:row + 1, :], e3.shape).reshape(c, w)


def _gla_masks(c, base, fwd):
    i = lax.broadcasted_iota(jnp.int32, (c, c), 0)
    j = lax.broadcasted_iota(jnp.int32, (c, c), 1)
    sh = int(math.log2(base))
    order = (j <= i) if fwd else (j >= i)
    masks = [((i >> sh) == (j >> sh)) & order]
    s = base
    while 2 * s <= c:
        sh += 1
        masks.append((i >> sh) == (j >> sh))
        s *= 2
    return masks


def _gla_chunk_head(q, k, v, e, st, masks, *, fwd, base):
    c, dk = q.shape
    row = lax.broadcasted_iota(jnp.int32, (c, 1), 0)
    d0 = e - _bcast_rows(e, base, base // 2 - 1 if fwd else base // 2)
    q0 = (q * jnp.exp(d0)).astype(BF16)
    k0 = (k * jnp.exp(-d0)).astype(BF16)
    att = jnp.where(masks[0], _dot_nt(q0, k0), 0.0)
    s, lvl = base, 1
    while 2 * s <= c:
        d = e - _bcast_rows(e, 2 * s, s - 1 if fwd else s)
        later = ((row >> int(math.log2(s))) & 1) == (1 if fwd else 0)
        x = jnp.exp(jnp.where(later, d, -d))
        ql = jnp.where(later, q * x, 0.0).astype(BF16)
        kl = jnp.where(later, 0.0, k * x).astype(BF16)
        att = att + jnp.where(masks[lvl], _dot_nt(ql, kl), 0.0)
        s *= 2
        lvl += 1
    e_edge = e[c - 1:c] if fwd else e[0:1]
    qs = (q * jnp.exp(e)).astype(BF16)
    ks = (k * jnp.exp(e_edge - e)).astype(BF16)
    o = jnp.dot(att.astype(BF16), v, preferred_element_type=F32) + _dot_nt(qs, st.astype(BF16))
    st_new = st * jnp.exp(e_edge) + _dot_tn(v, ks)
    return o, st_new


def _gla_scan_body(*refs, fwd, base, combine):
    if combine:
        q_ref, k_ref, v_ref, e_ref, s0_ref, of_ref, sg_ref, gn_ref, o_ref, sfin_ref, st_ref = refs
    else:
        q_ref, k_ref, v_ref, e_ref, s0_ref, o_ref, sfin_ref, st_ref = refs
    step = pl.program_id(0)
    c = q_ref.shape[0]
    dk = q_ref.shape[1] // GLA_HEADS
    dv = v_ref.shape[1] // GLA_HEADS

    @pl.when(step == 0)
    def _():
        st_ref[...] = s0_ref[...]

    masks = _gla_masks(c, base, fwd)
    for h in range(GLA_HEADS):
        ks = slice(h * dk, (h + 1) * dk)
        vs = slice(h * dv, (h + 1) * dv)
        o, st_new = _gla_chunk_head(
            q_ref[:, ks].astype(F32), k_ref[:, ks].astype(F32), v_ref[:, vs], e_ref[:, ks], st_ref[h],
            masks, fwd=fwd, base=base)
        st_ref[h] = st_new
        if combine:
            t = o + of_ref[:, vs]
            t = t * lax.rsqrt(jnp.mean(t * t, axis=-1, keepdims=True) + EPS) * gn_ref[...]
            o_ref[:, vs] = (t * sg_ref[:, vs].astype(F32)).astype(o_ref.dtype)
        else:
            o_ref[:, vs] = o.astype(o_ref.dtype)

    @pl.when(step == pl.num_programs(0) - 1)
    def _():
        sfin_ref[...] = st_ref[...]


def _gla_scan(q, k, v, e, s0, *, fwd, chunk, base=16, combine=None):
    l, hdk = q.shape
    hdv = v.shape[1]
    dk, dv = hdk // GLA_HEADS, hdv // GLA_HEADS
    chunk = min(chunk, l)
    n = l // chunk
    idx = (lambda i: (i, 0)) if fwd else (lambda i: (n - 1 - i, 0))
    st_spec = pl.BlockSpec((GLA_HEADS, dv, dk), lambda i: (0, 0, 0))
    in_specs = [pl.BlockSpec((chunk, hdk), idx), pl.BlockSpec((chunk, hdk), idx),
                pl.BlockSpec((chunk, hdv), idx), pl.BlockSpec((chunk, hdk), idx), st_spec]
    args = [q, k, v, e, s0]
    vm = 2 * (2 * _nbytes((chunk, hdk), BF16) + _nbytes((chunk, hdv), BF16) + _nbytes((chunk, hdk), F32))
    vm += 3 * _nbytes((GLA_HEADS, dv, dk), F32) * 2 + 2 * _nbytes((chunk, hdv), F32)
    if combine is not None:
        o_other, gate, norm_g = combine
        in_specs += [pl.BlockSpec((chunk, hdv), idx), pl.BlockSpec((chunk, hdv), idx),
                     pl.BlockSpec((1, dv), lambda i: (0, 0))]
        args += [o_other, gate, norm_g.reshape(1, dv)]
        vm += 2 * (_nbytes((chunk, hdv), F32) + _nbytes((chunk, hdv), BF16))
    vm += 24 * _nbytes((chunk, max(dk, chunk)), F32)
    return pl.pallas_call(
        functools.partial(_gla_scan_body, fwd=fwd, base=base, combine=combine is not None),
        grid=(n,),
        in_specs=in_specs,
        out_specs=[pl.BlockSpec((chunk, hdv), idx), st_spec],
        out_shape=[jax.ShapeDtypeStruct((l, hdv), BF16 if combine is not None else F32),
                   jax.ShapeDtypeStruct((GLA_HEADS, dv, dk), F32)],
        scratch_shapes=[pltpu.VMEM((GLA_HEADS, dv, dk), F32)],
        compiler_params=_cparams(("arbitrary",), vm),
        name="gla_scan_fwd" if fwd else "gla_scan_bwd",
    )(*args)


def _shortconv_body(u0, u1, u2, p0, p1, p2, n0, n1, n2, w0, w1, w2, b0, b1, b2, x0_ref, z_ref):
    i = pl.program_id(0)
    last = pl.num_programs(0) - 1
    tm = u0.shape[0]
    row = lax.broadcasted_iota(jnp.int32, (tm, 1), 0)

    def conv(u_ref, p_ref, n_ref, w_ref, b_ref):
        u = u_ref[...]
        prev_row = jnp.where(i == 0, 0.0, p_ref[SUBLANES - 1:SUBLANES, :])
        next_row = jnp.where(i == last, 0.0, n_ref[0:1, :])
        before = jnp.where(row == 0, prev_row, pltpu.roll(u, 1, axis=0))
        after = jnp.where(row == tm - 1, next_row, pltpu.roll(u, tm - 1, axis=0))
        return b_ref[...] + before * w_ref[0:1, :] + u * w_ref[1:2, :] + after * w_ref[2:3, :]

    x0_ref[...] = conv(u0, p0, n0, w0, b0)
    z_ref[...] = conv(u1, p1, n1, w1, b1) * conv(u2, p2, n2, w2, b2)


def _hyena_shortconv(hy, conv_w, conv_b, tm=512, cb=512):
    l, w3 = hy.shape
    w = w3 // 3
    tm = min(tm, l)
    cb = min(cb, w)
    nb = w // cb
    hb = tm // SUBLANES
    nrow8 = l // SUBLANES
    cur = [pl.BlockSpec((tm, cb), lambda i, j, g=g: (i, g * nb + j)) for g in range(3)]
    prv = [pl.BlockSpec((SUBLANES, cb), lambda i, j, g=g: (jnp.maximum(i * hb - 1, 0), g * nb + j)) for g in range(3)]
    nxt = [pl.BlockSpec((SUBLANES, cb), lambda i, j, g=g: (jnp.minimum((i + 1) * hb, nrow8 - 1), g * nb + j))
           for g in range(3)]
    wsp = [pl.BlockSpec((HY_SHORT, cb), lambda i, j, g=g: (0, g * nb + j)) for g in range(3)]
    bsp = [pl.BlockSpec((1, cb), lambda i, j, g=g: (0, g * nb + j)) for g in range(3)]
    out = pl.BlockSpec((tm, cb), lambda i, j: (i, j))
    vm = 2 * 5 * _nbytes((tm, cb), F32) + 8 * _nbytes((tm, cb), F32)
    return pl.pallas_call(
        _shortconv_body,
        grid=(l // tm, nb),
        in_specs=cur + prv + nxt + wsp + bsp,
        out_specs=[out, out],
        out_shape=[jax.ShapeDtypeStruct((l, w), F32)] * 2,
        compiler_params=_cparams(("parallel", "parallel"), vm),
        name="hyena_shortconv",
    )(hy, hy, hy, hy, hy, hy, hy, hy, hy, conv_w, conv_w, conv_w,
      conv_b.reshape(1, w3), conv_b.reshape(1, w3), conv_b.reshape(1, w3))


def _filter_body(w1_ref, b1_ref, w2_ref, b2_ref, w3_ref, b3_ref, fr_ref, w4f_ref, w4b_ref,
                 hf_ref, hg_ref, ss_ref, *, seq_len):
    i = pl.program_id(0)
    tr = hf_ref.shape[0]
    wdt = hf_ref.shape[1]
    pos = (lax.broadcasted_iota(jnp.int32, (tr, 1), 0) + i * tr).astype(F32)
    t = pos / float(max(seq_len - 1, 1))
    lane = lax.broadcasted_iota(jnp.int32, (1, LANES), 1)
    band = ((lane - 1) & (HY_EMB_BANDS - 1)).astype(F32)
    bands = 1e-4 + band * ((HY_EMB_BANDS - 1 - 1e-4) / (HY_EMB_BANDS - 1))
    ang = ((2.0 * math.pi / seq_len) * pos) * bands
    trig = jnp.cos(ang + jnp.where(lane > HY_EMB_BANDS, 0.5 * math.pi, 0.0))
    emb = jnp.where(lane == 0, t, jnp.where(lane <= 2 * HY_EMB_BANDS, trig, 0.0))
    fr = fr_ref[...]
    h = jnp.sin(fr * (jnp.dot(emb, w1_ref[...], preferred_element_type=F32, precision=HIGHEST) + b1_ref[...]))
    h = jnp.sin(fr * (jnp.dot(h, w2_ref[...], preferred_element_type=F32, precision=HIGHEST) + b2_ref[...]))
    h = jnp.sin(fr * (jnp.dot(h, w3_ref[...], preferred_element_type=F32, precision=HIGHEST) + b3_ref[...]))
    ch = lax.broadcasted_iota(jnp.int32, (1, wdt), 1).astype(F32)
    lo = math.log(HY_DECAY_TARGET) / HY_SLOW_DECAY
    hi = math.log(HY_DECAY_TARGET) / HY_FAST_DECAY
    deltas = jnp.abs(lo + ch * ((hi - lo) / (wdt - 1)))
    window = jnp.exp(-t * deltas)
    hf = jnp.dot(h, w4f_ref[...], preferred_element_type=F32, precision=HIGHEST) * window
    hg = jnp.dot(h, w4b_ref[...], preferred_element_type=F32, precision=HIGHEST) * window
    hg = jnp.where(pos == 0.0, 0.0, hg)
    hf_ref[...] = hf
    hg_ref[...] = hg

    @pl.when(i == 0)
    def _():
        ss_ref[...] = jnp.zeros_like(ss_ref)

    ss_ref[...] += jnp.sum(hf * hf + hg * hg, axis=0, keepdims=True)


def _pad2(a, rows, cols):
    return jnp.zeros((rows, cols), a.dtype).at[:a.shape[0], :a.shape[1]].set(a)


def _hyena_filter(seq_len, p, tr=256):
    wdt = p["hy_w4"].shape[1] // 2
    tr = min(tr, seq_len)
    hid = LANES
    w1 = _pad2(p["hy_w1"], LANES, hid)
    w2 = _pad2(p["hy_w2"], hid, hid)
    w3 = _pad2(p["hy_w3"], hid, hid)
    b1, b2, b3, fr = (_pad2(p[k].reshape(1, -1), 1, hid) for k in ("hy_b1", "hy_b2", "hy_b3", "hy_freq"))
    w4f = _pad2(p["hy_w4"][:, :wdt], hid, wdt)
    w4b = _pad2(p["hy_w4"][:, wdt:], hid, wdt)
    full = lambda a: pl.BlockSpec(a.shape, lambda i: (0, 0))
    out = pl.BlockSpec((tr, wdt), lambda i: (i, 0))
    args = (w1, b1, w2, b2, w3, b3, fr, w4f, w4b)
    vm = 4 * _nbytes((hid, wdt), F32) + 8 * _nbytes((tr, wdt), F32)
    return pl.pallas_call(
        functools.partial(_filter_body, seq_len=seq_len),
        grid=(seq_len // tr,),
        in_specs=[full(a) for a in args],
        out_specs=[out, out, pl.BlockSpec((1, wdt), lambda i: (0, 0))],
        out_shape=[jax.ShapeDtypeStruct((seq_len, wdt), F32)] * 2 + [jax.ShapeDtypeStruct((1, wdt), F32)],
        compiler_params=_cparams(("arbitrary",), vm),
        name="hyena_filter",
    )(*args)


def _fft_dims(seq_len):
    n = 2 * seq_len
    p = 1 << ((n.bit_length() - 1) // 2)
    return p, n // p


@functools.lru_cache(maxsize=None)
def _fft_consts(seq_len):
    pp, mm = _fft_dims(seq_len)
    n = pp * mm
    a = np.arange(pp // 2)
    b = np.arange(mm)
    d = np.arange(pp)
    ang = -2 * np.pi * (d[None, :, None] * a[None, None, :] / pp + b[:, None, None] * d[None, :, None] / n)
    f1 = np.concatenate([np.cos(ang), np.sin(ang)], axis=1)
    angc = -2 * np.pi * np.outer(b, b) / mm
    cr, ci = np.cos(angc), np.sin(angc)
    w2 = np.block([[cr, -ci], [ci, cr]])
    v2 = np.block([[cr, ci], [-ci, cr]])
    angl = 2 * np.pi * (a[None, :, None] * d[None, None, :] / pp + b[:, None, None] * d[None, None, :] / n)
    g1 = np.concatenate([np.cos(angl), -np.sin(angl)], axis=2) / n
    as_bf16 = lambda x: jnp.asarray(x, dtype=F32).astype(BF16)
    return as_bf16(f1), as_bf16(w2), as_bf16(v2), as_bf16(g1)


def _fft_s1_body(z_ref, f_ref, o_ref):
    for bl in range(SUBLANES):
        o_ref[bl] = jnp.dot(f_ref[bl], z_ref[:, bl, :].astype(BF16), preferred_element_type=F32)


def _fft_stage1(z, f1, cb=256):
    l, c = z.shape
    mm, p2, ph = f1.shape
    cb = min(cb, c)
    z3 = z.reshape(ph, mm, c)
    vm = 2 * (_nbytes((ph, SUBLANES, cb), F32) + _nbytes((SUBLANES, p2, ph), BF16) + _nbytes((SUBLANES, p2, cb), F32))
    return pl.pallas_call(
        _fft_s1_body,
        grid=(c // cb, mm // SUBLANES),
        in_specs=[pl.BlockSpec((ph, SUBLANES, cb), lambda j, b: (0, b, j)),
                  pl.BlockSpec((SUBLANES, p2, ph), lambda j, b: (b, 0, 0))],
        out_specs=pl.BlockSpec((SUBLANES, p2, cb), lambda j, b: (b, 0, j)),
        out_shape=jax.ShapeDtypeStruct((mm, p2, c), F32),
        compiler_params=_cparams(("parallel", "parallel"), vm),
        name="fft_stage1",
    )(z3, f1)


def _stack_ri(re_ref, im_ref, dl):
    return jnp.concatenate([re_ref[:, dl, :], im_ref[:, dl, :]], axis=0).astype(BF16)


def _fft_spec_body(fre, fim, gre, gim, w2_ref, o_ref):
    mm = fre.shape[0]
    for dl in range(SUBLANES):
        hf = jnp.dot(w2_ref[...], _stack_ri(fre, fim, dl), preferred_element_type=F32)
        hg = jnp.dot(w2_ref[...], _stack_ri(gre, gim, dl), preferred_element_type=F32)
        o_ref[dl, :mm, :] = hf[:mm] + hg[:mm]
        o_ref[dl, mm:, :] = hf[mm:] - hg[mm:]


def _fft_mid_body(are, aim, h_ref, w2_ref, v2_ref, o_ref):
    mm = are.shape[0]
    for dl in range(SUBLANES):
        x = jnp.dot(w2_ref[...], _stack_ri(are, aim, dl), preferred_element_type=F32)
        xr, xi = x[:mm], x[mm:]
        hr, hi = h_ref[dl, :mm, :], h_ref[dl, mm:, :]
        y = jnp.concatenate([xr * hr - xi * hi, xr * hi + xi * hr], axis=0).astype(BF16)
        o_ref[dl] = jnp.dot(v2_ref[...], y, preferred_element_type=F32)


def _ri_specs(mm, pp, cb):
    nd = pp // SUBLANES
    return [pl.BlockSpec((mm, SUBLANES, cb), lambda j, d: (0, d, j)),
            pl.BlockSpec((mm, SUBLANES, cb), lambda j, d: (0, nd + d, j))]


def _fft_filter_spectrum(af, ag, w2, cb=256):
    mm, p2, c = af.shape
    pp = p2 // 2
    cb = min(cb, c)
    const = pl.BlockSpec(w2.shape, lambda j, d: (0, 0))
    out = pl.BlockSpec((SUBLANES, 2 * mm, cb), lambda j, d: (d, 0, j))
    vm = 2 * (4 * _nbytes((mm, SUBLANES, cb), F32) + _nbytes((SUBLANES, 2 * mm, cb), F32)) + 6 * _nbytes((2 * mm, cb), F32)
    return pl.pallas_call(
        _fft_spec_body,
        grid=(c // cb, pp // SUBLANES),
        in_specs=_ri_specs(mm, pp, cb) + _ri_specs(mm, pp, cb) + [const],
        out_specs=out,
        out_shape=jax.ShapeDtypeStruct((pp, 2 * mm, c), F32),
        compiler_params=_cparams(("parallel", "parallel"), vm),
        name="fft_filter_spectrum",
    )(af, af, ag, ag, w2)


def _fft_mid(a, h, w2, v2, cb=256):
    mm, p2, c = a.shape
    pp = p2 // 2
    cb = min(cb, c)
    const = pl.BlockSpec(w2.shape, lambda j, d: (0, 0))
    slab = pl.BlockSpec((SUBLANES, 2 * mm, cb), lambda j, d: (d, 0, j))
    vm = 2 * (2 * _nbytes((mm, SUBLANES, cb), F32) + 2 * _nbytes((SUBLANES, 2 * mm, cb), F32)) + 8 * _nbytes((2 * mm, cb), F32)
    return pl.pallas_call(
        _fft_mid_body,
        grid=(c // cb, pp // SUBLANES),
        in_specs=_ri_specs(mm, pp, cb) + [slab, const, const],
        out_specs=slab,
        out_shape=jax.ShapeDtypeStruct((pp, 2 * mm, c), F32),
        compiler_params=_cparams(("parallel", "parallel"), vm),
        name="fft_mid",
    )(a, a, h, w2, v2)


def _fft_last_body(bre, bim, g_ref, x0_ref, z_ref, ss_ref, bias_ref, o_ref):
    scale = lax.rsqrt(ss_ref[...] + EPS)
    for bl in range(SUBLANES):
        y = jnp.dot(g_ref[bl], _stack_ri(bre, bim, bl), preferred_element_type=F32)
        z = z_ref[:, bl, :]
        o_ref[:, bl, :] = x0_ref[:, bl, :] * (y * scale + z * bias_ref[...])


def _fft_last(bmat, g1, x0, z, ss, bias, cb=256):
    pp, m2, c = bmat.shape
    mm = m2 // 2
    ph = g1.shape[1]
    cb = min(cb, c)
    nb = mm // SUBLANES
    view = pl.BlockSpec((ph, SUBLANES, cb), lambda j, b: (0, b, j))
    row = pl.BlockSpec((1, cb), lambda j, b: (0, j))
    vm = 2 * (2 * _nbytes((pp, SUBLANES, cb), F32) + 3 * _nbytes((ph, SUBLANES, cb), F32)
              + _nbytes((SUBLANES, ph, 2 * pp), BF16)) + 4 * _nbytes((2 * pp, cb), F32)
    out = pl.pallas_call(
        _fft_last_body,
        grid=(c // cb, nb),
        in_specs=[pl.BlockSpec((pp, SUBLANES, cb), lambda j, b: (0, b, j)),
                  pl.BlockSpec((pp, SUBLANES, cb), lambda j, b: (0, nb + b, j)),
                  pl.BlockSpec((SUBLANES, ph, 2 * pp), lambda j, b: (b, 0, 0)),
                  view, view, row, row],
        out_specs=view,
        out_shape=jax.ShapeDtypeStruct((ph, mm, c), F32),
        compiler_params=_cparams(("parallel", "parallel"), vm),
        name="fft_last",
    )(bmat, bmat, g1, x0.reshape(ph, mm, c), z.reshape(ph, mm, c), ss, bias.reshape(1, c))
    return out.reshape(ph * mm, c)


def _hyena_longconv(x0, z, hf, hg, ss, bias):
    f1, w2, v2, g1 = _fft_consts(z.shape[0])
    h = _fft_filter_spectrum(_fft_stage1(hf, f1), _fft_stage1(hg, f1), w2)
    return _fft_last(_fft_mid(_fft_stage1(z, f1), h, w2, v2), g1, x0, z, ss, bias)


def _router_body(h_ref, w_ref, o_ref):
    logits = jnp.dot(h_ref[...], w_ref[...], preferred_element_type=F32, precision=HIGHEST)
    lane = lax.broadcasted_iota(jnp.int32, logits.shape, 1)
    lg = jnp.where(lane < N_EXPERTS, logits, -jnp.inf)
    m1 = jnp.max(lg, axis=-1, keepdims=True)
    i1 = jnp.min(jnp.where(lg == m1, lane, LANES), axis=-1, keepdims=True)
    l2 = jnp.where(lane == i1, -jnp.inf, lg)
    m2 = jnp.max(l2, axis=-1, keepdims=True)
    i2 = jnp.min(jnp.where(l2 == m2, lane, LANES), axis=-1, keepdims=True)
    e = jnp.exp(m2 - m1)
    w1 = 1.0 / (1.0 + e)
    w2 = e * w1
    o_ref[...] = jnp.where(lane == 0, i1.astype(F32), jnp.where(lane == 1, i2.astype(F32),
                           jnp.where(lane == 2, w1, jnp.where(lane == 3, w2, 0.0))))


def _router(h, router_w, tm=256):
    m, d = h.shape
    tm = min(tm, m)
    w = _pad2(router_w, d, LANES)
    vm = 2 * (_nbytes((tm, d), F32) + _nbytes((d, LANES), F32)) + 8 * _nbytes((tm, LANES), F32)
    return pl.pallas_call(
        _router_body,
        grid=(m // tm,),
        in_specs=[pl.BlockSpec((tm, d), lambda i: (i, 0)), pl.BlockSpec((d, LANES), lambda i: (0, 0))],
        out_specs=pl.BlockSpec((tm, LANES), lambda i: (i, 0)),
        out_shape=jax.ShapeDtypeStruct((m, LANES), F32),
        compiler_params=_cparams(("parallel",), vm),
        name="moe_router",
    )(h, w)


def _routing_tables(route, tile):
    t = route.shape[0]
    e_flat = route[:, :TOP_K].astype(jnp.int32).reshape(-1)
    w_flat = route[:, TOP_K:2 * TOP_K].reshape(-1)
    onehot = (e_flat[:, None] == jnp.arange(N_EXPERTS, dtype=jnp.int32)[None, :]).astype(jnp.int32)
    csum = jnp.cumsum(onehot, axis=0)
    rank = jnp.take_along_axis(csum, e_flat[:, None], axis=1)[:, 0] - 1
    counts = csum[-1]
    padded = ((counts + tile - 1) // tile) * tile
    ends = jnp.cumsum(padded)
    pos = (ends - padded)[e_flat] + rank
    n_tiles = (t * TOP_K) // tile + N_EXPERTS
    rows = n_tiles * tile
    row_token = jnp.zeros((rows,), jnp.int32).at[pos].set(jnp.arange(t * TOP_K, dtype=jnp.int32) // TOP_K)
    row_w = jnp.zeros((rows,), F32).at[pos].set(w_flat)
    start = jnp.arange(n_tiles, dtype=jnp.int32) * tile
    valid = start < ends[-1]
    expert = jnp.minimum(jnp.searchsorted(ends, start, side="right").astype(jnp.int32), N_EXPERTS - 1)
    last_valid = jnp.max(jnp.where(valid, expert, 0))
    expert = jnp.where(valid, expert, last_valid)
    first = jnp.concatenate([jnp.ones((1,), jnp.int32), (expert[1:] != expert[:-1]).astype(jnp.int32)])
    return row_token, row_w, pos.astype(jnp.int32), expert, first, valid.astype(jnp.int32)


def _row_copy(src_hbm, dst, sem, src_row, dst_row):
    return pltpu.make_async_copy(src_hbm.at[pl.ds(src_row, 1)], dst.at[pl.ds(dst_row, 1)], sem)


def _gather_rows_body(tok_ref, h_hbm, o_ref, buf, sem):
    i = pl.program_id(0)
    gt = buf.shape[0]

    def issue(r, carry):
        _row_copy(h_hbm, buf, sem, tok_ref[i * gt + r], r).start()
        return carry

    def drain(r, carry):
        _row_copy(h_hbm, buf, sem, 0, r).wait()
        return carry

    lax.fori_loop(0, gt, issue, 0)
    lax.fori_loop(0, gt, drain, 0)
    o_ref[...] = buf[...].astype(o_ref.dtype)


def _gather_rows(h, row_token, gt=256):
    rows = row_token.shape[0]
    d = h.shape[1]
    vm = 3 * _nbytes((gt, d), F32)
    return pl.pallas_call(
        _gather_rows_body,
        grid_spec=pltpu.PrefetchScalarGridSpec(
            num_scalar_prefetch=1,
            grid=(rows // gt,),
            in_specs=[pl.BlockSpec(memory_space=pl.ANY)],
            out_specs=pl.BlockSpec((gt, d), lambda i, tok: (i, 0)),
            scratch_shapes=[pltpu.VMEM((gt, d), F32), pltpu.SemaphoreType.DMA(())],
        ),
        out_shape=jax.ShapeDtypeStruct((rows, d), BF16),
        compiler_params=_cparams(("arbitrary",), vm),
        name="moe_gather_rows",
    )(row_token, h)


def _moe_up_body(te_ref, tf_ref, tv_ref, x_ref, wg_ref, wu_ref, o_ref, wg_bf, wu_bf):
    i = pl.program_id(1)

    @pl.when(tf_ref[i] == 1)
    def _():
        wg_bf[...] = wg_ref[...].astype(BF16)
        wu_bf[...] = wu_ref[...].astype(BF16)

    @pl.when(tv_ref[i] == 1)
    def _():
        x = x_ref[...]
        g = jnp.dot(x, wg_bf[...], preferred_element_type=F32)
        u = jnp.dot(x, wu_bf[...], preferred_element_type=F32)
        o_ref[...] = (g * jax.nn.sigmoid(g) * u).astype(o_ref.dtype)

    @pl.when(tv_ref[i] == 0)
    def _():
        o_ref[...] = jnp.zeros_like(o_ref)


def _moe_up(xs, wg, wu, expert, first, valid, tile, tn=512):
    rows, d = xs.shape
    f = wg.shape[2]
    wspec = pl.BlockSpec((None, d, tn), lambda j, i, te, tf, tv: (te[i], 0, j))
    vm = 4 * _nbytes((d, tn), F32) + 2 * _nbytes((d, tn), BF16) + 2 * _nbytes((tile, d), BF16) + 6 * _nbytes((tile, tn), F32)
    return pl.pallas_call(
        _moe_up_body,
        grid_spec=pltpu.PrefetchScalarGridSpec(
            num_scalar_prefetch=3,
            grid=(f // tn, rows // tile),
            in_specs=[pl.BlockSpec((tile, d), lambda j, i, te, tf, tv: (i, 0)), wspec, wspec],
            out_specs=pl.BlockSpec((tile, tn), lambda j, i, te, tf, tv: (i, j)),
            scratch_shapes=[pltpu.VMEM((d, tn), BF16), pltpu.VMEM((d, tn), BF16)],
        ),
        out_shape=jax.ShapeDtypeStruct((rows, f), BF16),
        compiler_params=_cparams(("arbitrary", "arbitrary"), vm),
        name="moe_up",
    )(expert, first, valid, xs, wg, wu)


def _moe_down_body(te_ref, tf_ref, tv_ref, a_ref, wd_ref, rw_ref, o_ref, wd_bf):
    i = pl.program_id(1)

    @pl.when(tf_ref[i] == 1)
    def _():
        wd_bf[...] = wd_ref[...].astype(BF16)

    @pl.when(tv_ref[i] == 1)
    def _():
        o_ref[...] = jnp.dot(a_ref[...], wd_bf[...], preferred_element_type=F32) * rw_ref[...]

    @pl.when(tv_ref[i] == 0)
    def _():
        o_ref[...] = jnp.zeros_like(o_ref)


def _moe_down(act, wd, row_w, expert, first, valid, tile, tn=256):
    rows, f = act.shape
    d = wd.shape[2]
    vm = 2 * _nbytes((f, tn), F32) + _nbytes((f, tn), BF16) + 2 * _nbytes((tile, f), BF16) + 4 * _nbytes((tile, tn), F32)
    return pl.pallas_call(
        _moe_down_body,
        grid_spec=pltpu.PrefetchScalarGridSpec(
            num_scalar_prefetch=3,
            grid=(d // tn, rows // tile),
            in_specs=[pl.BlockSpec((tile, f), lambda j, i, te, tf, tv: (i, 0)),
                      pl.BlockSpec((None, f, tn), lambda j, i, te, tf, tv: (te[i], 0, j)),
                      pl.BlockSpec((tile, 1), lambda j, i, te, tf, tv: (i, 0))],
            out_specs=pl.BlockSpec((tile, tn), lambda j, i, te, tf, tv: (i, j)),
            scratch_shapes=[pltpu.VMEM((f, tn), BF16)],
        ),
        out_shape=jax.ShapeDtypeStruct((rows, d), F32),
        compiler_params=_cparams(("arbitrary", "arbitrary"), vm),
        name="moe_down",
    )(expert, first, valid, act, wd, row_w.reshape(rows, 1))


def _moe_combine_body(pos_ref, ys_hbm, x_ref, gate_ref, g_ref, o_ref, buf, sem, *, final_norm):
    i = pl.program_id(0)
    gt = x_ref.shape[0]

    def issue(r, carry):
        for k in range(TOP_K):
            _row_copy(ys_hbm, buf.at[k], sem, pos_ref[TOP_K * (i * gt + r) + k], r).start()
        return carry

    def drain(r, carry):
        for k in range(TOP_K):
            _row_copy(ys_hbm, buf.at[k], sem, 0, r).wait()
        return carry

    lax.fori_loop(0, gt, issue, 0)
    lax.fori_loop(0, gt, drain, 0)
    y = buf[0]
    for k in range(1, TOP_K):
        y = y + buf[k]
    x = x_ref[...] + gate_ref[...] * y
    if final_norm:
        x = x * lax.rsqrt(jnp.mean(x * x, axis=-1, keepdims=True) + EPS) * g_ref[...]
    o_ref[...] = x


def _moe_combine(ys, pos, x, gate, norm_g, final_norm, gt=128):
    t, d = x.shape
    gt = min(gt, t)
    row = pl.BlockSpec((1, d), lambda i, p: (0, 0))
    tilespec = pl.BlockSpec((gt, d), lambda i, p: (i, 0))
    vm = (TOP_K + 6) * _nbytes((gt, d), F32)
    return pl.pallas_call(
        functools.partial(_moe_combine_body, final_norm=final_norm),
        grid_spec=pltpu.PrefetchScalarGridSpec(
            num_scalar_prefetch=1,
            grid=(t // gt,),
            in_specs=[pl.BlockSpec(memory_space=pl.ANY), tilespec, row, row],
            out_specs=tilespec,
            scratch_shapes=[pltpu.VMEM((TOP_K, gt, d), F32), pltpu.SemaphoreType.DMA(())],
        ),
        out_shape=jax.ShapeDtypeStruct((t, d), F32),
        compiler_params=_cparams(("arbitrary",), vm),
        name="moe_combine",
    )(pos, ys, x, gate.reshape(1, d), norm_g.reshape(1, d))


MOE_ROW_TILE = 256


def _moe_ffn(h32, x, gate, p, norm_g, final_norm):
    route = _router(h32, p["router_w"])
    row_token, row_w, pos, expert, first, valid = _routing_tables(route, MOE_ROW_TILE)
    xs = _gather_rows(h32, row_token)
    act = _moe_up(xs, p["exp_w_gate"], p["exp_w_up"], expert, first, valid, MOE_ROW_TILE)
    ys = _moe_down(act, p["exp_w_down"], row_w, expert, first, valid, MOE_ROW_TILE)
    return _moe_combine(ys, pos, x, gate, norm_g, final_norm)


def _silu(v):
    return v * jax.nn.sigmoid(v)


def _epi_plain(accs, ex):
    return [accs[0]]


def _in_weights(w_in, d):
    qk, vw, r = d // 2, d, GLA_GATE_RANK
    col_v = qk
    col_a = col_v + vw
    col_q = col_a + 2 * r
    col_g = col_q + qk
    col_hy = col_g + vw
    col_gate = col_hy + 3 * d
    cut = lambda a, b: w_in[:, a:b].astype(BF16)
    return dict(k=cut(0, col_v), v=cut(col_v, col_a), a=_pad2(cut(col_a, col_q), d, LANES),
                q=cut(col_q, col_g), g=cut(col_g, col_hy), hy=cut(col_hy, col_gate),
                gate=cut(col_gate, col_gate + 2 * d))


def _gate_matrices(p, d):
    r = GLA_GATE_RANK
    qk = d // 2
    awf = jnp.zeros((LANES, qk), F32).at[:r].set(p["gla_aw_f"])
    awb = jnp.zeros((LANES, qk), F32).at[r:2 * r].set(p["gla_aw_b"])
    return awf, awb, p["gla_ab_f"].reshape(1, qk), p["gla_ab_b"].reshape(1, qk)


GLA_CHUNK = 256


def _gla_states_only(h, wi, p, d):
    k, = _mm(h, [wi["k"]], _epi_plain, (BF16,), name="proj_k")
    v, = _mm(h, [wi["v"]], _epi_plain, (BF16,), name="proj_v")
    a, = _mm(h, [wi["a"]], _epi_plain, (F32,), name="proj_a")
    ef, eb = _gla_decay(a, *_gate_matrices(p, d), chunk=min(GLA_CHUNK, h.shape[0]))
    dk, dv = d // 2 // GLA_HEADS, d // GLA_HEADS
    zero = jnp.zeros((GLA_HEADS, dv, dk), F32)
    _, s_f = _gla_scan(k, k, v, ef, zero, fwd=True, chunk=GLA_CHUNK)
    _, s_b = _gla_scan(k, k, v, eb, zero, fwd=False, chunk=GLA_CHUNK)
    return s_f, s_b


def _token_mixer(x, h, mod_gate, s0_f, s0_b, wi, p, d):
    l = h.shape[0]
    dk = d // 2 // GLA_HEADS
    k, = _mm(h, [wi["k"]], _epi_plain, (BF16,), name="proj_k")
    v, = _mm(h, [wi["v"]], _epi_plain, (BF16,), name="proj_v")
    a, = _mm(h, [wi["a"]], _epi_plain, (F32,), name="proj_a")
    q, = _mm(h, [wi["q"]], lambda accs, ex: [accs[0] * (dk ** -0.5)], (BF16,), name="proj_q")
    sg, = _mm(h, [wi["g"]], lambda accs, ex: [_silu(accs[0])], (BF16,), name="proj_g")
    hy, = _mm(h, [wi["hy"]], _epi_plain, (F32,), tn=1024, name="proj_hy")
    gates, = _mm(h, [wi["gate"]], lambda accs, ex: [jax.nn.sigmoid(accs[0])], (BF16,), tn=1024, name="proj_gate")

    ef, eb = _gla_decay(a, *_gate_matrices(p, d), chunk=min(GLA_CHUNK, l))
    o_f, s_f = _gla_scan(q, k, v, ef, s0_f, fwd=True, chunk=GLA_CHUNK)
    o_gla, s_b = _gla_scan(q, k, v, eb, s0_b, fwd=False, chunk=GLA_CHUNK, combine=(o_f, sg, p["gla_norm_g"]))

    x0, z = _hyena_shortconv(hy, p["hy_conv_w"], p["hy_conv_b"])
    hf, hg, ss = _hyena_filter(l, p)
    o_hy = _hyena_longconv(x0, z, hf, hg, ss, p["hy_bias"])

    t1, = _mm(o_hy, [p["w_up_hy"].astype(BF16)], lambda accs, ex: [ex[0].astype(F32) * accs[0]], (F32,),
              extras=[(gates, "tile", 0)], name="up_hy")
    merged, = _mm(o_gla, [p["w_up_gla"].astype(BF16)],
                  lambda accs, ex: [ex[1] + ex[0].astype(F32) * accs[0]], (BF16,),
                  extras=[(gates, "tile", d), (t1, "tile", 0)], name="up_gla_merge")
    x_new, = _mm(merged, [p["w_out"].astype(BF16)], lambda accs, ex: [ex[0] + ex[1] * accs[0]], (F32,),
                 extras=[(x, "tile", 0), (mod_gate.reshape(1, d), "row", 0)], name="out_proj")
    return x_new, s_f, s_b


def _dense_ffn(x, h, mod_gate, wg, wu, wd):
    d, f = wg.shape
    fp = -(-f // 512) * 512
    wg_b = _pad2(wg.astype(BF16), d, fp)
    wu_b = _pad2(wu.astype(BF16), d, fp)
    wd_b = _pad2(wd.astype(BF16), fp, d)
    act, = _mm(h, [wg_b, wu_b], lambda accs, ex: [_silu(accs[0]) * accs[1]], (BF16,), name="ffn_up")
    x_new, = _mm(act, [wd_b], lambda accs, ex: [ex[0] + ex[1] * accs[0]], (F32,),
                 extras=[(x, "tile", 0), (mod_gate.reshape(1, d), "row", 0)], name="ffn_down")
    return x_new


def kernel(x, c, ctx, c_ctx, ada_w, ada_b, norm_mix_g, norm_ffn_g, w_in, hy_conv_w, hy_conv_b, hy_w1, hy_b1, hy_w2, hy_b2, hy_w3, hy_b3, hy_w4, hy_freq, hy_bias, gla_aw_f, gla_ab_f, gla_aw_b, gla_ab_b, gla_norm_g, w_up_hy, w_up_gla, w_out, ffn_w_gate, ffn_w_up, ffn_w_down, router_w, exp_w_gate, exp_w_up, exp_w_down, final_norm_g):
    assert x.shape[0] == 1 and c.shape[0] == 1, "batch size 1 only"
    depth, d = norm_mix_g.shape
    x_lat, x_ctx = x[0], ctx[0]
    mods = _ada_modulation(jnp.concatenate([c, c_ctx.reshape(1, d)], axis=0), ada_w, ada_b)
    dk, dv = d // 2 // GLA_HEADS, d // GLA_HEADS
    per_layer = dict(hy_conv_w=hy_conv_w, hy_conv_b=hy_conv_b, hy_w1=hy_w1, hy_b1=hy_b1, hy_w2=hy_w2, hy_b2=hy_b2,
                     hy_w3=hy_w3, hy_b3=hy_b3, hy_w4=hy_w4, hy_freq=hy_freq, hy_bias=hy_bias,
                     gla_aw_f=gla_aw_f, gla_ab_f=gla_ab_f, gla_aw_b=gla_aw_b, gla_ab_b=gla_ab_b,
                     gla_norm_g=gla_norm_g, w_up_hy=w_up_hy, w_up_gla=w_up_gla, w_out=w_out)
    for l in range(depth):
        last = l == depth - 1
        p = {name: arr[l] for name, arr in per_layer.items()}
        wi = _in_weights(w_in[l], d)
        lat = [mods[l, 0, i * d:(i + 1) * d] for i in range(N_ADA)]
        cxm = [mods[l, 1, i * d:(i + 1) * d] for i in range(N_ADA)]

        h_ctx, = _modnorm(x_ctx, norm_mix_g[l], cxm[0], cxm[1], (BF16,))
        if last:
            s_f, s_b = _gla_states_only(h_ctx, wi, p, d)
        else:
            zero = jnp.zeros((GLA_HEADS, dv, dk), F32)
            x_ctx, s_f, s_b = _token_mixer(x_ctx, h_ctx, cxm[2], zero, zero, wi, p, d)
        h_lat, = _modnorm(x_lat, norm_mix_g[l], lat[0], lat[1], (BF16,))
        x_lat, _, _ = _token_mixer(x_lat, h_lat, lat[2], s_f, s_b, wi, p, d)

        i = l // 2
        if l % 2 == 0:
            h2, = _modnorm(x_lat, norm_ffn_g[l], lat[3], lat[4], (BF16,))
            x_lat = _dense_ffn(x_lat, h2, lat[5], ffn_w_gate[i], ffn_w_up[i], ffn_w_down[i])
            if not last:
                h2c, = _modnorm(x_ctx, norm_ffn_g[l], cxm[3], cxm[4], (BF16,))
                x_ctx = _dense_ffn(x_ctx, h2c, cxm[5], ffn_w_gate[i], ffn_w_up[i], ffn_w_down[i])
        else:
            pm = dict(router_w=router_w[i], exp_w_gate=exp_w_gate[i], exp_w_up=exp_w_up[i], exp_w_down=exp_w_down[i])
            h2, = _modnorm(x_lat, norm_ffn_g[l], lat[3], lat[4], (F32,))
            x_lat = _moe_ffn(h2, x_lat, lat[5], pm, final_norm_g, final_norm=last)
            if not last:
                h2c, = _modnorm(x_ctx, norm_ffn_g[l], cxm[3], cxm[4], (F32,))
                x_ctx = _moe_ffn(h2c, x_ctx, cxm[5], pm, final_norm_g, final_norm=False)
    if depth % 2 == 1:
        ones = jnp.ones((d,), F32)
        x_lat, = _modnorm(x_lat, final_norm_g, jnp.zeros((d,), F32), jnp.zeros((d,), F32), (F32,))
    return x_lat[None]
```

```python
import functools
import math

import numpy as np
import jax
import jax.numpy as jnp
from jax import lax
from jax.experimental import pallas as pl
from jax.experimental.pallas import tpu as pltpu

F32 = jnp.float32
BF16 = jnp.bfloat16
HIGHEST = lax.Precision.HIGHEST

EPS = 1e-6
N_ADA = 6
LANES = 128
SUBLANES = 8
VMEM_LIMIT_CAP = 60 * 1024 * 1024

GLA_HEADS = 4
GLA_GATE_RANK = 16
GLA_GATE_TEMP = 16.0
HY_SHORT = 3
HY_EMB_BANDS = 16
HY_FILTER_HIDDEN = 64
HY_DECAY_TARGET = 1e-2
HY_FAST_DECAY = 0.3
HY_SLOW_DECAY = 1.5
N_EXPERTS = 8
TOP_K = 2


def _cparams(sem, vmem_bytes):
    limit = int(min(max(vmem_bytes * 5 // 4 + (2 << 20), 16 << 20), VMEM_LIMIT_CAP))
    return pltpu.CompilerParams(dimension_semantics=sem, vmem_limit_bytes=limit)


def _nbytes(shape, dtype):
    return int(np.prod(shape)) * jnp.dtype(dtype).itemsize


def _ada_body(c_ref, w_ref, b_ref, o_ref):
    tn = o_ref.shape[-1]
    d = c_ref.shape[1]
    rows = []
    for r in range(2):
        s = c_ref[r]
        s = s * jax.nn.sigmoid(s)
        chunks = []
        for n0 in range(0, tn, LANES):
            p = w_ref[:, n0:n0 + LANES] * s
            acc = p.reshape(d // SUBLANES, SUBLANES, LANES).sum(axis=0)
            chunks.append(acc.sum(axis=0, keepdims=True))
        rows.append(jnp.concatenate(chunks, axis=1))
    o_ref[...] = jnp.concatenate(rows, axis=0) + b_ref[...]


def _ada_modulation(cond2, ada_w, ada_b):
    depth, d, n = ada_w.shape
    tn = 1536 if n % 1536 == 0 else LANES
    cb = jnp.broadcast_to(cond2[:, :, None], (2, d, LANES))
    vm = 2 * _nbytes((d, tn), F32) + 2 * _nbytes((2, d, LANES), F32)
    return pl.pallas_call(
        _ada_body,
        grid=(depth, n // tn),
        in_specs=[
            pl.BlockSpec((2, d, LANES), lambda l, j: (0, 0, 0)),
            pl.BlockSpec((None, d, tn), lambda l, j: (l, 0, j)),
            pl.BlockSpec((None, 1, tn), lambda l, j: (l, 0, j)),
        ],
        out_specs=pl.BlockSpec((None, 2, tn), lambda l, j: (l, 0, j)),
        out_shape=jax.ShapeDtypeStruct((depth, 2, n), F32),
        compiler_params=_cparams(("parallel", "parallel"), vm),
        name="ada_modulation",
    )(cb, ada_w, ada_b.reshape(depth, 1, n))


def _modnorm_body(x_ref, g_ref, sh_ref, sc_ref, *o_refs):
    x = x_ref[...]
    ms = jnp.mean(x * x, axis=-1, keepdims=True)
    y = x * lax.rsqrt(ms + EPS) * g_ref[...]
    y = y * (1.0 + sc_ref[...]) + sh_ref[...]
    for o in o_refs:
        o[...] = y.astype(o.dtype)


def _modnorm(x, g, shift, scale, out_dtypes, tm=256):
    m, d = x.shape
    tm = min(tm, m)
    row = pl.BlockSpec((1, d), lambda i: (0, 0))
    tile = pl.BlockSpec((tm, d), lambda i: (i, 0))
    vm = 2 * _nbytes((tm, d), F32) * (1 + len(out_dtypes))
    outs = pl.pallas_call(
        _modnorm_body,
        grid=(m // tm,),
        in_specs=[tile, row, row, row],
        out_specs=[tile] * len(out_dtypes),
        out_shape=[jax.ShapeDtypeStruct((m, d), dt) for dt in out_dtypes],
        compiler_params=_cparams(("parallel",), vm),
        name="modnorm",
    )(x, g.reshape(1, d), shift.reshape(1, d), scale.reshape(1, d))
    return outs


def _mm_body(*refs, n_w, n_e, epi, precision):
    x_ref = refs[0]
    w_refs = refs[1:1 + n_w]
    e_refs = refs[1 + n_w:1 + n_w + n_e]
    o_refs = refs[1 + n_w + n_e:]
    x = x_ref[...]
    if precision is None:
        x = x.astype(BF16)
    accs = [jnp.dot(x, w[...], preferred_element_type=F32, precision=precision) for w in w_refs]
    outs = epi(accs, [e[...] for e in e_refs])
    for o, v in zip(o_refs, outs):
        o[...] = v.astype(o.dtype)


def _mm(x, ws, epi, out_dtypes, extras=(), tm=1024, tn=1024, precision=None, name="mm"):
    m, k = x.shape
    n = ws[0].shape[1]
    tm = min(tm, m)
    tn = min(tn, n)
    assert m % tm == 0 and n % tn == 0, (m, tm, n, tn)
    in_specs = [pl.BlockSpec((tm, k), lambda j, i: (i, 0))]
    in_specs += [pl.BlockSpec((k, tn), lambda j, i: (0, j))] * len(ws)
    vm = 2 * _nbytes((tm, k), x.dtype) + 2 * len(ws) * _nbytes((k, tn), ws[0].dtype)
    for arr, kind, col in extras:
        assert col % tn == 0
        if kind == "tile":
            in_specs.append(pl.BlockSpec((tm, tn), lambda j, i, c=col // tn: (i, j + c)))
            vm += 2 * _nbytes((tm, tn), arr.dtype)
        else:
            in_specs.append(pl.BlockSpec((1, tn), lambda j, i, c=col // tn: (0, j + c)))
    vm += sum(2 * _nbytes((tm, tn), dt) for dt in out_dtypes) + (1 + len(ws)) * _nbytes((tm, tn), F32)
    return pl.pallas_call(
        functools.partial(_mm_body, n_w=len(ws), n_e=len(extras), epi=epi, precision=precision),
        grid=(n // tn, m // tm),
        in_specs=in_specs,
        out_specs=[pl.BlockSpec((tm, tn), lambda j, i: (i, j))] * len(out_dtypes),
        out_shape=[jax.ShapeDtypeStruct((m, n), dt) for dt in out_dtypes],
        compiler_params=_cparams(("parallel", "parallel"), vm),
        name=name,
    )(x, *ws, *[a for a, _, _ in extras])


def _log_sigmoid(z):
    return jnp.minimum(z, 0.0) - jnp.log1p(jnp.exp(-jnp.abs(z)))


def _split_bf16(x, terms):
    parts = []
    for _ in range(terms):
        p = x.astype(BF16)
        parts.append(p)
        x = x - p.astype(F32)
    return parts


def _gla_decay_body(a_ref, awf_ref, awb_ref, abf_ref, abb_ref, ef_ref, eb_ref):
    c = a_ref.shape[0]
    a = a_ref[...]
    r = lax.broadcasted_iota(jnp.int32, (c, c), 0)
    s = lax.broadcasted_iota(jnp.int32, (c, c), 1)
    lower = (s <= r).astype(BF16)
    upper = (s >= r).astype(BF16)
    a_parts = _split_bf16(a, 2)

    def gate_logits(w_ref, b_ref):
        w_hi, w_lo = _split_bf16(w_ref[...], 2)
        z = jnp.dot(a_parts[0], w_hi, preferred_element_type=F32)
        z = z + jnp.dot(a_parts[1], w_hi, preferred_element_type=F32)
        z = z + jnp.dot(a_parts[0], w_lo, preferred_element_type=F32)
        return z + b_ref[...]

    def chunk_sums(tri, g):
        return sum(jnp.dot(tri, part, preferred_element_type=F32) for part in _split_bf16(g, 3))

    gf = _log_sigmoid(gate_logits(awf_ref, abf_ref)) * (1.0 / GLA_GATE_TEMP)
    gb = _log_sigmoid(gate_logits(awb_ref, abb_ref)) * (1.0 / GLA_GATE_TEMP)
    ef_ref[...] = chunk_sums(lower, gf)
    eb_ref[...] = chunk_sums(upper, gb)


def _gla_decay(a, awf, awb, abf, abb, chunk, tn=512):
    l = a.shape[0]
    n = awf.shape[1]
    tn = min(tn, n)
    col = pl.BlockSpec((a.shape[1], tn), lambda i, j: (0, j))
    row = pl.BlockSpec((1, tn), lambda i, j: (0, j))
    out = pl.BlockSpec((chunk, tn), lambda i, j: (i, j))
    vm = 4 * _nbytes((chunk, tn), F32) * 3 + 4 * _nbytes((a.shape[1], tn), F32)
    return pl.pallas_call(
        _gla_decay_body,
        grid=(l // chunk, n // tn),
        in_specs=[pl.BlockSpec((chunk, a.shape[1]), lambda i, j: (i, 0)), col, col, row, row],
        out_specs=[out, out],
        out_shape=[jax.ShapeDtypeStruct((l, n), F32)] * 2,
        compiler_params=_cparams(("parallel", "parallel"), vm),
        name="gla_decay",
    )(a, awf, awb, abf, abb)


def _dot_nt(a, b):
    return lax.dot_general(a, b, (((1,), (1,)), ((), ())), preferred_element_type=F32)


def _dot_tn(a, b):
    return lax.dot_general(a, b, (((0,), (0,)), ((), ())), preferred_element_type=F32)


def _bcast_rows(e, group, row):
    c, w = e.shape
    e3 = e.reshape(c // group, group, w)
    return jnp.broadcast_to(e3[:, row:row + 1, :], e3.shape).reshape(c, w)


def _gla_masks(c, base, fwd):
    i = lax.broadcasted_iota(jnp.int32, (c, c), 0)
    j = lax.broadcasted_iota(jnp.int32, (c, c), 1)
    sh = int(math.log2(base))
    order = (j <= i) if fwd else (j >= i)
    masks = [((i >> sh) == (j >> sh)) & order]
    s = base
    while 2 * s <= c:
        sh += 1
        masks.append((i >> sh) == (j >> sh))
        s *= 2
    return masks


def _gla_chunk_head(q, k, v, e, st, masks, *, fwd, base):
    c, dk = q.shape
    row = lax.broadcasted_iota(jnp.int32, (c, 1), 0)
    d0 = e - _bcast_rows(e, base, base // 2 - 1 if fwd else base // 2)
    q0 = (q * jnp.exp(d0)).astype(BF16)
    k0 = (k * jnp.exp(-d0)).astype(BF16)
    att = jnp.where(masks[0], _dot_nt(q0, k0), 0.0)
    s, lvl = base, 1
    while 2 * s <= c:
        d = e - _bcast_rows(e, 2 * s, s - 1 if fwd else s)
        later = ((row >> int(math.log2(s))) & 1) == (1 if fwd else 0)
        x = jnp.exp(jnp.where(later, d, -d))
        ql = jnp.where(later, q * x, 0.0).astype(BF16)
        kl = jnp.where(later, 0.0, k * x).astype(BF16)
        att = att + jnp.where(masks[lvl], _dot_nt(ql, kl), 0.0)
        s *= 2
        lvl += 1
    e_edge = e[c - 1:c] if fwd else e[0:1]
    qs = (q * jnp.exp(e)).astype(BF16)
    ks = (k * jnp.exp(e_edge - e)).astype(BF16)
    o = jnp.dot(att.astype(BF16), v, preferred_element_type=F32) + _dot_nt(qs, st.astype(BF16))
    st_new = st * jnp.exp(e_edge) + _dot_tn(v, ks)
    return o, st_new


def _gla_scan_body(*refs, fwd, base, combine):
    if combine:
        q_ref, k_ref, v_ref, e_ref, s0_ref, of_ref, sg_ref, gn_ref, o_ref, sfin_ref, st_ref = refs
    else:
        q_ref, k_ref, v_ref, e_ref, s0_ref, o_ref, sfin_ref, st_ref = refs
    step = pl.program_id(0)
    c = q_ref.shape[0]
    dk = q_ref.shape[1] // GLA_HEADS
    dv = v_ref.shape[1] // GLA_HEADS

    @pl.when(step == 0)
    def _():
        st_ref[...] = s0_ref[...]

    masks = _gla_masks(c, base, fwd)
    for h in range(GLA_HEADS):
        ks = slice(h * dk, (h + 1) * dk)
        vs = slice(h * dv, (h + 1) * dv)
        o, st_new = _gla_chunk_head(
            q_ref[:, ks].astype(F32), k_ref[:, ks].astype(F32), v_ref[:, vs], e_ref[:, ks], st_ref[h],
            masks, fwd=fwd, base=base)
        st_ref[h] = st_new
        if combine:
            t = o + of_ref[:, vs]
            t = t * lax.rsqrt(jnp.mean(t * t, axis=-1, keepdims=True) + EPS) * gn_ref[...]
            o_ref[:, vs] = (t * sg_ref[:, vs].astype(F32)).astype(o_ref.dtype)
        else:
            o_ref[:, vs] = o.astype(o_ref.dtype)

    @pl.when(step == pl.num_programs(0) - 1)
    def _():
        sfin_ref[...] = st_ref[...]


def _gla_scan(q, k, v, e, s0, *, fwd, chunk, base=16, combine=None):
    l, hdk = q.shape
    hdv = v.shape[1]
    dk, dv = hdk // GLA_HEADS, hdv // GLA_HEADS
    chunk = min(chunk, l)
    n = l // chunk
    idx = (lambda i: (i, 0)) if fwd else (lambda i: (n - 1 - i, 0))
    st_spec = pl.BlockSpec((GLA_HEADS, dv, dk), lambda i: (0, 0, 0))
    in_specs = [pl.BlockSpec((chunk, hdk), idx), pl.BlockSpec((chunk, hdk), idx),
                pl.BlockSpec((chunk, hdv), idx), pl.BlockSpec((chunk, hdk), idx), st_spec]
    args = [q, k, v, e, s0]
    vm = 2 * (2 * _nbytes((chunk, hdk), BF16) + _nbytes((chunk, hdv), BF16) + _nbytes((chunk, hdk), F32))
    vm += 3 * _nbytes((GLA_HEADS, dv, dk), F32) * 2 + 2 * _nbytes((chunk, hdv), F32)
    if combine is not None:
        o_other, gate, norm_g = combine
        in_specs += [pl.BlockSpec((chunk, hdv), idx), pl.BlockSpec((chunk, hdv), idx),
                     pl.BlockSpec((1, dv), lambda i: (0, 0))]
        args += [o_other, gate, norm_g.reshape(1, dv)]
        vm += 2 * (_nbytes((chunk, hdv), F32) + _nbytes((chunk, hdv), BF16))
    vm += 24 * _nbytes((chunk, max(dk, chunk)), F32)
    return pl.pallas_call(
        functools.partial(_gla_scan_body, fwd=fwd, base=base, combine=combine is not None),
        grid=(n,),
        in_specs=in_specs,
        out_specs=[pl.BlockSpec((chunk, hdv), idx), st_spec],
        out_shape=[jax.ShapeDtypeStruct((l, hdv), BF16 if combine is not None else F32),
                   jax.ShapeDtypeStruct((GLA_HEADS, dv, dk), F32)],
        scratch_shapes=[pltpu.VMEM((GLA_HEADS, dv, dk), F32)],
        compiler_params=_cparams(("arbitrary",), vm),
        name="gla_scan_fwd" if fwd else "gla_scan_bwd",
    )(*args)


def _shortconv_body(u0, u1, u2, p0, p1, p2, n0, n1, n2, w0, w1, w2, b0, b1, b2, x0_ref, z_ref):
    i = pl.program_id(0)
    last = pl.num_programs(0) - 1
    tm = u0.shape[0]
    row = lax.broadcasted_iota(jnp.int32, (tm, 1), 0)

    def conv(u_ref, p_ref, n_ref, w_ref, b_ref):
        u = u_ref[...]
        prev_row = jnp.where(i == 0, 0.0, p_ref[SUBLANES - 1:SUBLANES, :])
        next_row = jnp.where(i == last, 0.0, n_ref[0:1, :])
        before = jnp.where(row == 0, prev_row, pltpu.roll(u, 1, axis=0))
        after = jnp.where(row == tm - 1, next_row, pltpu.roll(u, tm - 1, axis=0))
        return b_ref[...] + before * w_ref[0:1, :] + u * w_ref[1:2, :] + after * w_ref[2:3, :]

    x0_ref[...] = conv(u0, p0, n0, w0, b0)
    z_ref[...] = conv(u1, p1, n1, w1, b1) * conv(u2, p2, n2, w2, b2)


def _hyena_shortconv(hy, conv_w, conv_b, tm=512, cb=512):
    l, w3 = hy.shape
    w = w3 // 3
    tm = min(tm, l)
    cb = min(cb, w)
    nb = w // cb
    hb = tm // SUBLANES
    nrow8 = l // SUBLANES
    cur = [pl.BlockSpec((tm, cb), lambda i, j, g=g: (i, g * nb + j)) for g in range(3)]
    prv = [pl.BlockSpec((SUBLANES, cb), lambda i, j, g=g: (jnp.maximum(i * hb - 1, 0), g * nb + j)) for g in range(3)]
    nxt = [pl.BlockSpec((SUBLANES, cb), lambda i, j, g=g: (jnp.minimum((i + 1) * hb, nrow8 - 1), g * nb + j))
           for g in range(3)]
    wsp = [pl.BlockSpec((HY_SHORT, cb), lambda i, j, g=g: (0, g * nb + j)) for g in range(3)]
    bsp = [pl.BlockSpec((1, cb), lambda i, j, g=g: (0, g * nb + j)) for g in range(3)]
    out = pl.BlockSpec((tm, cb), lambda i, j: (i, j))
    vm = 2 * 5 * _nbytes((tm, cb), F32) + 8 * _nbytes((tm, cb), F32)
    return pl.pallas_call(
        _shortconv_body,
        grid=(l // tm, nb),
        in_specs=cur + prv + nxt + wsp + bsp,
        out_specs=[out, out],
        out_shape=[jax.ShapeDtypeStruct((l, w), F32)] * 2,
        compiler_params=_cparams(("parallel", "parallel"), vm),
        name="hyena_shortconv",
    )(hy, hy, hy, hy, hy, hy, hy, hy, hy, conv_w, conv_w, conv_w,
      conv_b.reshape(1, w3), conv_b.reshape(1, w3), conv_b.reshape(1, w3))


def _filter_body(w1_ref, b1_ref, w2_ref, b2_ref, w3_ref, b3_ref, fr_ref, w4f_ref, w4b_ref,
                 hf_ref, hg_ref, ss_ref, *, seq_len):
    i = pl.program_id(0)
    tr = hf_ref.shape[0]
    wdt = hf_ref.shape[1]
    pos = (lax.broadcasted_iota(jnp.int32, (tr, 1), 0) + i * tr).astype(F32)
    t = pos / float(max(seq_len - 1, 1))
    lane = lax.broadcasted_iota(jnp.int32, (1, LANES), 1)
    band = ((lane - 1) & (HY_EMB_BANDS - 1)).astype(F32)
    bands = 1e-4 + band * ((HY_EMB_BANDS - 1 - 1e-4) / (HY_EMB_BANDS - 1))
    ang = ((2.0 * math.pi / seq_len) * pos) * bands
    trig = jnp.cos(ang + jnp.where(lane > HY_EMB_BANDS, 0.5 * math.pi, 0.0))
    emb = jnp.where(lane == 0, t, jnp.where(lane <= 2 * HY_EMB_BANDS, trig, 0.0))
    fr = fr_ref[...]
    h = jnp.sin(fr * (jnp.dot(emb, w1_ref[...], preferred_element_type=F32, precision=HIGHEST) + b1_ref[...]))
    h = jnp.sin(fr * (jnp.dot(h, w2_ref[...], preferred_element_type=F32, precision=HIGHEST) + b2_ref[...]))
    h = jnp.sin(fr * (jnp.dot(h, w3_ref[...], preferred_element_type=F32, precision=HIGHEST) + b3_ref[...]))
    ch = lax.broadcasted_iota(jnp.int32, (1, wdt), 1).astype(F32)
    lo = math.log(HY_DECAY_TARGET) / HY_SLOW_DECAY
    hi = math.log(HY_DECAY_TARGET) / HY_FAST_DECAY
    deltas = jnp.abs(lo + ch * ((hi - lo) / (wdt - 1)))
    window = jnp.exp(-t * deltas)
    hb = h.astype(BF16)
    hf = jnp.dot(hb, w4f_ref[...].astype(BF16), preferred_element_type=F32) * window
    hg = jnp.dot(hb, w4b_ref[...].astype(BF16), preferred_element_type=F32) * window
    hg = jnp.where(pos == 0.0, 0.0, hg)
    hf_ref[...] = hf
    hg_ref[...] = hg

    @pl.when(i == 0)
    def _():
        ss_ref[...] = jnp.zeros_like(ss_ref)

    ss_ref[...] += jnp.sum(hf * hf + hg * hg, axis=0, keepdims=True)


def _pad2(a, rows, cols):
    return jnp.zeros((rows, cols), a.dtype).at[:a.shape[0], :a.shape[1]].set(a)


def _hyena_filter(seq_len, p, tr=256):
    wdt = p["hy_w4"].shape[1] // 2
    tr = min(tr, seq_len)
    hid = LANES
    w1 = _pad2(p["hy_w1"], LANES, hid)
    w2 = _pad2(p["hy_w2"], hid, hid)
    w3 = _pad2(p["hy_w3"], hid, hid)
    b1, b2, b3, fr = (_pad2(p[k].reshape(1, -1), 1, hid) for k in ("hy_b1", "hy_b2", "hy_b3", "hy_freq"))
    w4f = _pad2(p["hy_w4"][:, :wdt], hid, wdt)
    w4b = _pad2(p["hy_w4"][:, wdt:], hid, wdt)
    full = lambda a: pl.BlockSpec(a.shape, lambda i: (0, 0))
    out = pl.BlockSpec((tr, wdt), lambda i: (i, 0))
    args = (w1, b1, w2, b2, w3, b3, fr, w4f, w4b)
    vm = 4 * _nbytes((hid, wdt), F32) + 8 * _nbytes((tr, wdt), F32)
    return pl.pallas_call(
        functools.partial(_filter_body, seq_len=seq_len),
        grid=(seq_len // tr,),
        in_specs=[full(a) for a in args],
        out_specs=[out, out, pl.BlockSpec((1, wdt), lambda i: (0, 0))],
        out_shape=[jax.ShapeDtypeStruct((seq_len, wdt), F32)] * 2 + [jax.ShapeDtypeStruct((1, wdt), F32)],
        compiler_params=_cparams(("arbitrary",), vm),
        name="hyena_filter",
    )(*args)


def _fft_dims(seq_len):
    n = 2 * seq_len
    p = 1 << ((n.bit_length() - 1) // 2)
    return p, n // p


@functools.lru_cache(maxsize=None)
def _fft_consts(seq_len):
    pp, mm = _fft_dims(seq_len)
    n = pp * mm
    a = np.arange(pp // 2)
    b = np.arange(mm)
    d = np.arange(pp)
    ang = -2 * np.pi * (d[None, :, None] * a[None, None, :] / pp + b[:, None, None] * d[None, :, None] / n)
    f1 = np.concatenate([np.cos(ang), np.sin(ang)], axis=1)
    angc = -2 * np.pi * np.outer(b, b) / mm
    cr, ci = np.cos(angc), np.sin(angc)
    w2 = np.block([[cr, -ci], [ci, cr]])
    v2 = np.block([[cr, ci], [-ci, cr]])
    angl = 2 * np.pi * (a[None, :, None] * d[None, None, :] / pp + b[:, None, None] * d[None, None, :] / n)
    g1 = np.concatenate([np.cos(angl), -np.sin(angl)], axis=2) / n
    as_bf16 = lambda x: jnp.asarray(x, dtype=F32).astype(BF16)
    return as_bf16(f1), as_bf16(w2), as_bf16(v2), as_bf16(g1)


def _stack_ri(re_ref, im_ref, dl):
    return jnp.concatenate([re_ref[:, dl, :], im_ref[:, dl, :]], axis=0).astype(BF16)


def _stage1_to_scratch(a_ref, f_ref, rhs_of, t):
    for bl in range(SUBLANES):
        r = jnp.dot(f_ref[bl], rhs_of(bl), preferred_element_type=F32)
        a_ref[:, t * SUBLANES + bl] = r.reshape(a_ref.shape[0], SUBLANES, r.shape[-1])


def _scratch_rows(a_ref, dt, dl):
    nd = a_ref.shape[0] // 2
    return jnp.concatenate([a_ref[dt, :, dl, :], a_ref[nd + dt, :, dl, :]], axis=0).astype(BF16)


def _fft_spec_body(hf_ref, hg_ref, f_ref, w2_ref, o_ref, a_ref, *, nb):
    t = pl.program_id(1)
    cb = hf_ref.shape[-1]
    mm = a_ref.shape[1]

    @pl.when(t < nb)
    def _():
        rhs = lambda bl: jnp.concatenate([hf_ref[:, bl, :], hg_ref[:, bl, :]], axis=-1).astype(BF16)
        _stage1_to_scratch(a_ref, f_ref, rhs, t)

    @pl.when(t >= nb)
    def _():
        for dl in range(SUBLANES):
            h = jnp.dot(w2_ref[...], _scratch_rows(a_ref, t - nb, dl), preferred_element_type=F32)
            o_ref[dl, :mm, :] = (h[:mm, :cb] + h[:mm, cb:]).astype(o_ref.dtype)
            o_ref[dl, mm:, :] = (h[mm:, :cb] - h[mm:, cb:]).astype(o_ref.dtype)


def _fft_filter_spectrum(hf, hg, f1, w2, cb=LANES):
    l, c = hf.shape
    mm, p2, ph = f1.shape
    pp = p2 // 2
    cb = min(cb, c)
    nb, nd = mm // SUBLANES, pp // SUBLANES
    taps = pl.BlockSpec((ph, SUBLANES, cb), lambda j, t: (0, jnp.minimum(t, nb - 1), j))
    vm = (_nbytes((p2 // SUBLANES, mm, SUBLANES, 2 * cb), F32) + 4 * _nbytes((ph, SUBLANES, cb), F32)
          + 2 * _nbytes((SUBLANES, p2, ph), BF16) + 2 * _nbytes((SUBLANES, 2 * mm, cb), BF16)
          + 6 * _nbytes((2 * mm, 2 * cb), F32))
    return pl.pallas_call(
        functools.partial(_fft_spec_body, nb=nb),
        grid=(c // cb, nb + nd),
        in_specs=[taps, taps,
                  pl.BlockSpec((SUBLANES, p2, ph), lambda j, t: (jnp.minimum(t, nb - 1), 0, 0)),
                  pl.BlockSpec(w2.shape, lambda j, t: (0, 0))],
        out_specs=pl.BlockSpec((SUBLANES, 2 * mm, cb), lambda j, t: (jnp.maximum(t - nb, 0), 0, j)),
        out_shape=jax.ShapeDtypeStruct((pp, 2 * mm, c), BF16),
        scratch_shapes=[pltpu.VMEM((p2 // SUBLANES, mm, SUBLANES, 2 * cb), F32)],
        compiler_params=_cparams(("parallel", "arbitrary"), vm),
        name="fft_filter_spectrum",
    )(hf.reshape(ph, mm, c), hg.reshape(ph, mm, c), f1, w2)


def _fft_mid_body(z_ref, f_ref, h_ref, w2_ref, v2_ref, o_ref, a_ref, *, nb):
    t = pl.program_id(1)
    mm = a_ref.shape[1]

    @pl.when(t < nb)
    def _():
        _stage1_to_scratch(a_ref, f_ref, lambda bl: z_ref[:, bl, :].astype(BF16), t)

    @pl.when(t >= nb)
    def _():
        for dl in range(SUBLANES):
            x = jnp.dot(w2_ref[...], _scratch_rows(a_ref, t - nb, dl), preferred_element_type=F32)
            xr, xi = x[:mm], x[mm:]
            hr, hi = h_ref[dl, :mm, :].astype(F32), h_ref[dl, mm:, :].astype(F32)
            y = jnp.concatenate([xr * hr - xi * hi, xr * hi + xi * hr], axis=0).astype(BF16)
            o_ref[dl] = jnp.dot(v2_ref[...], y, preferred_element_type=F32)


def _fft_mid(z, h, f1, w2, v2, cb=256):
    l, c = z.shape
    mm, p2, ph = f1.shape
    pp = p2 // 2
    cb = min(cb, c)
    nb, nd = mm // SUBLANES, pp // SUBLANES
    const = pl.BlockSpec(w2.shape, lambda j, t: (0, 0))
    slab = lambda dt: pl.BlockSpec((SUBLANES, 2 * mm, cb), lambda j, t: (jnp.maximum(t - nb, 0), 0, j))
    vm = (_nbytes((p2 // SUBLANES, mm, SUBLANES, cb), F32) + 2 * _nbytes((ph, SUBLANES, cb), F32)
          + 2 * _nbytes((SUBLANES, p2, ph), BF16) + 2 * _nbytes((SUBLANES, 2 * mm, cb), BF16)
          + 2 * _nbytes((SUBLANES, 2 * mm, cb), F32) + 8 * _nbytes((2 * mm, cb), F32))
    return pl.pallas_call(
        functools.partial(_fft_mid_body, nb=nb),
        grid=(c // cb, nb + nd),
        in_specs=[pl.BlockSpec((ph, SUBLANES, cb), lambda j, t: (0, jnp.minimum(t, nb - 1), j)),
                  pl.BlockSpec((SUBLANES, p2, ph), lambda j, t: (jnp.minimum(t, nb - 1), 0, 0)),
                  slab(0), const, const],
        out_specs=slab(0),
        out_shape=jax.ShapeDtypeStruct((pp, 2 * mm, c), F32),
        scratch_shapes=[pltpu.VMEM((p2 // SUBLANES, mm, SUBLANES, cb), F32)],
        compiler_params=_cparams(("parallel", "arbitrary"), vm),
        name="fft_mid",
    )(z.reshape(ph, mm, c), f1, h, w2, v2)


def _fft_last_body(bre, bim, g_ref, x0_ref, z_ref, ss_ref, bias_ref, o_ref):
    scale = lax.rsqrt(ss_ref[...] + EPS)
    for bl in range(SUBLANES):
        y = jnp.dot(g_ref[bl], _stack_ri(bre, bim, bl), preferred_element_type=F32)
        z = z_ref[:, bl, :]
        o_ref[:, bl, :] = x0_ref[:, bl, :] * (y * scale + z * bias_ref[...])


def _fft_last(bmat, g1, x0, z, ss, bias, cb=256):
    pp, m2, c = bmat.shape
    mm = m2 // 2
    ph = g1.shape[1]
    cb = min(cb, c)
    nb = mm // SUBLANES
    view = pl.BlockSpec((ph, SUBLANES, cb), lambda j, b: (0, b, j))
    row = pl.BlockSpec((1, cb), lambda j, b: (0, j))
    vm = 2 * (2 * _nbytes((pp, SUBLANES, cb), F32) + 3 * _nbytes((ph, SUBLANES, cb), F32)
              + _nbytes((SUBLANES, ph, 2 * pp), BF16)) + 4 * _nbytes((2 * pp, cb), F32)
    out = pl.pallas_call(
        _fft_last_body,
        grid=(c // cb, nb),
        in_specs=[pl.BlockSpec((pp, SUBLANES, cb), lambda j, b: (0, b, j)),
                  pl.BlockSpec((pp, SUBLANES, cb), lambda j, b: (0, nb + b, j)),
                  pl.BlockSpec((SUBLANES, ph, 2 * pp), lambda j, b: (b, 0, 0)),
                  view, view, row, row],
        out_specs=view,
        out_shape=jax.ShapeDtypeStruct((ph, mm, c), F32),
        compiler_params=_cparams(("parallel", "parallel"), vm),
        name="fft_last",
    )(bmat, bmat, g1, x0.reshape(ph, mm, c), z.reshape(ph, mm, c), ss, bias.reshape(1, c))
    return out.reshape(ph * mm, c)


def _hyena_longconv(x0, z, hf, hg, ss, bias):
    f1, w2, v2, g1 = _fft_consts(z.shape[0])
    h = _fft_filter_spectrum(hf, hg, f1, w2)
    return _fft_last(_fft_mid(z, h, f1, w2, v2), g1, x0, z, ss, bias)


def _router_body(h_ref, w_ref, o_ref):
    logits = jnp.dot(h_ref[...], w_ref[...], preferred_element_type=F32, precision=HIGHEST)
    lane = lax.broadcasted_iota(jnp.int32, logits.shape, 1)
    lg = jnp.where(lane < N_EXPERTS, logits, -jnp.inf)
    m1 = jnp.max(lg, axis=-1, keepdims=True)
    i1 = jnp.min(jnp.where(lg == m1, lane, LANES), axis=-1, keepdims=True)
    l2 = jnp.where(lane == i1, -jnp.inf, lg)
    m2 = jnp.max(l2, axis=-1, keepdims=True)
    i2 = jnp.min(jnp.where(l2 == m2, lane, LANES), axis=-1, keepdims=True)
    e = jnp.exp(m2 - m1)
    w1 = 1.0 / (1.0 + e)
    w2 = e * w1
    o_ref[...] = jnp.where(lane == 0, i1.astype(F32), jnp.where(lane == 1, i2.astype(F32),
                           jnp.where(lane == 2, w1, jnp.where(lane == 3, w2, 0.0))))


def _router(h, router_w, tm=256):
    m, d = h.shape
    tm = min(tm, m)
    w = _pad2(router_w, d, LANES)
    vm = 2 * (_nbytes((tm, d), F32) + _nbytes((d, LANES), F32)) + 8 * _nbytes((tm, LANES), F32)
    return pl.pallas_call(
        _router_body,
        grid=(m // tm,),
        in_specs=[pl.BlockSpec((tm, d), lambda i: (i, 0)), pl.BlockSpec((d, LANES), lambda i: (0, 0))],
        out_specs=pl.BlockSpec((tm, LANES), lambda i: (i, 0)),
        out_shape=jax.ShapeDtypeStruct((m, LANES), F32),
        compiler_params=_cparams(("parallel",), vm),
        name="moe_router",
    )(h, w)


def _routing_tables(route, tile):
    t = route.shape[0]
    e_flat = route[:, :TOP_K].astype(jnp.int32).reshape(-1)
    w_flat = route[:, TOP_K:2 * TOP_K].reshape(-1)
    onehot = (e_flat[:, None] == jnp.arange(N_EXPERTS, dtype=jnp.int32)[None, :]).astype(jnp.int32)
    csum = jnp.cumsum(onehot, axis=0)
    rank = jnp.take_along_axis(csum, e_flat[:, None], axis=1)[:, 0] - 1
    counts = csum[-1]
    padded = ((counts + tile - 1) // tile) * tile
    ends = jnp.cumsum(padded)
    pos = (ends - padded)[e_flat] + rank
    n_tiles = (t * TOP_K) // tile + N_EXPERTS
    rows = n_tiles * tile
    row_token = jnp.zeros((rows,), jnp.int32).at[pos].set(jnp.arange(t * TOP_K, dtype=jnp.int32) // TOP_K)
    row_w = jnp.zeros((rows,), F32).at[pos].set(w_flat)
    start = jnp.arange(n_tiles, dtype=jnp.int32) * tile
    valid = start < ends[-1]
    expert = jnp.minimum(jnp.sum((start[:, None] >= ends[None, :]).astype(jnp.int32), axis=1), N_EXPERTS - 1)
    last_valid = jnp.max(jnp.where(valid, expert, 0))
    expert = jnp.where(valid, expert, last_valid)
    first = jnp.concatenate([jnp.ones((1,), jnp.int32), (expert[1:] != expert[:-1]).astype(jnp.int32)])
    return row_token, row_w, pos.astype(jnp.int32), expert, first, valid.astype(jnp.int32)


def _row_copy(src_hbm, dst, sem, src_row, dst_row):
    return pltpu.make_async_copy(src_hbm.at[pl.ds(src_row, 1)], dst.at[pl.ds(dst_row, 1)], sem)


DMA_LOOP_UNROLL = 8


def _gather_rows_body(tok_ref, h_hbm, o_ref, buf, sems):
    i = pl.program_id(0)
    gt = buf.shape[1]
    slot = i % 2

    def issue(step, s):
        def body(r, carry):
            _row_copy(h_hbm, buf.at[s], sems.at[s], tok_ref[step * gt + r], r).start()
            return carry
        lax.fori_loop(0, gt, body, 0, unroll=DMA_LOOP_UNROLL)

    def drain(s):
        def body(r, carry):
            _row_copy(h_hbm, buf.at[s], sems.at[s], 0, r).wait()
            return carry
        lax.fori_loop(0, gt, body, 0, unroll=DMA_LOOP_UNROLL)

    @pl.when(i == 0)
    def _():
        issue(0, 0)

    @pl.when(i + 1 < pl.num_programs(0))
    def _():
        issue(i + 1, 1 - slot)

    drain(slot)
    o_ref[...] = buf[slot].astype(o_ref.dtype)


def _gather_rows(h, row_token, gt=256):
    rows = row_token.shape[0]
    d = h.shape[1]
    vm = 4 * _nbytes((gt, d), F32)
    return pl.pallas_call(
        _gather_rows_body,
        grid_spec=pltpu.PrefetchScalarGridSpec(
            num_scalar_prefetch=1,
            grid=(rows // gt,),
            in_specs=[pl.BlockSpec(memory_space=pl.ANY)],
            out_specs=pl.BlockSpec((gt, d), lambda i, tok: (i, 0)),
            scratch_shapes=[pltpu.VMEM((2, gt, d), F32), pltpu.SemaphoreType.DMA((2,))],
        ),
        out_shape=jax.ShapeDtypeStruct((rows, d), BF16),
        compiler_params=_cparams(("arbitrary",), vm),
        name="moe_gather_rows",
    )(row_token, h)


def _moe_up_body(te_ref, tf_ref, tv_ref, x_ref, wg_ref, wu_ref, o_ref, wg_bf, wu_bf):
    i = pl.program_id(1)

    @pl.when(tf_ref[i] == 1)
    def _():
        wg_bf[...] = wg_ref[...].astype(BF16)
        wu_bf[...] = wu_ref[...].astype(BF16)

    @pl.when(tv_ref[i] == 1)
    def _():
        x = x_ref[...]
        g = jnp.dot(x, wg_bf[...], preferred_element_type=F32)
        u = jnp.dot(x, wu_bf[...], preferred_element_type=F32)
        o_ref[...] = (g * jax.nn.sigmoid(g) * u).astype(o_ref.dtype)

    @pl.when(tv_ref[i] == 0)
    def _():
        o_ref[...] = jnp.zeros_like(o_ref)


def _moe_up(xs, wg, wu, expert, first, valid, tile, tn=1024):
    rows, d = xs.shape
    f = wg.shape[2]
    wspec = pl.BlockSpec((None, d, tn), lambda j, i, te, tf, tv: (te[i], 0, j))
    vm = 4 * _nbytes((d, tn), F32) + 2 * _nbytes((d, tn), BF16) + 2 * _nbytes((tile, d), BF16) + 6 * _nbytes((tile, tn), F32)
    return pl.pallas_call(
        _moe_up_body,
        grid_spec=pltpu.PrefetchScalarGridSpec(
            num_scalar_prefetch=3,
            grid=(f // tn, rows // tile),
            in_specs=[pl.BlockSpec((tile, d), lambda j, i, te, tf, tv: (i, 0)), wspec, wspec],
            out_specs=pl.BlockSpec((tile, tn), lambda j, i, te, tf, tv: (i, j)),
            scratch_shapes=[pltpu.VMEM((d, tn), BF16), pltpu.VMEM((d, tn), BF16)],
        ),
        out_shape=jax.ShapeDtypeStruct((rows, f), BF16),
        compiler_params=_cparams(("arbitrary", "arbitrary"), vm),
        name="moe_up",
    )(expert, first, valid, xs, wg, wu)


def _moe_down_body(te_ref, tf_ref, tv_ref, a_ref, wd_ref, rw_ref, o_ref, wd_bf):
    i = pl.program_id(1)

    @pl.when(tf_ref[i] == 1)
    def _():
        wd_bf[...] = wd_ref[...].astype(BF16)

    @pl.when(tv_ref[i] == 1)
    def _():
        o_ref[...] = jnp.dot(a_ref[...], wd_bf[...], preferred_element_type=F32) * rw_ref[...]

    @pl.when(tv_ref[i] == 0)
    def _():
        o_ref[...] = jnp.zeros_like(o_ref)


def _moe_down(act, wd, row_w, expert, first, valid, tile, tn=512):
    rows, f = act.shape
    d = wd.shape[2]
    vm = 2 * _nbytes((f, tn), F32) + _nbytes((f, tn), BF16) + 2 * _nbytes((tile, f), BF16) + 4 * _nbytes((tile, tn), F32)
    return pl.pallas_call(
        _moe_down_body,
        grid_spec=pltpu.PrefetchScalarGridSpec(
            num_scalar_prefetch=3,
            grid=(d // tn, rows // tile),
            in_specs=[pl.BlockSpec((tile, f), lambda j, i, te, tf, tv: (i, 0)),
                      pl.BlockSpec((None, f, tn), lambda j, i, te, tf, tv: (te[i], 0, j)),
                      pl.BlockSpec((tile, 1), lambda j, i, te, tf, tv: (i, 0))],
            out_specs=pl.BlockSpec((tile, tn), lambda j, i, te, tf, tv: (i, j)),
            scratch_shapes=[pltpu.VMEM((f, tn), BF16)],
        ),
        out_shape=jax.ShapeDtypeStruct((rows, d), F32),
        compiler_params=_cparams(("arbitrary", "arbitrary"), vm),
        name="moe_down",
    )(expert, first, valid, act, wd, row_w.reshape(rows, 1))


def _moe_combine_body(pos_ref, ys_hbm, x_ref, gate_ref, g_ref, o_ref, buf, sems, *, final_norm):
    i = pl.program_id(0)
    gt = x_ref.shape[0]
    slot = i % 2

    def issue(step, s):
        def body(r, carry):
            for k in range(TOP_K):
                _row_copy(ys_hbm, buf.at[s, k], sems.at[s], pos_ref[TOP_K * (step * gt + r) + k], r).start()
            return carry
        lax.fori_loop(0, gt, body, 0, unroll=DMA_LOOP_UNROLL)

    def drain(s):
        def body(r, carry):
            for k in range(TOP_K):
                _row_copy(ys_hbm, buf.at[s, k], sems.at[s], 0, r).wait()
            return carry
        lax.fori_loop(0, gt, body, 0, unroll=DMA_LOOP_UNROLL)

    @pl.when(i == 0)
    def _():
        issue(0, 0)

    @pl.when(i + 1 < pl.num_programs(0))
    def _():
        issue(i + 1, 1 - slot)

    drain(slot)
    y = buf[slot, 0]
    for k in range(1, TOP_K):
        y = y + buf[slot, k]
    x = x_ref[...] + gate_ref[...] * y
    if final_norm:
        x = x * lax.rsqrt(jnp.mean(x * x, axis=-1, keepdims=True) + EPS) * g_ref[...]
    o_ref[...] = x


def _moe_combine(ys, pos, x, gate, norm_g, final_norm, gt=128):
    t, d = x.shape
    gt = min(gt, t)
    row = pl.BlockSpec((1, d), lambda i, p: (0, 0))
    tilespec = pl.BlockSpec((gt, d), lambda i, p: (i, 0))
    vm = (2 * TOP_K + 6) * _nbytes((gt, d), F32)
    return pl.pallas_call(
        functools.partial(_moe_combine_body, final_norm=final_norm),
        grid_spec=pltpu.PrefetchScalarGridSpec(
            num_scalar_prefetch=1,
            grid=(t // gt,),
            in_specs=[pl.BlockSpec(memory_space=pl.ANY), tilespec, row, row],
            out_specs=tilespec,
            scratch_shapes=[pltpu.VMEM((2, TOP_K, gt, d), F32), pltpu.SemaphoreType.DMA((2,))],
        ),
        out_shape=jax.ShapeDtypeStruct((t, d), F32),
        compiler_params=_cparams(("arbitrary",), vm),
        name="moe_combine",
    )(pos, ys, x, gate.reshape(1, d), norm_g.reshape(1, d))


MOE_ROW_TILE = 256


def _moe_ffn(h32, x, gate, p, norm_g, final_norm):
    route = _router(h32, p["router_w"])
    row_token, row_w, pos, expert, first, valid = _routing_tables(route, MOE_ROW_TILE)
    xs = _gather_rows(h32, row_token)
    act = _moe_up(xs, p["exp_w_gate"], p["exp_w_up"], expert, first, valid, MOE_ROW_TILE)
    ys = _moe_down(act, p["exp_w_down"], row_w, expert, first, valid, MOE_ROW_TILE)
    return _moe_combine(ys, pos, x, gate, norm_g, final_norm)


def _silu(v):
    return v * jax.nn.sigmoid(v)


def _epi_plain(accs, ex):
    return [accs[0]]


def _in_weights(w_in, d):
    qk, vw, r = d // 2, d, GLA_GATE_RANK
    col_v = qk
    col_a = col_v + vw
    col_q = col_a + 2 * r
    col_g = col_q + qk
    col_hy = col_g + vw
    col_gate = col_hy + 3 * d
    cut = lambda a, b: w_in[:, a:b].astype(BF16)
    return dict(k=cut(0, col_v), v=cut(col_v, col_a), a=_pad2(cut(col_a, col_q), d, LANES),
                q=cut(col_q, col_g), g=cut(col_g, col_hy), hy=cut(col_hy, col_gate),
                gate=cut(col_gate, col_gate + 2 * d))


def _gate_matrices(p, d):
    r = GLA_GATE_RANK
    qk = d // 2
    awf = jnp.zeros((LANES, qk), F32).at[:r].set(p["gla_aw_f"])
    awb = jnp.zeros((LANES, qk), F32).at[r:2 * r].set(p["gla_aw_b"])
    return awf, awb, p["gla_ab_f"].reshape(1, qk), p["gla_ab_b"].reshape(1, qk)


GLA_CHUNK = 256


def _gla_states_only(h, wi, p, d):
    k, = _mm(h, [wi["k"]], _epi_plain, (BF16,), name="proj_k")
    v, = _mm(h, [wi["v"]], _epi_plain, (BF16,), name="proj_v")
    a, = _mm(h, [wi["a"]], _epi_plain, (F32,), name="proj_a")
    ef, eb = _gla_decay(a, *_gate_matrices(p, d), chunk=min(GLA_CHUNK, h.shape[0]))
    dk, dv = d // 2 // GLA_HEADS, d // GLA_HEADS
    zero = jnp.zeros((GLA_HEADS, dv, dk), F32)
    _, s_f = _gla_scan(k, k, v, ef, zero, fwd=True, chunk=GLA_CHUNK)
    _, s_b = _gla_scan(k, k, v, eb, zero, fwd=False, chunk=GLA_CHUNK)
    return s_f, s_b


def _token_mixer(x, h, mod_gate, s0_f, s0_b, wi, p, d):
    l = h.shape[0]
    dk = d // 2 // GLA_HEADS
    k, = _mm(h, [wi["k"]], _epi_plain, (BF16,), name="proj_k")
    v, = _mm(h, [wi["v"]], _epi_plain, (BF16,), name="proj_v")
    a, = _mm(h, [wi["a"]], _epi_plain, (F32,), name="proj_a")
    q, = _mm(h, [wi["q"]], lambda accs, ex: [accs[0] * (dk ** -0.5)], (BF16,), name="proj_q")
    sg, = _mm(h, [wi["g"]], lambda accs, ex: [_silu(accs[0])], (BF16,), name="proj_g")
    hy, = _mm(h, [wi["hy"]], _epi_plain, (F32,), tn=1024, name="proj_hy")
    gates, = _mm(h, [wi["gate"]], lambda accs, ex: [jax.nn.sigmoid(accs[0])], (BF16,), tn=1024, name="proj_gate")

    ef, eb = _gla_decay(a, *_gate_matrices(p, d), chunk=min(GLA_CHUNK, l))
    o_f, s_f = _gla_scan(q, k, v, ef, s0_f, fwd=True, chunk=GLA_CHUNK)
    o_gla, s_b = _gla_scan(q, k, v, eb, s0_b, fwd=False, chunk=GLA_CHUNK, combine=(o_f, sg, p["gla_norm_g"]))

    x0, z = _hyena_shortconv(hy, p["hy_conv_w"], p["hy_conv_b"])
    hf, hg, ss = _hyena_filter(l, p)
    o_hy = _hyena_longconv(x0, z, hf, hg, ss, p["hy_bias"])

    t1, = _mm(o_hy, [p["w_up_hy"].astype(BF16)], lambda accs, ex: [ex[0].astype(F32) * accs[0]], (F32,),
              extras=[(gates, "tile", 0)], name="up_hy")
    merged, = _mm(o_gla, [p["w_up_gla"].astype(BF16)],
                  lambda accs, ex: [ex[1] + ex[0].astype(F32) * accs[0]], (BF16,),
                  extras=[(gates, "tile", d), (t1, "tile", 0)], name="up_gla_merge")
    x_new, = _mm(merged, [p["w_out"].astype(BF16)], lambda accs, ex: [ex[0] + ex[1] * accs[0]], (F32,),
                 extras=[(x, "tile", 0), (mod_gate.reshape(1, d), "row", 0)], name="out_proj")
    return x_new, s_f, s_b


def _dense_ffn(x, h, mod_gate, wg, wu, wd):
    d, f = wg.shape
    fp = -(-f // 512) * 512
    wg_b = _pad2(wg.astype(BF16), d, fp)
    wu_b = _pad2(wu.astype(BF16), d, fp)
    wd_b = _pad2(wd.astype(BF16), fp, d)
    act, = _mm(h, [wg_b, wu_b], lambda accs, ex: [_silu(accs[0]) * accs[1]], (BF16,), tn=512, name="ffn_up")
    x_new, = _mm(act, [wd_b], lambda accs, ex: [ex[0] + ex[1] * accs[0]], (F32,), tm=512,
                 extras=[(x, "tile", 0), (mod_gate.reshape(1, d), "row", 0)], name="ffn_down")
    return x_new


def kernel(x, c, ctx, c_ctx, ada_w, ada_b, norm_mix_g, norm_ffn_g, w_in, hy_conv_w, hy_conv_b, hy_w1, hy_b1, hy_w2, hy_b2, hy_w3, hy_b3, hy_w4, hy_freq, hy_bias, gla_aw_f, gla_ab_f, gla_aw_b, gla_ab_b, gla_norm_g, w_up_hy, w_up_gla, w_out, ffn_w_gate, ffn_w_up, ffn_w_down, router_w, exp_w_gate, exp_w_up, exp_w_down, final_norm_g):
    assert x.shape[0] == 1 and c.shape[0] == 1, "batch size 1 only"
    depth, d = norm_mix_g.shape
    x_lat, x_ctx = x[0], ctx[0]
    mods = _ada_modulation(jnp.concatenate([c, c_ctx.reshape(1, d)], axis=0), ada_w, ada_b)
    dk, dv = d // 2 // GLA_HEADS, d // GLA_HEADS
    per_layer = dict(hy_conv_w=hy_conv_w, hy_conv_b=hy_conv_b, hy_w1=hy_w1, hy_b1=hy_b1, hy_w2=hy_w2, hy_b2=hy_b2,
                     hy_w3=hy_w3, hy_b3=hy_b3, hy_w4=hy_w4, hy_freq=hy_freq, hy_bias=hy_bias,
                     gla_aw_f=gla_aw_f, gla_ab_f=gla_ab_f, gla_aw_b=gla_aw_b, gla_ab_b=gla_ab_b,
                     gla_norm_g=gla_norm_g, w_up_hy=w_up_hy, w_up_gla=w_up_gla, w_out=w_out)
    for l in range(depth):
        last = l == depth - 1
        p = {name: arr[l] for name, arr in per_layer.items()}
        wi = _in_weights(w_in[l], d)
        lat = [mods[l, 0, i * d:(i + 1) * d] for i in range(N_ADA)]
        cxm = [mods[l, 1, i * d:(i + 1) * d] for i in range(N_ADA)]

        h_ctx, = _modnorm(x_ctx, norm_mix_g[l], cxm[0], cxm[1], (BF16,))
        if last:
            s_f, s_b = _gla_states_only(h_ctx, wi, p, d)
        else:
            zero = jnp.zeros((GLA_HEADS, dv, dk), F32)
            x_ctx, s_f, s_b = _token_mixer(x_ctx, h_ctx, cxm[2], zero, zero, wi, p, d)
        h_lat, = _modnorm(x_lat, norm_mix_g[l], lat[0], lat[1], (BF16,))
        x_lat, _, _ = _token_mixer(x_lat, h_lat, lat[2], s_f, s_b, wi, p, d)

        i = l // 2
        if l % 2 == 0:
            h2, = _modnorm(x_lat, norm_ffn_g[l], lat[3], lat[4], (BF16,))
            x_lat = _dense_ffn(x_lat, h2, lat[5], ffn_w_gate[i], ffn_w_up[i], ffn_w_down[i])
            if not last:
                h2c, = _modnorm(x_ctx, norm_ffn_g[l], cxm[3], cxm[4], (BF16,))
                x_ctx = _dense_ffn(x_ctx, h2c, cxm[5], ffn_w_gate[i], ffn_w_up[i], ffn_w_down[i])
        else:
            pm = dict(router_w=router_w[i], exp_w_gate=exp_w_gate[i], exp_w_up=exp_w_up[i], exp_w_down=exp_w_down[i])
            h2, = _modnorm(x_lat, norm_ffn_g[l], lat[3], lat[4], (F32,))
            x_lat = _moe_ffn(h2, x_lat, lat[5], pm, final_norm_g, final_norm=last)
            if not last:
                h2c, = _modnorm(x_ctx, norm_ffn_g[l], cxm[3], cxm[4], (F32,))
                x_ctx = _moe_ffn(h2c, x_ctx, cxm[5], pm, final_norm_g, final_norm=False)
    if depth % 2 == 1:
        x_lat, = _modnorm(x_lat, final_norm_g, jnp.zeros((d,), F32), jnp.zeros((d,), F32), (F32,))
    return x_lat[None]
```

```python
import functools
import math

import numpy as np
import jax
import jax.numpy as jnp
from jax import lax
from jax.experimental import pallas as pl
from jax.experimental.pallas import tpu as pltpu

F32 = jnp.float32
BF16 = jnp.bfloat16
HIGHEST = lax.Precision.HIGHEST

EPS = 1e-6
N_ADA = 6
LANES = 128
SUBLANES = 8
VMEM_LIMIT_CAP = 60 * 1024 * 1024

GLA_HEADS = 4
GLA_GATE_RANK = 16
GLA_GATE_TEMP = 16.0
HY_SHORT = 3
HY_EMB_BANDS = 16
HY_FILTER_HIDDEN = 64
HY_DECAY_TARGET = 1e-2
HY_FAST_DECAY = 0.3
HY_SLOW_DECAY = 1.5
N_EXPERTS = 8
TOP_K = 2


def _cparams(sem, vmem_bytes):
    limit = int(min(max(vmem_bytes * 5 // 4 + (2 << 20), 16 << 20), VMEM_LIMIT_CAP))
    return pltpu.CompilerParams(dimension_semantics=sem, vmem_limit_bytes=limit)


def _nbytes(shape, dtype):
    return int(np.prod(shape)) * jnp.dtype(dtype).itemsize


def _ada_body(c_ref, w_ref, b_ref, o_ref):
    tn = o_ref.shape[-1]
    d = c_ref.shape[1]
    rows = []
    for r in range(2):
        s = c_ref[r]
        s = s * jax.nn.sigmoid(s)
        chunks = []
        for n0 in range(0, tn, LANES):
            p = w_ref[:, n0:n0 + LANES] * s
            acc = p.reshape(d // SUBLANES, SUBLANES, LANES).sum(axis=0)
            chunks.append(acc.sum(axis=0, keepdims=True))
        rows.append(jnp.concatenate(chunks, axis=1))
    o_ref[...] = jnp.concatenate(rows, axis=0) + b_ref[...]


def _ada_modulation(cond2, ada_w, ada_b):
    depth, d, n = ada_w.shape
    tn = 1536 if n % 1536 == 0 else LANES
    cb = jnp.broadcast_to(cond2[:, :, None], (2, d, LANES))
    vm = 2 * _nbytes((d, tn), F32) + 2 * _nbytes((2, d, LANES), F32)
    return pl.pallas_call(
        _ada_body,
        grid=(depth, n // tn),
        in_specs=[
            pl.BlockSpec((2, d, LANES), lambda l, j: (0, 0, 0)),
            pl.BlockSpec((None, d, tn), lambda l, j: (l, 0, j)),
            pl.BlockSpec((None, 1, tn), lambda l, j: (l, 0, j)),
        ],
        out_specs=pl.BlockSpec((None, 2, tn), lambda l, j: (l, 0, j)),
        out_shape=jax.ShapeDtypeStruct((depth, 2, n), F32),
        compiler_params=_cparams(("parallel", "parallel"), vm),
        name="ada_modulation",
    )(cb, ada_w, ada_b.reshape(depth, 1, n))


def _modnorm_body(x_ref, g_ref, sh_ref, sc_ref, *o_refs):
    x = x_ref[...]
    ms = jnp.mean(x * x, axis=-1, keepdims=True)
    y = x * lax.rsqrt(ms + EPS) * g_ref[...]
    y = y * (1.0 + sc_ref[...]) + sh_ref[...]
    for o in o_refs:
        o[...] = y.astype(o.dtype)


def _modnorm(x, g, shift, scale, out_dtypes, tm=256):
    m, d = x.shape
    tm = min(tm, m)
    row = pl.BlockSpec((1, d), lambda i: (0, 0))
    tile = pl.BlockSpec((tm, d), lambda i: (i, 0))
    vm = 2 * _nbytes((tm, d), F32) * (1 + len(out_dtypes))
    outs = pl.pallas_call(
        _modnorm_body,
        grid=(m // tm,),
        in_specs=[tile, row, row, row],
        out_specs=[tile] * len(out_dtypes),
        out_shape=[jax.ShapeDtypeStruct((m, d), dt) for dt in out_dtypes],
        compiler_params=_cparams(("parallel",), vm),
        name="modnorm",
    )(x, g.reshape(1, d), shift.reshape(1, d), scale.reshape(1, d))
    return outs


def _mm_body(*refs, n_w, n_e, epi, precision):
    x_ref = refs[0]
    w_refs = refs[1:1 + n_w]
    e_refs = refs[1 + n_w:1 + n_w + n_e]
    o_refs = refs[1 + n_w + n_e:]
    x = x_ref[...]
    if precision is None:
        x = x.astype(BF16)
    accs = [jnp.dot(x, w[...], preferred_element_type=F32, precision=precision) for w in w_refs]
    outs = epi(accs, [e[...] for e in e_refs])
    for o, v in zip(o_refs, outs):
        o[...] = v.astype(o.dtype)


def _mm(x, ws, epi, out_dtypes, extras=(), tm=1024, tn=1024, precision=None, name="mm"):
    m, k = x.shape
    n = ws[0].shape[1]
    tm = min(tm, m)
    tn = min(tn, n)
    assert m % tm == 0 and n % tn == 0, (m, tm, n, tn)
    in_specs = [pl.BlockSpec((tm, k), lambda j, i: (i, 0))]
    in_specs += [pl.BlockSpec((k, tn), lambda j, i: (0, j))] * len(ws)
    vm = 2 * _nbytes((tm, k), x.dtype) + 2 * len(ws) * _nbytes((k, tn), ws[0].dtype)
    for arr, kind, col in extras:
        assert col % tn == 0
        if kind == "tile":
            in_specs.append(pl.BlockSpec((tm, tn), lambda j, i, c=col // tn: (i, j + c)))
            vm += 2 * _nbytes((tm, tn), arr.dtype)
        else:
            in_specs.append(pl.BlockSpec((1, tn), lambda j, i, c=col // tn: (0, j + c)))
    vm += sum(2 * _nbytes((tm, tn), dt) for dt in out_dtypes) + (1 + len(ws)) * _nbytes((tm, tn), F32)
    return pl.pallas_call(
        functools.partial(_mm_body, n_w=len(ws), n_e=len(extras), epi=epi, precision=precision),
        grid=(n // tn, m // tm),
        in_specs=in_specs,
        out_specs=[pl.BlockSpec((tm, tn), lambda j, i: (i, j))] * len(out_dtypes),
        out_shape=[jax.ShapeDtypeStruct((m, n), dt) for dt in out_dtypes],
        compiler_params=_cparams(("parallel", "parallel"), vm),
        name=name,
    )(x, *ws, *[a for a, _, _ in extras])


def _log_sigmoid(z):
    return jnp.minimum(z, 0.0) - jnp.log1p(jnp.exp(-jnp.abs(z)))


def _split_bf16(x, terms):
    parts = []
    for _ in range(terms):
        p = x.astype(BF16)
        parts.append(p)
        x = x - p.astype(F32)
    return parts


def _gla_decay_body(a_ref, awf_ref, awb_ref, abf_ref, abb_ref, ef_ref, eb_ref):
    c = a_ref.shape[0]
    a = a_ref[...]
    r = lax.broadcasted_iota(jnp.int32, (c, c), 0)
    s = lax.broadcasted_iota(jnp.int32, (c, c), 1)
    lower = (s <= r).astype(BF16)
    upper = (s >= r).astype(BF16)
    a_parts = _split_bf16(a, 2)

    def gate_logits(w_ref, b_ref):
        w_hi, w_lo = _split_bf16(w_ref[...], 2)
        z = jnp.dot(a_parts[0], w_hi, preferred_element_type=F32)
        z = z + jnp.dot(a_parts[1], w_hi, preferred_element_type=F32)
        z = z + jnp.dot(a_parts[0], w_lo, preferred_element_type=F32)
        return z + b_ref[...]

    def chunk_sums(tri, g):
        return sum(jnp.dot(tri, part, preferred_element_type=F32) for part in _split_bf16(g, 3))

    gf = _log_sigmoid(gate_logits(awf_ref, abf_ref)) * (1.0 / GLA_GATE_TEMP)
    gb = _log_sigmoid(gate_logits(awb_ref, abb_ref)) * (1.0 / GLA_GATE_TEMP)
    ef_ref[...] = chunk_sums(lower, gf)
    eb_ref[...] = chunk_sums(upper, gb)


def _gla_decay(a, awf, awb, abf, abb, chunk, tn=512):
    l = a.shape[0]
    n = awf.shape[1]
    tn = min(tn, n)
    col = pl.BlockSpec((a.shape[1], tn), lambda i, j: (0, j))
    row = pl.BlockSpec((1, tn), lambda i, j: (0, j))
    out = pl.BlockSpec((chunk, tn), lambda i, j: (i, j))
    vm = 4 * _nbytes((chunk, tn), F32) * 3 + 4 * _nbytes((a.shape[1], tn), F32)
    return pl.pallas_call(
        _gla_decay_body,
        grid=(l // chunk, n // tn),
        in_specs=[pl.BlockSpec((chunk, a.shape[1]), lambda i, j: (i, 0)), col, col, row, row],
        out_specs=[out, out],
        out_shape=[jax.ShapeDtypeStruct((l, n), F32)] * 2,
        compiler_params=_cparams(("parallel", "parallel"), vm),
        name="gla_decay",
    )(a, awf, awb, abf, abb)


def _dot_nt(a, b):
    return lax.dot_general(a, b, (((1,), (1,)), ((), ())), preferred_element_type=F32)


def _dot_tn(a, b):
    return lax.dot_general(a, b, (((0,), (0,)), ((), ())), preferred_element_type=F32)


def _bcast_rows(e, group, row):
    c, w = e.shape
    e3 = e.reshape(c // group, group, w)
    return jnp.broadcast_to(e3[:, row:row + 1, :], e3.shape).reshape(c, w)


def _gla_masks(c, base, fwd):
    i = lax.broadcasted_iota(jnp.int32, (c, c), 0)
    j = lax.broadcasted_iota(jnp.int32, (c, c), 1)
    sh = int(math.log2(base))
    order = (j <= i) if fwd else (j >= i)
    masks = [((i >> sh) == (j >> sh)) & order]
    s = base
    while 2 * s <= c:
        sh += 1
        masks.append((i >> sh) == (j >> sh))
        s *= 2
    return masks


def _gla_chunk_head(q, k, v, e, st, masks, *, fwd, base):
    c, dk = q.shape
    row = lax.broadcasted_iota(jnp.int32, (c, 1), 0)
    d0 = e - _bcast_rows(e, base, base // 2 - 1 if fwd else base // 2)
    q0 = (q * jnp.exp(d0)).astype(BF16)
    k0 = (k * jnp.exp(-d0)).astype(BF16)
    att = jnp.where(masks[0], _dot_nt(q0, k0), 0.0)
    s, lvl = base, 1
    while 2 * s <= c:
        d = e - _bcast_rows(e, 2 * s, s - 1 if fwd else s)
        later = ((row >> int(math.log2(s))) & 1) == (1 if fwd else 0)
        x = jnp.exp(jnp.where(later, d, -d))
        ql = jnp.where(later, q * x, 0.0).astype(BF16)
        kl = jnp.where(later, 0.0, k * x).astype(BF16)
        att = att + jnp.where(masks[lvl], _dot_nt(ql, kl), 0.0)
        s *= 2
        lvl += 1
    e_edge = e[c - 1:c] if fwd else e[0:1]
    qs = (q * jnp.exp(e)).astype(BF16)
    ks = (k * jnp.exp(e_edge - e)).astype(BF16)
    o = jnp.dot(att.astype(BF16), v, preferred_element_type=F32) + _dot_nt(qs, st.astype(BF16))
    st_new = st * jnp.exp(e_edge) + _dot_tn(v, ks)
    return o, st_new


def _gla_scan_body(*refs, fwd, base, combine):
    if combine:
        q_ref, k_ref, v_ref, e_ref, s0_ref, of_ref, sg_ref, gn_ref, o_ref, sfin_ref, st_ref = refs
    else:
        q_ref, k_ref, v_ref, e_ref, s0_ref, o_ref, sfin_ref, st_ref = refs
    step = pl.program_id(0)
    c = q_ref.shape[0]
    dk = q_ref.shape[1] // GLA_HEADS
    dv = v_ref.shape[1] // GLA_HEADS

    @pl.when(step == 0)
    def _():
        st_ref[...] = s0_ref[...]

    masks = _gla_masks(c, base, fwd)
    for h in range(GLA_HEADS):
        ks = slice(h * dk, (h + 1) * dk)
        vs = slice(h * dv, (h + 1) * dv)
        o, st_new = _gla_chunk_head(
            q_ref[:, ks].astype(F32), k_ref[:, ks].astype(F32), v_ref[:, vs], e_ref[:, ks], st_ref[h],
            masks, fwd=fwd, base=base)
        st_ref[h] = st_new
        if combine:
            t = o + of_ref[:, vs]
            t = t * lax.rsqrt(jnp.mean(t * t, axis=-1, keepdims=True) + EPS) * gn_ref[...]
            o_ref[:, vs] = (t * sg_ref[:, vs].astype(F32)).astype(o_ref.dtype)
        else:
            o_ref[:, vs] = o.astype(o_ref.dtype)

    @pl.when(step == pl.num_programs(0) - 1)
    def _():
        sfin_ref[...] = st_ref[...]


def _gla_scan(q, k, v, e, s0, *, fwd, chunk, base=32, combine=None):
    l, hdk = q.shape
    hdv = v.shape[1]
    dk, dv = hdk // GLA_HEADS, hdv // GLA_HEADS
    chunk = min(chunk, l)
    n = l // chunk
    idx = (lambda i: (i, 0)) if fwd else (lambda i: (n - 1 - i, 0))
    st_spec = pl.BlockSpec((GLA_HEADS, dv, dk), lambda i: (0, 0, 0))
    in_specs = [pl.BlockSpec((chunk, hdk), idx), pl.BlockSpec((chunk, hdk), idx),
                pl.BlockSpec((chunk, hdv), idx), pl.BlockSpec((chunk, hdk), idx), st_spec]
    args = [q, k, v, e, s0]
    vm = 2 * (2 * _nbytes((chunk, hdk), BF16) + _nbytes((chunk, hdv), BF16) + _nbytes((chunk, hdk), F32))
    vm += 3 * _nbytes((GLA_HEADS, dv, dk), F32) * 2 + 2 * _nbytes((chunk, hdv), F32)
    if combine is not None:
        o_other, gate, norm_g = combine
        in_specs += [pl.BlockSpec((chunk, hdv), idx), pl.BlockSpec((chunk, hdv), idx),
                     pl.BlockSpec((1, dv), lambda i: (0, 0))]
        args += [o_other, gate, norm_g.reshape(1, dv)]
        vm += 2 * (_nbytes((chunk, hdv), F32) + _nbytes((chunk, hdv), BF16))
    vm += 24 * _nbytes((chunk, max(dk, chunk)), F32)
    return pl.pallas_call(
        functools.partial(_gla_scan_body, fwd=fwd, base=base, combine=combine is not None),
        grid=(n,),
        in_specs=in_specs,
        out_specs=[pl.BlockSpec((chunk, hdv), idx), st_spec],
        out_shape=[jax.ShapeDtypeStruct((l, hdv), BF16 if combine is not None else F32),
                   jax.ShapeDtypeStruct((GLA_HEADS, dv, dk), F32)],
        scratch_shapes=[pltpu.VMEM((GLA_HEADS, dv, dk), F32)],
        compiler_params=_cparams(("arbitrary",), vm),
        name="gla_scan_fwd" if fwd else "gla_scan_bwd",
    )(*args)


def _shortconv_body(u0, u1, u2, p0, p1, p2, n0, n1, n2, w0, w1, w2, b0, b1, b2, x0_ref, z_ref):
    i = pl.program_id(0)
    last = pl.num_programs(0) - 1
    tm = u0.shape[0]
    row = lax.broadcasted_iota(jnp.int32, (tm, 1), 0)

    def conv(u_ref, p_ref, n_ref, w_ref, b_ref):
        u = u_ref[...]
        prev_row = jnp.where(i == 0, 0.0, p_ref[SUBLANES - 1:SUBLANES, :])
        next_row = jnp.where(i == last, 0.0, n_ref[0:1, :])
        before = jnp.where(row == 0, prev_row, pltpu.roll(u, 1, axis=0))
        after = jnp.where(row == tm - 1, next_row, pltpu.roll(u, tm - 1, axis=0))
        return b_ref[...] + before * w_ref[0:1, :] + u * w_ref[1:2, :] + after * w_ref[2:3, :]

    x0_ref[...] = conv(u0, p0, n0, w0, b0)
    z_ref[...] = conv(u1, p1, n1, w1, b1) * conv(u2, p2, n2, w2, b2)


def _hyena_shortconv(hy, conv_w, conv_b, tm=512, cb=512):
    l, w3 = hy.shape
    w = w3 // 3
    tm = min(tm, l)
    cb = min(cb, w)
    nb = w // cb
    hb = tm // SUBLANES
    nrow8 = l // SUBLANES
    cur = [pl.BlockSpec((tm, cb), lambda i, j, g=g: (i, g * nb + j)) for g in range(3)]
    prv = [pl.BlockSpec((SUBLANES, cb), lambda i, j, g=g: (jnp.maximum(i * hb - 1, 0), g * nb + j)) for g in range(3)]
    nxt = [pl.BlockSpec((SUBLANES, cb), lambda i, j, g=g: (jnp.minimum((i + 1) * hb, nrow8 - 1), g * nb + j))
           for g in range(3)]
    wsp = [pl.BlockSpec((HY_SHORT, cb), lambda i, j, g=g: (0, g * nb + j)) for g in range(3)]
    bsp = [pl.BlockSpec((1, cb), lambda i, j, g=g: (0, g * nb + j)) for g in range(3)]
    out = pl.BlockSpec((tm, cb), lambda i, j: (i, j))
    vm = 2 * 5 * _nbytes((tm, cb), F32) + 8 * _nbytes((tm, cb), F32)
    return pl.pallas_call(
        _shortconv_body,
        grid=(l // tm, nb),
        in_specs=cur + prv + nxt + wsp + bsp,
        out_specs=[out, out],
        out_shape=[jax.ShapeDtypeStruct((l, w), F32)] * 2,
        compiler_params=_cparams(("parallel", "parallel"), vm),
        name="hyena_shortconv",
    )(hy, hy, hy, hy, hy, hy, hy, hy, hy, conv_w, conv_w, conv_w,
      conv_b.reshape(1, w3), conv_b.reshape(1, w3), conv_b.reshape(1, w3))


def _filter_body(w1_ref, b1_ref, w2_ref, b2_ref, w3_ref, b3_ref, fr_ref, w4f_ref, w4b_ref,
                 hf_ref, hg_ref, ss_ref, *, seq_len):
    i = pl.program_id(0)
    tr = hf_ref.shape[0]
    wdt = hf_ref.shape[1]
    pos = (lax.broadcasted_iota(jnp.int32, (tr, 1), 0) + i * tr).astype(F32)
    t = pos / float(max(seq_len - 1, 1))
    lane = lax.broadcasted_iota(jnp.int32, (1, LANES), 1)
    band = ((lane - 1) & (HY_EMB_BANDS - 1)).astype(F32)
    bands = 1e-4 + band * ((HY_EMB_BANDS - 1 - 1e-4) / (HY_EMB_BANDS - 1))
    ang = ((2.0 * math.pi / seq_len) * pos) * bands
    trig = jnp.cos(ang + jnp.where(lane > HY_EMB_BANDS, 0.5 * math.pi, 0.0))
    emb = jnp.where(lane == 0, t, jnp.where(lane <= 2 * HY_EMB_BANDS, trig, 0.0))
    fr = fr_ref[...]
    h = jnp.sin(fr * (jnp.dot(emb, w1_ref[...], preferred_element_type=F32, precision=HIGHEST) + b1_ref[...]))
    h = jnp.sin(fr * (jnp.dot(h, w2_ref[...], preferred_element_type=F32, precision=HIGHEST) + b2_ref[...]))
    h = jnp.sin(fr * (jnp.dot(h, w3_ref[...], preferred_element_type=F32, precision=HIGHEST) + b3_ref[...]))
    ch = lax.broadcasted_iota(jnp.int32, (1, wdt), 1).astype(F32)
    lo = math.log(HY_DECAY_TARGET) / HY_SLOW_DECAY
    hi = math.log(HY_DECAY_TARGET) / HY_FAST_DECAY
    deltas = jnp.abs(lo + ch * ((hi - lo) / (wdt - 1)))
    window = jnp.exp(-t * deltas)
    hb = h.astype(BF16)
    hf = jnp.dot(hb, w4f_ref[...].astype(BF16), preferred_element_type=F32) * window
    hg = jnp.dot(hb, w4b_ref[...].astype(BF16), preferred_element_type=F32) * window
    hg = jnp.where(pos == 0.0, 0.0, hg)
    hf_ref[...] = hf
    hg_ref[...] = hg

    @pl.when(i == 0)
    def _():
        ss_ref[...] = jnp.zeros_like(ss_ref)

    ss_ref[...] += jnp.sum(hf * hf + hg * hg, axis=0, keepdims=True)


def _pad2(a, rows, cols):
    return jnp.zeros((rows, cols), a.dtype).at[:a.shape[0], :a.shape[1]].set(a)


def _hyena_filter(seq_len, p, tr=256):
    wdt = p["hy_w4"].shape[1] // 2
    tr = min(tr, seq_len)
    hid = LANES
    w1 = _pad2(p["hy_w1"], LANES, hid)
    w2 = _pad2(p["hy_w2"], hid, hid)
    w3 = _pad2(p["hy_w3"], hid, hid)
    b1, b2, b3, fr = (_pad2(p[k].reshape(1, -1), 1, hid) for k in ("hy_b1", "hy_b2", "hy_b3", "hy_freq"))
    w4f = _pad2(p["hy_w4"][:, :wdt], hid, wdt)
    w4b = _pad2(p["hy_w4"][:, wdt:], hid, wdt)
    full = lambda a: pl.BlockSpec(a.shape, lambda i: (0, 0))
    out = pl.BlockSpec((tr, wdt), lambda i: (i, 0))
    args = (w1, b1, w2, b2, w3, b3, fr, w4f, w4b)
    vm = 4 * _nbytes((hid, wdt), F32) + 8 * _nbytes((tr, wdt), F32)
    return pl.pallas_call(
        functools.partial(_filter_body, seq_len=seq_len),
        grid=(seq_len // tr,),
        in_specs=[full(a) for a in args],
        out_specs=[out, out, pl.BlockSpec((1, wdt), lambda i: (0, 0))],
        out_shape=[jax.ShapeDtypeStruct((seq_len, wdt), F32)] * 2 + [jax.ShapeDtypeStruct((1, wdt), F32)],
        compiler_params=_cparams(("arbitrary",), vm),
        name="hyena_filter",
    )(*args)


def _fft_dims(seq_len):
    n = 2 * seq_len
    p = 1 << ((n.bit_length() - 1) // 2)
    return p, n // p


@functools.lru_cache(maxsize=None)
def _fft_consts(seq_len):
    pp, mm = _fft_dims(seq_len)
    n = pp * mm
    ph = pp // 2
    a = np.arange(ph)
    b = np.arange(mm)
    d = np.arange(pp)
    eye = np.eye(SUBLANES)
    ang = 2 * np.pi * np.outer(d, a) / pp
    fk = np.kron(np.concatenate([np.cos(ang), -np.sin(ang)], axis=0), eye)
    bt = b.reshape(mm // SUBLANES, 1, SUBLANES)
    angt = (2 * np.pi * d[None, :, None] * bt / n).reshape(mm // SUBLANES, pp * SUBLANES, 1)
    angc = -2 * np.pi * np.outer(b, b) / mm
    cr, ci = np.cos(angc), np.sin(angc)
    w2 = np.block([[cr, -ci], [ci, cr]])
    v2 = np.block([[cr, ci], [-ci, cr]])
    gk = np.kron(np.concatenate([np.cos(ang.T), -np.sin(ang.T)], axis=1) / n, eye)
    as_bf16 = lambda x: np.asarray(x, dtype=np.float32).astype(BF16)
    return dict(fk=as_bf16(fk), w2=as_bf16(w2), v2=as_bf16(v2), gk=as_bf16(gk),
                twc=np.cos(angt).astype(np.float32), tws=np.sin(angt).astype(np.float32), pp=pp, mm=mm)


def _stage1_to_scratch(a_ref, fk_ref, twc_ref, tws_ref, zt, t):
    pp = a_ref.shape[0] // 2
    lanes = zt.shape[-1]
    r = jnp.dot(fk_ref[...], zt, preferred_element_type=F32)
    rr, ri = r[:pp * SUBLANES], r[pp * SUBLANES:]
    c, s = twc_ref[...], tws_ref[...]
    a_ref[:pp, t] = (rr * c + ri * s).reshape(pp, SUBLANES, lanes)
    a_ref[pp:, t] = (ri * c - rr * s).reshape(pp, SUBLANES, lanes)


def _scratch_rows(a_ref, d):
    pp = a_ref.shape[0] // 2
    mm = a_ref.shape[1] * SUBLANES
    lanes = a_ref.shape[-1]
    return jnp.concatenate([a_ref[d].reshape(mm, lanes), a_ref[pp + d].reshape(mm, lanes)], axis=0).astype(BF16)


def _fft_spec_body(hf_ref, hg_ref, fk_ref, twc_ref, tws_ref, w2_ref, o_ref, a_ref, *, nb):
    t = pl.program_id(1)
    ph, _, cb = hf_ref.shape
    mm = a_ref.shape[1] * SUBLANES

    @pl.when(t < nb)
    def _():
        zt = jnp.concatenate([hf_ref[...].reshape(ph * SUBLANES, cb), hg_ref[...].reshape(ph * SUBLANES, cb)],
                             axis=-1).astype(BF16)
        _stage1_to_scratch(a_ref, fk_ref, twc_ref, tws_ref, zt, t)

    @pl.when(t >= nb)
    def _():
        for dl in range(SUBLANES):
            h = jnp.dot(w2_ref[...], _scratch_rows(a_ref, (t - nb) * SUBLANES + dl), preferred_element_type=F32)
            o_ref[dl, :mm, :] = (h[:mm, :cb] + h[:mm, cb:]).astype(o_ref.dtype)
            o_ref[dl, mm:, :] = (h[mm:, :cb] - h[mm:, cb:]).astype(o_ref.dtype)


def _phase_specs(k, cb, nb):
    pp, mm = k["pp"], k["mm"]
    ph = pp // 2
    step = lambda t: jnp.minimum(t, nb - 1)
    return (pl.BlockSpec((ph, SUBLANES, cb), lambda j, t: (0, step(t), j)),
            pl.BlockSpec(k["fk"].shape, lambda j, t: (0, 0)),
            pl.BlockSpec((None, pp * SUBLANES, 1), lambda j, t: (step(t), 0, 0)))


def _fft_filter_spectrum(hf, hg, k, cb=LANES):
    l, c = hf.shape
    pp, mm = k["pp"], k["mm"]
    ph = pp // 2
    cb = min(cb, c)
    nb, nd = mm // SUBLANES, pp // SUBLANES
    taps, fk_spec, tw_spec = _phase_specs(k, cb, nb)
    vm = (_nbytes((2 * pp, nb, SUBLANES, 2 * cb), F32) + 4 * _nbytes((ph, SUBLANES, cb), F32)
          + 2 * _nbytes(k["fk"].shape, BF16) + 2 * _nbytes((SUBLANES, 2 * mm, cb), BF16)
          + 5 * _nbytes((2 * pp * SUBLANES, 2 * cb), F32))
    return pl.pallas_call(
        functools.partial(_fft_spec_body, nb=nb),
        grid=(c // cb, nb + nd),
        in_specs=[taps, taps, fk_spec, tw_spec, tw_spec, pl.BlockSpec(k["w2"].shape, lambda j, t: (0, 0))],
        out_specs=pl.BlockSpec((SUBLANES, 2 * mm, cb), lambda j, t: (jnp.maximum(t - nb, 0), 0, j)),
        out_shape=jax.ShapeDtypeStruct((pp, 2 * mm, c), BF16),
        scratch_shapes=[pltpu.VMEM((2 * pp, nb, SUBLANES, 2 * cb), F32)],
        compiler_params=_cparams(("parallel", "arbitrary"), vm),
        name="fft_filter_spectrum",
    )(hf.reshape(ph, mm, c), hg.reshape(ph, mm, c), k["fk"], k["twc"], k["tws"], k["w2"])


def _fft_mid_body(z_ref, fk_ref, twc_ref, tws_ref, h_ref, w2_ref, v2_ref, o_ref, a_ref, *, nb):
    t = pl.program_id(1)
    ph, _, cb = z_ref.shape
    mm = a_ref.shape[1] * SUBLANES

    @pl.when(t < nb)
    def _():
        zt = z_ref[...].reshape(ph * SUBLANES, cb).astype(BF16)
        _stage1_to_scratch(a_ref, fk_ref, twc_ref, tws_ref, zt, t)

    @pl.when(t >= nb)
    def _():
        for dl in range(SUBLANES):
            x = jnp.dot(w2_ref[...], _scratch_rows(a_ref, (t - nb) * SUBLANES + dl), preferred_element_type=F32)
            xr, xi = x[:mm], x[mm:]
            hr, hi = h_ref[dl, :mm, :].astype(F32), h_ref[dl, mm:, :].astype(F32)
            y = jnp.concatenate([xr * hr - xi * hi, xr * hi + xi * hr], axis=0).astype(BF16)
            o_ref[dl] = jnp.dot(v2_ref[...], y, preferred_element_type=F32)


def _fft_mid(z, h, k, cb=256):
    l, c = z.shape
    pp, mm = k["pp"], k["mm"]
    ph = pp // 2
    cb = min(cb, c)
    nb, nd = mm // SUBLANES, pp // SUBLANES
    sig, fk_spec, tw_spec = _phase_specs(k, cb, nb)
    const = pl.BlockSpec(k["w2"].shape, lambda j, t: (0, 0))
    slab = pl.BlockSpec((SUBLANES, 2 * mm, cb), lambda j, t: (jnp.maximum(t - nb, 0), 0, j))
    vm = (_nbytes((2 * pp, nb, SUBLANES, cb), F32) + 2 * _nbytes((ph, SUBLANES, cb), F32)
          + 2 * _nbytes(k["fk"].shape, BF16) + 2 * _nbytes((SUBLANES, 2 * mm, cb), BF16)
          + 2 * _nbytes((SUBLANES, 2 * mm, cb), F32) + 5 * _nbytes((2 * pp * SUBLANES, cb), F32))
    return pl.pallas_call(
        functools.partial(_fft_mid_body, nb=nb),
        grid=(c // cb, nb + nd),
        in_specs=[sig, fk_spec, tw_spec, tw_spec, slab, const, const],
        out_specs=slab,
        out_shape=jax.ShapeDtypeStruct((pp, 2 * mm, c), F32),
        scratch_shapes=[pltpu.VMEM((2 * pp, nb, SUBLANES, cb), F32)],
        compiler_params=_cparams(("parallel", "arbitrary"), vm),
        name="fft_mid",
    )(z.reshape(ph, mm, c), k["fk"], k["twc"], k["tws"], h, k["w2"], k["v2"])


def _fft_last_body(bre, bim, gk_ref, twc_ref, tws_ref, x0_ref, z_ref, ss_ref, bias_ref, o_ref):
    pp, _, cb = bre.shape
    br = bre[...].reshape(pp * SUBLANES, cb)
    bi = bim[...].reshape(pp * SUBLANES, cb)
    c, s = twc_ref[...], tws_ref[...]
    rhs = jnp.concatenate([br * c - bi * s, br * s + bi * c], axis=0).astype(BF16)
    y = jnp.dot(gk_ref[...], rhs, preferred_element_type=F32).reshape(o_ref.shape)
    scale = lax.rsqrt(ss_ref[...] + EPS)
    o_ref[...] = x0_ref[...] * (y * scale + z_ref[...] * bias_ref[...])


def _fft_last(bmat, k, x0, z, ss, bias, cb=256):
    pp, m2, c = bmat.shape
    mm = m2 // 2
    ph = pp // 2
    cb = min(cb, c)
    nb = mm // SUBLANES
    view = pl.BlockSpec((ph, SUBLANES, cb), lambda j, b: (0, b, j))
    row = pl.BlockSpec((1, cb), lambda j, b: (0, j))
    tw_spec = pl.BlockSpec((None, pp * SUBLANES, 1), lambda j, b: (b, 0, 0))
    vm = 2 * (2 * _nbytes((pp, SUBLANES, cb), F32) + 3 * _nbytes((ph, SUBLANES, cb), F32)
              + _nbytes(k["gk"].shape, BF16)) + 5 * _nbytes((2 * pp * SUBLANES, cb), F32)
    out = pl.pallas_call(
        _fft_last_body,
        grid=(c // cb, nb),
        in_specs=[pl.BlockSpec((pp, SUBLANES, cb), lambda j, b: (0, b, j)),
                  pl.BlockSpec((pp, SUBLANES, cb), lambda j, b: (0, nb + b, j)),
                  pl.BlockSpec(k["gk"].shape, lambda j, b: (0, 0)),
                  tw_spec, tw_spec, view, view, row, row],
        out_specs=view,
        out_shape=jax.ShapeDtypeStruct((ph, mm, c), F32),
        compiler_params=_cparams(("parallel", "parallel"), vm),
        name="fft_last",
    )(bmat, bmat, k["gk"], k["twc"], k["tws"], x0.reshape(ph, mm, c), z.reshape(ph, mm, c), ss,
      bias.reshape(1, c))
    return out.reshape(ph * mm, c)


def _hyena_longconv(x0, z, hf, hg, ss, bias):
    k = _fft_consts(z.shape[0])
    h = _fft_filter_spectrum(hf, hg, k)
    return _fft_last(_fft_mid(z, h, k), k, x0, z, ss, bias)


def _router_body(h_ref, w_ref, o_ref):
    logits = jnp.dot(h_ref[...], w_ref[...], preferred_element_type=F32, precision=HIGHEST)
    lane = lax.broadcasted_iota(jnp.int32, logits.shape, 1)
    lg = jnp.where(lane < N_EXPERTS, logits, -jnp.inf)
    m1 = jnp.max(lg, axis=-1, keepdims=True)
    i1 = jnp.min(jnp.where(lg == m1, lane, LANES), axis=-1, keepdims=True)
    l2 = jnp.where(lane == i1, -jnp.inf, lg)
    m2 = jnp.max(l2, axis=-1, keepdims=True)
    i2 = jnp.min(jnp.where(l2 == m2, lane, LANES), axis=-1, keepdims=True)
    e = jnp.exp(m2 - m1)
    w1 = 1.0 / (1.0 + e)
    w2 = e * w1
    o_ref[...] = jnp.where(lane == 0, i1.astype(F32), jnp.where(lane == 1, i2.astype(F32),
                           jnp.where(lane == 2, w1, jnp.where(lane == 3, w2, 0.0))))


def _router(h, router_w, tm=256):
    m, d = h.shape
    tm = min(tm, m)
    w = _pad2(router_w, d, LANES)
    vm = 2 * (_nbytes((tm, d), F32) + _nbytes((d, LANES), F32)) + 8 * _nbytes((tm, LANES), F32)
    return pl.pallas_call(
        _router_body,
        grid=(m // tm,),
        in_specs=[pl.BlockSpec((tm, d), lambda i: (i, 0)), pl.BlockSpec((d, LANES), lambda i: (0, 0))],
        out_specs=pl.BlockSpec((tm, LANES), lambda i: (i, 0)),
        out_shape=jax.ShapeDtypeStruct((m, LANES), F32),
        compiler_params=_cparams(("parallel",), vm),
        name="moe_router",
    )(h, w)


def _routing_tables(route, tile):
    t = route.shape[0]
    e_flat = route[:, :TOP_K].astype(jnp.int32).reshape(-1)
    w_flat = route[:, TOP_K:2 * TOP_K].reshape(-1)
    onehot = (e_flat[:, None] == jnp.arange(N_EXPERTS, dtype=jnp.int32)[None, :]).astype(jnp.int32)
    csum = jnp.cumsum(onehot, axis=0)
    rank = jnp.take_along_axis(csum, e_flat[:, None], axis=1)[:, 0] - 1
    counts = csum[-1]
    padded = ((counts + tile - 1) // tile) * tile
    ends = jnp.cumsum(padded)
    pos = (ends - padded)[e_flat] + rank
    n_tiles = (t * TOP_K) // tile + N_EXPERTS
    rows = n_tiles * tile
    row_token = jnp.zeros((rows,), jnp.int32).at[pos].set(jnp.arange(t * TOP_K, dtype=jnp.int32) // TOP_K)
    row_w = jnp.zeros((rows,), F32).at[pos].set(w_flat)
    start = jnp.arange(n_tiles, dtype=jnp.int32) * tile
    valid = start < ends[-1]
    expert = jnp.minimum(jnp.sum((start[:, None] >= ends[None, :]).astype(jnp.int32), axis=1), N_EXPERTS - 1)
    last_valid = jnp.max(jnp.where(valid, expert, 0))
    expert = jnp.where(valid, expert, last_valid)
    changed = jnp.concatenate([jnp.ones((1,), bool), expert[1:] != expert[:-1]])
    is_start = changed & valid
    idx = jnp.arange(n_tiles, dtype=jnp.int32)
    later_start = jnp.where(is_start[None, :] & (idx[None, :] > idx[:, None]), idx[None, :], n_tiles)
    nxt = jnp.min(later_start, axis=1)
    wrap = (nxt == n_tiles).astype(jnp.int32)
    next_expert = expert[jnp.where(nxt == n_tiles, 0, nxt)]
    tiles = (expert, is_start.astype(jnp.int32), valid.astype(jnp.int32), next_expert, wrap)
    return row_token, row_w, pos.astype(jnp.int32), tiles


def _row_copy(src_hbm, dst, sem, src_row, dst_row):
    return pltpu.make_async_copy(src_hbm.at[pl.ds(src_row, 1)], dst.at[pl.ds(dst_row, 1)], sem)


DMA_LOOP_UNROLL = 8


def _gather_rows_body(tok_ref, h_hbm, o_ref, buf, sems):
    i = pl.program_id(0)
    gt = buf.shape[1]
    slot = i % 2

    def issue(step, s):
        def body(r, carry):
            _row_copy(h_hbm, buf.at[s], sems.at[s], tok_ref[step * gt + r], r).start()
            return carry
        lax.fori_loop(0, gt, body, 0, unroll=DMA_LOOP_UNROLL)

    def drain(s):
        def body(r, carry):
            _row_copy(h_hbm, buf.at[s], sems.at[s], 0, r).wait()
            return carry
        lax.fori_loop(0, gt, body, 0, unroll=DMA_LOOP_UNROLL)

    @pl.when(i == 0)
    def _():
        issue(0, 0)

    @pl.when(i + 1 < pl.num_programs(0))
    def _():
        issue(i + 1, 1 - slot)

    drain(slot)
    o_ref[...] = buf[slot].astype(o_ref.dtype)


def _gather_rows(h, row_token, gt=256):
    rows = row_token.shape[0]
    d = h.shape[1]
    vm = 4 * _nbytes((gt, d), F32)
    return pl.pallas_call(
        _gather_rows_body,
        grid_spec=pltpu.PrefetchScalarGridSpec(
            num_scalar_prefetch=1,
            grid=(rows // gt,),
            in_specs=[pl.BlockSpec(memory_space=pl.ANY)],
            out_specs=pl.BlockSpec((gt, d), lambda i, tok: (i, 0)),
            scratch_shapes=[pltpu.VMEM((2, gt, d), F32), pltpu.SemaphoreType.DMA((2,))],
        ),
        out_shape=jax.ShapeDtypeStruct((rows, d), BF16),
        compiler_params=_cparams(("arbitrary",), vm),
        name="moe_gather_rows",
    )(row_token, h)


def _group_weights(tiles, w_hbms, w_f32s, w_bf16s, sems):
    te_ref, ts_ref, _, ne_ref, wrap_ref = tiles
    j, i = pl.program_id(0), pl.program_id(1)
    tn = w_f32s[0].shape[1]

    def copies(expert, col_block):
        col = pl.multiple_of(col_block * tn, tn)
        return [pltpu.make_async_copy(w.at[expert, :, pl.ds(col, tn)], buf, sems.at[n])
                for n, (w, buf) in enumerate(zip(w_hbms, w_f32s))]

    @pl.when(ts_ref[i] == 1)
    def _():
        @pl.when((j == 0) & (i == 0))
        def _():
            for cp in copies(te_ref[i], j):
                cp.start()

        for cp in copies(te_ref[i], j):
            cp.wait()
        for src, dst in zip(w_f32s, w_bf16s):
            dst[...] = src[...].astype(BF16)
        nj = j + wrap_ref[i]

        @pl.when(nj < pl.num_programs(0))
        def _():
            for cp in copies(ne_ref[i], nj):
                cp.start()


def _moe_up_body(te, ts, tv, ne, wrap, x_ref, wg_hbm, wu_hbm, o_ref, wg32, wu32, wg_bf, wu_bf, sems):
    i = pl.program_id(1)
    _group_weights((te, ts, tv, ne, wrap), (wg_hbm, wu_hbm), (wg32, wu32), (wg_bf, wu_bf), sems)

    @pl.when(tv[i] == 1)
    def _():
        x = x_ref[...]
        g = jnp.dot(x, wg_bf[...], preferred_element_type=F32)
        u = jnp.dot(x, wu_bf[...], preferred_element_type=F32)
        o_ref[...] = (g * jax.nn.sigmoid(g) * u).astype(o_ref.dtype)

    @pl.when(tv[i] == 0)
    def _():
        o_ref[...] = jnp.zeros_like(o_ref)


def _moe_up(xs, wg, wu, tiles, tile, tn=1024):
    rows, d = xs.shape
    f = wg.shape[2]
    vm = 2 * _nbytes((d, tn), F32) + 2 * _nbytes((d, tn), BF16) + 2 * _nbytes((tile, d), BF16) + 6 * _nbytes((tile, tn), F32)
    return pl.pallas_call(
        _moe_up_body,
        grid_spec=pltpu.PrefetchScalarGridSpec(
            num_scalar_prefetch=len(tiles),
            grid=(f // tn, rows // tile),
            in_specs=[pl.BlockSpec((tile, d), lambda j, i, *_: (i, 0)),
                      pl.BlockSpec(memory_space=pl.ANY), pl.BlockSpec(memory_space=pl.ANY)],
            out_specs=pl.BlockSpec((tile, tn), lambda j, i, *_: (i, j)),
            scratch_shapes=[pltpu.VMEM((d, tn), F32), pltpu.VMEM((d, tn), F32),
                            pltpu.VMEM((d, tn), BF16), pltpu.VMEM((d, tn), BF16),
                            pltpu.SemaphoreType.DMA((2,))],
        ),
        out_shape=jax.ShapeDtypeStruct((rows, f), BF16),
        compiler_params=_cparams(("arbitrary", "arbitrary"), vm),
        name="moe_up",
    )(*tiles, xs, wg, wu)


def _moe_down_body(te, ts, tv, ne, wrap, a_ref, wd_hbm, rw_ref, o_ref, wd32, wd_bf, sems):
    i = pl.program_id(1)
    _group_weights((te, ts, tv, ne, wrap), (wd_hbm,), (wd32,), (wd_bf,), sems)

    @pl.when(tv[i] == 1)
    def _():
        o_ref[...] = jnp.dot(a_ref[...], wd_bf[...], preferred_element_type=F32) * rw_ref[...]

    @pl.when(tv[i] == 0)
    def _():
        o_ref[...] = jnp.zeros_like(o_ref)


def _moe_down(act, wd, row_w, tiles, tile, tn=512):
    rows, f = act.shape
    d = wd.shape[2]
    vm = _nbytes((f, tn), F32) + _nbytes((f, tn), BF16) + 2 * _nbytes((tile, f), BF16) + 4 * _nbytes((tile, tn), F32)
    return pl.pallas_call(
        _moe_down_body,
        grid_spec=pltpu.PrefetchScalarGridSpec(
            num_scalar_prefetch=len(tiles),
            grid=(d // tn, rows // tile),
            in_specs=[pl.BlockSpec((tile, f), lambda j, i, *_: (i, 0)),
                      pl.BlockSpec(memory_space=pl.ANY),
                      pl.BlockSpec((tile, 1), lambda j, i, *_: (i, 0))],
            out_specs=pl.BlockSpec((tile, tn), lambda j, i, *_: (i, j)),
            scratch_shapes=[pltpu.VMEM((f, tn), F32), pltpu.VMEM((f, tn), BF16), pltpu.SemaphoreType.DMA((1,))],
        ),
        out_shape=jax.ShapeDtypeStruct((rows, d), F32),
        compiler_params=_cparams(("arbitrary", "arbitrary"), vm),
        name="moe_down",
    )(*tiles, act, wd, row_w.reshape(rows, 1))


def _moe_combine_body(pos_ref, ys_hbm, x_ref, gate_ref, g_ref, o_ref, buf, sems, *, final_norm):
    i = pl.program_id(0)
    gt = x_ref.shape[0]
    slot = i % 2

    def issue(step, s):
        def body(r, carry):
            for k in range(TOP_K):
                _row_copy(ys_hbm, buf.at[s, k], sems.at[s], pos_ref[TOP_K * (step * gt + r) + k], r).start()
            return carry
        lax.fori_loop(0, gt, body, 0, unroll=DMA_LOOP_UNROLL)

    def drain(s):
        def body(r, carry):
            for k in range(TOP_K):
                _row_copy(ys_hbm, buf.at[s, k], sems.at[s], 0, r).wait()
            return carry
        lax.fori_loop(0, gt, body, 0, unroll=DMA_LOOP_UNROLL)

    @pl.when(i == 0)
    def _():
        issue(0, 0)

    @pl.when(i + 1 < pl.num_programs(0))
    def _():
        issue(i + 1, 1 - slot)

    drain(slot)
    y = buf[slot, 0]
    for k in range(1, TOP_K):
        y = y + buf[slot, k]
    x = x_ref[...] + gate_ref[...] * y
    if final_norm:
        x = x * lax.rsqrt(jnp.mean(x * x, axis=-1, keepdims=True) + EPS) * g_ref[...]
    o_ref[...] = x


def _moe_combine(ys, pos, x, gate, norm_g, final_norm, gt=128):
    t, d = x.shape
    gt = min(gt, t)
    row = pl.BlockSpec((1, d), lambda i, p: (0, 0))
    tilespec = pl.BlockSpec((gt, d), lambda i, p: (i, 0))
    vm = (2 * TOP_K + 6) * _nbytes((gt, d), F32)
    return pl.pallas_call(
        functools.partial(_moe_combine_body, final_norm=final_norm),
        grid_spec=pltpu.PrefetchScalarGridSpec(
            num_scalar_prefetch=1,
            grid=(t // gt,),
            in_specs=[pl.BlockSpec(memory_space=pl.ANY), tilespec, row, row],
            out_specs=tilespec,
            scratch_shapes=[pltpu.VMEM((2, TOP_K, gt, d), F32), pltpu.SemaphoreType.DMA((2,))],
        ),
        out_shape=jax.ShapeDtypeStruct((t, d), F32),
        compiler_params=_cparams(("arbitrary",), vm),
        name="moe_combine",
    )(pos, ys, x, gate.reshape(1, d), norm_g.reshape(1, d))


MOE_ROW_TILE = 256


def _moe_ffn(h32, x, gate, p, norm_g, final_norm):
    route = _router(h32, p["router_w"])
    row_token, row_w, pos, tiles = _routing_tables(route, MOE_ROW_TILE)
    xs = _gather_rows(h32, row_token)
    act = _moe_up(xs, p["exp_w_gate"], p["exp_w_up"], tiles, MOE_ROW_TILE)
    ys = _moe_down(act, p["exp_w_down"], row_w, tiles, MOE_ROW_TILE)
    return _moe_combine(ys, pos, x, gate, norm_g, final_norm)


def _silu(v):
    return v * jax.nn.sigmoid(v)


def _epi_plain(accs, ex):
    return [accs[0]]


def _in_weights(w_in, d):
    qk, vw, r = d // 2, d, GLA_GATE_RANK
    col_v = qk
    col_a = col_v + vw
    col_q = col_a + 2 * r
    col_g = col_q + qk
    col_hy = col_g + vw
    col_gate = col_hy + 3 * d
    cut = lambda a, b: w_in[:, a:b].astype(BF16)
    return dict(k=cut(0, col_v), v=cut(col_v, col_a), a=_pad2(cut(col_a, col_q), d, LANES),
                q=cut(col_q, col_g), g=cut(col_g, col_hy), hy=cut(col_hy, col_gate),
                gate=cut(col_gate, col_gate + 2 * d))


def _gate_matrices(p, d):
    r = GLA_GATE_RANK
    qk = d // 2
    awf = jnp.zeros((LANES, qk), F32).at[:r].set(p["gla_aw_f"])
    awb = jnp.zeros((LANES, qk), F32).at[r:2 * r].set(p["gla_aw_b"])
    return awf, awb, p["gla_ab_f"].reshape(1, qk), p["gla_ab_b"].reshape(1, qk)


GLA_CHUNK = 256


def _gla_states_only(h, wi, p, d):
    k, = _mm(h, [wi["k"]], _epi_plain, (BF16,), name="proj_k")
    v, = _mm(h, [wi["v"]], _epi_plain, (BF16,), name="proj_v")
    a, = _mm(h, [wi["a"]], _epi_plain, (F32,), name="proj_a")
    ef, eb = _gla_decay(a, *_gate_matrices(p, d), chunk=min(GLA_CHUNK, h.shape[0]))
    dk, dv = d // 2 // GLA_HEADS, d // GLA_HEADS
    zero = jnp.zeros((GLA_HEADS, dv, dk), F32)
    _, s_f = _gla_scan(k, k, v, ef, zero, fwd=True, chunk=GLA_CHUNK)
    _, s_b = _gla_scan(k, k, v, eb, zero, fwd=False, chunk=GLA_CHUNK)
    return s_f, s_b


def _token_mixer(x, h, mod_gate, s0_f, s0_b, wi, p, d):
    l = h.shape[0]
    dk = d // 2 // GLA_HEADS
    k, = _mm(h, [wi["k"]], _epi_plain, (BF16,), name="proj_k")
    v, = _mm(h, [wi["v"]], _epi_plain, (BF16,), name="proj_v")
    a, = _mm(h, [wi["a"]], _epi_plain, (F32,), name="proj_a")
    q, = _mm(h, [wi["q"]], lambda accs, ex: [accs[0] * (dk ** -0.5)], (BF16,), name="proj_q")
    sg, = _mm(h, [wi["g"]], lambda accs, ex: [_silu(accs[0])], (BF16,), name="proj_g")
    hy, = _mm(h, [wi["hy"]], _epi_plain, (F32,), tn=1024, name="proj_hy")
    gates, = _mm(h, [wi["gate"]], lambda accs, ex: [jax.nn.sigmoid(accs[0])], (BF16,), tn=1024, name="proj_gate")

    ef, eb = _gla_decay(a, *_gate_matrices(p, d), chunk=min(GLA_CHUNK, l))
    o_f, s_f = _gla_scan(q, k, v, ef, s0_f, fwd=True, chunk=GLA_CHUNK)
    o_gla, s_b = _gla_scan(q, k, v, eb, s0_b, fwd=False, chunk=GLA_CHUNK, combine=(o_f, sg, p["gla_norm_g"]))

    x0, z = _hyena_shortconv(hy, p["hy_conv_w"], p["hy_conv_b"])
    hf, hg, ss = _hyena_filter(l, p)
    o_hy = _hyena_longconv(x0, z, hf, hg, ss, p["hy_bias"])

    t1, = _mm(o_hy, [p["w_up_hy"].astype(BF16)], lambda accs, ex: [ex[0].astype(F32) * accs[0]], (F32,),
              extras=[(gates, "tile", 0)], name="up_hy")
    merged, = _mm(o_gla, [p["w_up_gla"].astype(BF16)],
                  lambda accs, ex: [ex[1] + ex[0].astype(F32) * accs[0]], (BF16,),
                  extras=[(gates, "tile", d), (t1, "tile", 0)], name="up_gla_merge")
    x_new, = _mm(merged, [p["w_out"].astype(BF16)], lambda accs, ex: [ex[0] + ex[1] * accs[0]], (F32,),
                 extras=[(x, "tile", 0), (mod_gate.reshape(1, d), "row", 0)], name="out_proj")
    return x_new, s_f, s_b


def _dense_ffn(x, h, mod_gate, wg, wu, wd):
    d, f = wg.shape
    fp = -(-f // 512) * 512
    wg_b = _pad2(wg.astype(BF16), d, fp)
    wu_b = _pad2(wu.astype(BF16), d, fp)
    wd_b = _pad2(wd.astype(BF16), fp, d)
    act, = _mm(h, [wg_b, wu_b], lambda accs, ex: [_silu(accs[0]) * accs[1]], (BF16,), tn=512, name="ffn_up")
    x_new, = _mm(act, [wd_b], lambda accs, ex: [ex[0] + ex[1] * accs[0]], (F32,), tm=512,
                 extras=[(x, "tile", 0), (mod_gate.reshape(1, d), "row", 0)], name="ffn_down")
    return x_new


def kernel(x, c, ctx, c_ctx, ada_w, ada_b, norm_mix_g, norm_ffn_g, w_in, hy_conv_w, hy_conv_b, hy_w1, hy_b1, hy_w2, hy_b2, hy_w3, hy_b3, hy_w4, hy_freq, hy_bias, gla_aw_f, gla_ab_f, gla_aw_b, gla_ab_b, gla_norm_g, w_up_hy, w_up_gla, w_out, ffn_w_gate, ffn_w_up, ffn_w_down, router_w, exp_w_gate, exp_w_up, exp_w_down, final_norm_g):
    assert x.shape[0] == 1 and c.shape[0] == 1, "batch size 1 only"
    depth, d = norm_mix_g.shape
    x_lat, x_ctx = x[0], ctx[0]
    mods = _ada_modulation(jnp.concatenate([c, c_ctx.reshape(1, d)], axis=0), ada_w, ada_b)
    dk, dv = d // 2 // GLA_HEADS, d // GLA_HEADS
    per_layer = dict(hy_conv_w=hy_conv_w, hy_conv_b=hy_conv_b, hy_w1=hy_w1, hy_b1=hy_b1, hy_w2=hy_w2, hy_b2=hy_b2,
                     hy_w3=hy_w3, hy_b3=hy_b3, hy_w4=hy_w4, hy_freq=hy_freq, hy_bias=hy_bias,
                     gla_aw_f=gla_aw_f, gla_ab_f=gla_ab_f, gla_aw_b=gla_aw_b, gla_ab_b=gla_ab_b,
                     gla_norm_g=gla_norm_g, w_up_hy=w_up_hy, w_up_gla=w_up_gla, w_out=w_out)
    for l in range(depth):
        last = l == depth - 1
        p = {name: arr[l] for name, arr in per_layer.items()}
        wi = _in_weights(w_in[l], d)
        lat = [mods[l, 0, i * d:(i + 1) * d] for i in range(N_ADA)]
        cxm = [mods[l, 1, i * d:(i + 1) * d] for i in range(N_ADA)]

        h_ctx, = _modnorm(x_ctx, norm_mix_g[l], cxm[0], cxm[1], (BF16,))
        if last:
            s_f, s_b = _gla_states_only(h_ctx, wi, p, d)
        else:
            zero = jnp.zeros((GLA_HEADS, dv, dk), F32)
            x_ctx, s_f, s_b = _token_mixer(x_ctx, h_ctx, cxm[2], zero, zero, wi, p, d)
        h_lat, = _modnorm(x_lat, norm_mix_g[l], lat[0], lat[1], (BF16,))
        x_lat, _, _ = _token_mixer(x_lat, h_lat, lat[2], s_f, s_b, wi, p, d)

        i = l // 2
        if l % 2 == 0:
            h2, = _modnorm(x_lat, norm_ffn_g[l], lat[3], lat[4], (BF16,))
            x_lat = _dense_ffn(x_lat, h2, lat[5], ffn_w_gate[i], ffn_w_up[i], ffn_w_down[i])
            if not last:
                h2c, = _modnorm(x_ctx, norm_ffn_g[l], cxm[3], cxm[4], (BF16,))
                x_ctx = _dense_ffn(x_ctx, h2c, cxm[5], ffn_w_gate[i], ffn_w_up[i], ffn_w_down[i])
        else:
            pm = dict(router_w=router_w[i], exp_w_gate=exp_w_gate[i], exp_w_up=exp_w_up[i], exp_w_down=exp_w_down[i])
            h2, = _modnorm(x_lat, norm_ffn_g[l], lat[3], lat[4], (F32,))
            x_lat = _moe_ffn(h2, x_lat, lat[5], pm, final_norm_g, final_norm=last)
            if not last:
                h2c, = _modnorm(x_ctx, norm_ffn_g[l], cxm[3], cxm[4], (F32,))
                x_ctx = _moe_ffn(h2c, x_ctx, cxm[5], pm, final_norm_g, final_norm=False)
    if depth % 2 == 1:
        x_lat, = _modnorm(x_lat, final_norm_g, jnp.zeros((d,), F32), jnp.zeros((d,), F32), (F32,))
    return x_lat[None]
```

```python
import functools
import math

import numpy as np
import jax
import jax.numpy as jnp
from jax import lax
from jax.experimental import pallas as pl
from jax.experimental.pallas import tpu as pltpu

F32 = jnp.float32
BF16 = jnp.bfloat16
HIGHEST = lax.Precision.HIGHEST

EPS = 1e-6
N_ADA = 6
LANES = 128
SUBLANES = 8
VMEM_LIMIT_CAP = 60 * 1024 * 1024

GLA_HEADS = 4
GLA_GATE_RANK = 16
GLA_GATE_TEMP = 16.0
HY_SHORT = 3
HY_EMB_BANDS = 16
HY_FILTER_HIDDEN = 64
HY_DECAY_TARGET = 1e-2
HY_FAST_DECAY = 0.3
HY_SLOW_DECAY = 1.5
N_EXPERTS = 8
TOP_K = 2


def _cparams(sem, vmem_bytes):
    limit = int(min(max(vmem_bytes * 5 // 4 + (2 << 20), 16 << 20), VMEM_LIMIT_CAP))
    return pltpu.CompilerParams(dimension_semantics=sem, vmem_limit_bytes=limit)


def _nbytes(shape, dtype):
    return int(np.prod(shape)) * jnp.dtype(dtype).itemsize


def _ada_body(c_ref, w_ref, b_ref, o_ref):
    tn = o_ref.shape[-1]
    d = c_ref.shape[1]
    rows = []
    for r in range(2):
        s = c_ref[r]
        s = s * jax.nn.sigmoid(s)
        chunks = []
        for n0 in range(0, tn, LANES):
            p = w_ref[:, n0:n0 + LANES] * s
            acc = p.reshape(d // SUBLANES, SUBLANES, LANES).sum(axis=0)
            chunks.append(acc.sum(axis=0, keepdims=True))
        rows.append(jnp.concatenate(chunks, axis=1))
    o_ref[...] = jnp.concatenate(rows, axis=0) + b_ref[...]


def _ada_modulation(cond2, ada_w, ada_b):
    depth, d, n = ada_w.shape
    tn = 1536 if n % 1536 == 0 else LANES
    cb = jnp.broadcast_to(cond2[:, :, None], (2, d, LANES))
    vm = 2 * _nbytes((d, tn), F32) + 2 * _nbytes((2, d, LANES), F32)
    return pl.pallas_call(
        _ada_body,
        grid=(depth, n // tn),
        in_specs=[
            pl.BlockSpec((2, d, LANES), lambda l, j: (0, 0, 0)),
            pl.BlockSpec((None, d, tn), lambda l, j: (l, 0, j)),
            pl.BlockSpec((None, 1, tn), lambda l, j: (l, 0, j)),
        ],
        out_specs=pl.BlockSpec((None, 2, tn), lambda l, j: (l, 0, j)),
        out_shape=jax.ShapeDtypeStruct((depth, 2, n), F32),
        compiler_params=_cparams(("parallel", "parallel"), vm),
        name="ada_modulation",
    )(cb, ada_w, ada_b.reshape(depth, 1, n))


def _modnorm_body(x_ref, g_ref, sh_ref, sc_ref, *o_refs):
    x = x_ref[...]
    ms = jnp.mean(x * x, axis=-1, keepdims=True)
    y = x * lax.rsqrt(ms + EPS) * g_ref[...]
    y = y * (1.0 + sc_ref[...]) + sh_ref[...]
    for o in o_refs:
        o[...] = y.astype(o.dtype)


def _modnorm(x, g, shift, scale, out_dtypes, tm=256):
    m, d = x.shape
    tm = min(tm, m)
    row = pl.BlockSpec((1, d), lambda i: (0, 0))
    tile = pl.BlockSpec((tm, d), lambda i: (i, 0))
    vm = 2 * _nbytes((tm, d), F32) * (1 + len(out_dtypes))
    outs = pl.pallas_call(
        _modnorm_body,
        grid=(m // tm,),
        in_specs=[tile, row, row, row],
        out_specs=[tile] * len(out_dtypes),
        out_shape=[jax.ShapeDtypeStruct((m, d), dt) for dt in out_dtypes],
        compiler_params=_cparams(("parallel",), vm),
        name="modnorm",
    )(x, g.reshape(1, d), shift.reshape(1, d), scale.reshape(1, d))
    return outs


def _mm_body(*refs, n_w, n_e, n_o, epi, precision, cast_w):
    x_ref = refs[0]
    w_refs = refs[1:1 + n_w]
    e_refs = refs[1 + n_w:1 + n_w + n_e]
    o_refs = refs[1 + n_w + n_e:1 + n_w + n_e + n_o]
    w_bf16 = refs[1 + n_w + n_e + n_o:]
    x = x_ref[...]
    if precision is None:
        x = x.astype(BF16)
    if cast_w:
        @pl.when(pl.program_id(1) == 0)
        def _():
            for src, dst in zip(w_refs, w_bf16):
                dst[...] = src[...].astype(BF16)
        w_refs = w_bf16
    accs = [jnp.dot(x, w[...], preferred_element_type=F32, precision=precision) for w in w_refs]
    outs = epi(accs, [e[...] for e in e_refs])
    for o, v in zip(o_refs, outs):
        o[...] = v.astype(o.dtype)


def _mm(x, ws, epi, out_dtypes, extras=(), n=None, w_col=0, w_layer=None, tm=1024, tn=1024, precision=None,
        name="mm"):
    m, k = x.shape
    n = ws[0].shape[-1] if n is None else n
    tm = min(tm, m)
    tn = min(tn, n)
    assert m % tm == 0 and n % tn == 0 and w_col % tn == 0, (m, tm, n, tn, w_col)
    cast_w = precision is None and ws[0].dtype == F32
    in_specs = [pl.BlockSpec((tm, k), lambda j, i: (i, 0))]
    if w_layer is None:
        in_specs += [pl.BlockSpec((k, tn), lambda j, i: (0, j + w_col // tn))] * len(ws)
    else:
        in_specs += [pl.BlockSpec((None, k, tn), lambda j, i: (w_layer, 0, j + w_col // tn))] * len(ws)
    vm = 2 * _nbytes((tm, k), x.dtype) + 2 * len(ws) * _nbytes((k, tn), ws[0].dtype)
    for arr, kind, col in extras:
        assert col % tn == 0
        if kind == "tile":
            in_specs.append(pl.BlockSpec((tm, tn), lambda j, i, c=col // tn: (i, j + c)))
            vm += 2 * _nbytes((tm, tn), arr.dtype)
        else:
            in_specs.append(pl.BlockSpec((1, tn), lambda j, i, c=col // tn: (0, j + c)))
    vm += sum(2 * _nbytes((tm, tn), dt) for dt in out_dtypes) + (1 + len(ws)) * _nbytes((tm, tn), F32)
    scratch = [pltpu.VMEM((k, tn), BF16)] * len(ws) if cast_w else []
    vm += len(scratch) * _nbytes((k, tn), BF16)
    return pl.pallas_call(
        functools.partial(_mm_body, n_w=len(ws), n_e=len(extras), n_o=len(out_dtypes), epi=epi,
                          precision=precision, cast_w=cast_w),
        grid=(n // tn, m // tm),
        in_specs=in_specs,
        out_specs=[pl.BlockSpec((tm, tn), lambda j, i: (i, j))] * len(out_dtypes),
        out_shape=[jax.ShapeDtypeStruct((m, n), dt) for dt in out_dtypes],
        scratch_shapes=scratch,
        compiler_params=_cparams(("parallel", "arbitrary" if cast_w else "parallel"), vm),
        name=name,
    )(x, *ws, *[a for a, _, _ in extras])


def _log_sigmoid(z):
    return jnp.minimum(z, 0.0) - jnp.log1p(jnp.exp(-jnp.abs(z)))


def _split_bf16(x, terms):
    parts = []
    for _ in range(terms):
        p = x.astype(BF16)
        parts.append(p)
        x = x - p.astype(F32)
    return parts


def _gla_decay_body(a_ref, awf_ref, awb_ref, abf_ref, abb_ref, ef_ref, eb_ref):
    c = a_ref.shape[0]
    a = a_ref[...]
    r = lax.broadcasted_iota(jnp.int32, (c, c), 0)
    s = lax.broadcasted_iota(jnp.int32, (c, c), 1)
    lower = (s <= r).astype(BF16)
    upper = (s >= r).astype(BF16)
    a_parts = _split_bf16(a, 2)

    def gate_logits(w_ref, b_ref):
        w_hi, w_lo = _split_bf16(w_ref[...], 2)
        z = jnp.dot(a_parts[0], w_hi, preferred_element_type=F32)
        z = z + jnp.dot(a_parts[1], w_hi, preferred_element_type=F32)
        z = z + jnp.dot(a_parts[0], w_lo, preferred_element_type=F32)
        return z + b_ref[...]

    def chunk_sums(tri, g):
        return sum(jnp.dot(tri, part, preferred_element_type=F32) for part in _split_bf16(g, 3))

    gf = _log_sigmoid(gate_logits(awf_ref, abf_ref)) * (1.0 / GLA_GATE_TEMP)
    gb = _log_sigmoid(gate_logits(awb_ref, abb_ref)) * (1.0 / GLA_GATE_TEMP)
    ef_ref[...] = chunk_sums(lower, gf)
    eb_ref[...] = chunk_sums(upper, gb)


def _gla_decay(a, awf, awb, abf, abb, chunk, tn=512):
    l = a.shape[0]
    n = awf.shape[1]
    tn = min(tn, n)
    col = pl.BlockSpec((a.shape[1], tn), lambda i, j: (0, j))
    row = pl.BlockSpec((1, tn), lambda i, j: (0, j))
    out = pl.BlockSpec((chunk, tn), lambda i, j: (i, j))
    vm = 4 * _nbytes((chunk, tn), F32) * 3 + 4 * _nbytes((a.shape[1], tn), F32)
    return pl.pallas_call(
        _gla_decay_body,
        grid=(l // chunk, n // tn),
        in_specs=[pl.BlockSpec((chunk, a.shape[1]), lambda i, j: (i, 0)), col, col, row, row],
        out_specs=[out, out],
        out_shape=[jax.ShapeDtypeStruct((l, n), F32)] * 2,
        compiler_params=_cparams(("parallel", "parallel"), vm),
        name="gla_decay",
    )(a, awf, awb, abf, abb)


def _dot_nt(a, b):
    return lax.dot_general(a, b, (((1,), (1,)), ((), ())), preferred_element_type=F32)


def _dot_tn(a, b):
    return lax.dot_general(a, b, (((0,), (0,)), ((), ())), preferred_element_type=F32)


def _bcast_rows(e, group, row):
    c, w = e.shape
    e3 = e.reshape(c // group, group, w)
    return jnp.broadcast_to(e3[:, row:row + 1, :], e3.shape).reshape(c, w)


def _gla_masks(c, base, fwd):
    i = lax.broadcasted_iota(jnp.int32, (c, c), 0)
    j = lax.broadcasted_iota(jnp.int32, (c, c), 1)
    sh = int(math.log2(base))
    order = (j <= i) if fwd else (j >= i)
    masks = [((i >> sh) == (j >> sh)) & order]
    s = base
    while 2 * s <= c:
        sh += 1
        masks.append((i >> sh) == (j >> sh))
        s *= 2
    return masks


def _gla_chunk_head(q, k, v, e, st, masks, *, fwd, base):
    c, dk = q.shape
    row = lax.broadcasted_iota(jnp.int32, (c, 1), 0)
    d0 = e - _bcast_rows(e, base, base // 2 - 1 if fwd else base // 2)
    q0 = (q * jnp.exp(d0)).astype(BF16)
    k0 = (k * jnp.exp(-d0)).astype(BF16)
    att = jnp.where(masks[0], _dot_nt(q0, k0), 0.0)
    s, lvl = base, 1
    while 2 * s <= c:
        d = e - _bcast_rows(e, 2 * s, s - 1 if fwd else s)
        later = ((row >> int(math.log2(s))) & 1) == (1 if fwd else 0)
        x = jnp.exp(jnp.where(later, d, -d))
        ql = jnp.where(later, q * x, 0.0).astype(BF16)
        kl = jnp.where(later, 0.0, k * x).astype(BF16)
        att = att + jnp.where(masks[lvl], _dot_nt(ql, kl), 0.0)
        s *= 2
        lvl += 1
    e_edge = e[c - 1:c] if fwd else e[0:1]
    qs = (q * jnp.exp(e)).astype(BF16)
    ks = (k * jnp.exp(e_edge - e)).astype(BF16)
    o = jnp.dot(att.astype(BF16), v, preferred_element_type=F32) + _dot_nt(qs, st.astype(BF16))
    st_new = st * jnp.exp(e_edge) + _dot_tn(v, ks)
    return o, st_new


def _gla_scan_body(*refs, fwd, base, combine):
    if combine:
        q_ref, k_ref, v_ref, e_ref, s0_ref, of_ref, sg_ref, gn_ref, o_ref, sfin_ref, st_ref = refs
    else:
        q_ref, k_ref, v_ref, e_ref, s0_ref, o_ref, sfin_ref, st_ref = refs
    step = pl.program_id(0)
    c = q_ref.shape[0]
    dk = q_ref.shape[1] // GLA_HEADS
    dv = v_ref.shape[1] // GLA_HEADS

    @pl.when(step == 0)
    def _():
        st_ref[...] = s0_ref[...]

    masks = _gla_masks(c, base, fwd)
    for h in range(GLA_HEADS):
        ks = slice(h * dk, (h + 1) * dk)
        vs = slice(h * dv, (h + 1) * dv)
        o, st_new = _gla_chunk_head(
            q_ref[:, ks].astype(F32), k_ref[:, ks].astype(F32), v_ref[:, vs], e_ref[:, ks], st_ref[h],
            masks, fwd=fwd, base=base)
        st_ref[h] = st_new
        if combine:
            t = o + of_ref[:, vs]
            t = t * lax.rsqrt(jnp.mean(t * t, axis=-1, keepdims=True) + EPS) * gn_ref[...]
            o_ref[:, vs] = (t * sg_ref[:, vs].astype(F32)).astype(o_ref.dtype)
        else:
            o_ref[:, vs] = o.astype(o_ref.dtype)

    @pl.when(step == pl.num_programs(0) - 1)
    def _():
        sfin_ref[...] = st_ref[...]


def _gla_scan(q, k, v, e, s0, *, fwd, chunk, base=32, combine=None):
    l, hdk = q.shape
    hdv = v.shape[1]
    dk, dv = hdk // GLA_HEADS, hdv // GLA_HEADS
    chunk = min(chunk, l)
    n = l // chunk
    idx = (lambda i: (i, 0)) if fwd else (lambda i: (n - 1 - i, 0))
    st_spec = pl.BlockSpec((GLA_HEADS, dv, dk), lambda i: (0, 0, 0))
    in_specs = [pl.BlockSpec((chunk, hdk), idx), pl.BlockSpec((chunk, hdk), idx),
                pl.BlockSpec((chunk, hdv), idx), pl.BlockSpec((chunk, hdk), idx), st_spec]
    args = [q, k, v, e, s0]
    vm = 2 * (2 * _nbytes((chunk, hdk), BF16) + _nbytes((chunk, hdv), BF16) + _nbytes((chunk, hdk), F32))
    vm += 3 * _nbytes((GLA_HEADS, dv, dk), F32) * 2 + 2 * _nbytes((chunk, hdv), F32)
    if combine is not None:
        o_other, gate, norm_g = combine
        in_specs += [pl.BlockSpec((chunk, hdv), idx), pl.BlockSpec((chunk, hdv), idx),
                     pl.BlockSpec((1, dv), lambda i: (0, 0))]
        args += [o_other, gate, norm_g.reshape(1, dv)]
        vm += 2 * (_nbytes((chunk, hdv), F32) + _nbytes((chunk, hdv), BF16))
    vm += 24 * _nbytes((chunk, max(dk, chunk)), F32)
    return pl.pallas_call(
        functools.partial(_gla_scan_body, fwd=fwd, base=base, combine=combine is not None),
        grid=(n,),
        in_specs=in_specs,
        out_specs=[pl.BlockSpec((chunk, hdv), idx), st_spec],
        out_shape=[jax.ShapeDtypeStruct((l, hdv), BF16 if combine is not None else F32),
                   jax.ShapeDtypeStruct((GLA_HEADS, dv, dk), F32)],
        scratch_shapes=[pltpu.VMEM((GLA_HEADS, dv, dk), F32)],
        compiler_params=_cparams(("arbitrary",), vm),
        name="gla_scan_fwd" if fwd else "gla_scan_bwd",
    )(*args)


def _shortconv_body(u0, u1, u2, p0, p1, p2, n0, n1, n2, w0, w1, w2, b0, b1, b2, x0_ref, z_ref):
    i = pl.program_id(0)
    last = pl.num_programs(0) - 1
    tm = u0.shape[0]
    row = lax.broadcasted_iota(jnp.int32, (tm, 1), 0)

    def conv(u_ref, p_ref, n_ref, w_ref, b_ref):
        u = u_ref[...].astype(F32)
        halo = p_ref.shape[0]
        prev_row = jnp.where(i == 0, 0.0, p_ref[...].astype(F32)[halo - 1:halo, :])
        next_row = jnp.where(i == last, 0.0, n_ref[...].astype(F32)[0:1, :])
        before = jnp.where(row == 0, prev_row, pltpu.roll(u, 1, axis=0))
        after = jnp.where(row == tm - 1, next_row, pltpu.roll(u, tm - 1, axis=0))
        return b_ref[...] + before * w_ref[0:1, :] + u * w_ref[1:2, :] + after * w_ref[2:3, :]

    x0_ref[...] = conv(u0, p0, n0, w0, b0).astype(x0_ref.dtype)
    z_ref[...] = conv(u1, p1, n1, w1, b1) * conv(u2, p2, n2, w2, b2)


def _hyena_shortconv(hy, conv_w, conv_b, tm=512, cb=512):
    l, w3 = hy.shape
    w = w3 // 3
    tm = min(tm, l)
    cb = min(cb, w)
    nb = w // cb
    halo = SUBLANES * 4 // jnp.dtype(hy.dtype).itemsize
    hb = tm // halo
    n_halo = l // halo
    cur = [pl.BlockSpec((tm, cb), lambda i, j, g=g: (i, g * nb + j)) for g in range(3)]
    prv = [pl.BlockSpec((halo, cb), lambda i, j, g=g: (jnp.maximum(i * hb - 1, 0), g * nb + j)) for g in range(3)]
    nxt = [pl.BlockSpec((halo, cb), lambda i, j, g=g: (jnp.minimum((i + 1) * hb, n_halo - 1), g * nb + j))
           for g in range(3)]
    wsp = [pl.BlockSpec((HY_SHORT, cb), lambda i, j, g=g: (0, g * nb + j)) for g in range(3)]
    bsp = [pl.BlockSpec((1, cb), lambda i, j, g=g: (0, g * nb + j)) for g in range(3)]
    out = pl.BlockSpec((tm, cb), lambda i, j: (i, j))
    vm = 2 * 5 * _nbytes((tm, cb), F32) + 8 * _nbytes((tm, cb), F32)
    return pl.pallas_call(
        _shortconv_body,
        grid=(l // tm, nb),
        in_specs=cur + prv + nxt + wsp + bsp,
        out_specs=[out, out],
        out_shape=[jax.ShapeDtypeStruct((l, w), BF16), jax.ShapeDtypeStruct((l, w), F32)],
        compiler_params=_cparams(("parallel", "parallel"), vm),
        name="hyena_shortconv",
    )(hy, hy, hy, hy, hy, hy, hy, hy, hy, conv_w, conv_w, conv_w,
      conv_b.reshape(1, w3), conv_b.reshape(1, w3), conv_b.reshape(1, w3))


def _filter_body(w1_ref, b1_ref, w2_ref, b2_ref, w3_ref, b3_ref, fr_ref, w4f_ref, w4b_ref,
                 hf_ref, hg_ref, ss_ref, *, seq_len):
    i = pl.program_id(0)
    tr = hf_ref.shape[0]
    wdt = hf_ref.shape[1]
    pos = (lax.broadcasted_iota(jnp.int32, (tr, 1), 0) + i * tr).astype(F32)
    t = pos / float(max(seq_len - 1, 1))
    lane = lax.broadcasted_iota(jnp.int32, (1, LANES), 1)
    band = ((lane - 1) & (HY_EMB_BANDS - 1)).astype(F32)
    bands = 1e-4 + band * ((HY_EMB_BANDS - 1 - 1e-4) / (HY_EMB_BANDS - 1))
    ang = ((2.0 * math.pi / seq_len) * pos) * bands
    trig = jnp.cos(ang + jnp.where(lane > HY_EMB_BANDS, 0.5 * math.pi, 0.0))
    emb = jnp.where(lane == 0, t, jnp.where(lane <= 2 * HY_EMB_BANDS, trig, 0.0))
    fr = fr_ref[...]
    h = jnp.sin(fr * (jnp.dot(emb, w1_ref[...], preferred_element_type=F32, precision=HIGHEST) + b1_ref[...]))
    h = jnp.sin(fr * (jnp.dot(h, w2_ref[...], preferred_element_type=F32, precision=HIGHEST) + b2_ref[...]))
    h = jnp.sin(fr * (jnp.dot(h, w3_ref[...], preferred_element_type=F32, precision=HIGHEST) + b3_ref[...]))
    ch = lax.broadcasted_iota(jnp.int32, (1, wdt), 1).astype(F32)
    lo = math.log(HY_DECAY_TARGET) / HY_SLOW_DECAY
    hi = math.log(HY_DECAY_TARGET) / HY_FAST_DECAY
    deltas = jnp.abs(lo + ch * ((hi - lo) / (wdt - 1)))
    window = jnp.exp(-t * deltas)
    hb = h.astype(BF16)
    hf = jnp.dot(hb, w4f_ref[...].astype(BF16), preferred_element_type=F32) * window
    hg = jnp.dot(hb, w4b_ref[...].astype(BF16), preferred_element_type=F32) * window
    hg = jnp.where(pos == 0.0, 0.0, hg)
    hf_ref[...] = hf
    hg_ref[...] = hg

    @pl.when(i == 0)
    def _():
        ss_ref[...] = jnp.zeros_like(ss_ref)

    ss_ref[...] += jnp.sum(hf * hf + hg * hg, axis=0, keepdims=True)


def _pad2(a, rows, cols):
    return jnp.zeros((rows, cols), a.dtype).at[:a.shape[0], :a.shape[1]].set(a)


def _hyena_filter(seq_len, p, tr=256):
    wdt = p["hy_w4"].shape[1] // 2
    tr = min(tr, seq_len)
    hid = LANES
    w1 = _pad2(p["hy_w1"], LANES, hid)
    w2 = _pad2(p["hy_w2"], hid, hid)
    w3 = _pad2(p["hy_w3"], hid, hid)
    b1, b2, b3, fr = (_pad2(p[k].reshape(1, -1), 1, hid) for k in ("hy_b1", "hy_b2", "hy_b3", "hy_freq"))
    w4f = _pad2(p["hy_w4"][:, :wdt], hid, wdt)
    w4b = _pad2(p["hy_w4"][:, wdt:], hid, wdt)
    full = lambda a: pl.BlockSpec(a.shape, lambda i: (0, 0))
    out = pl.BlockSpec((tr, wdt), lambda i: (i, 0))
    args = (w1, b1, w2, b2, w3, b3, fr, w4f, w4b)
    vm = 4 * _nbytes((hid, wdt), F32) + 8 * _nbytes((tr, wdt), F32)
    return pl.pallas_call(
        functools.partial(_filter_body, seq_len=seq_len),
        grid=(seq_len // tr,),
        in_specs=[full(a) for a in args],
        out_specs=[out, out, pl.BlockSpec((1, wdt), lambda i: (0, 0))],
        out_shape=[jax.ShapeDtypeStruct((seq_len, wdt), F32)] * 2 + [jax.ShapeDtypeStruct((1, wdt), F32)],
        compiler_params=_cparams(("arbitrary",), vm),
        name="hyena_filter",
    )(*args)


def _fft_dims(seq_len):
    n = 2 * seq_len
    p = 1 << ((n.bit_length() - 1) // 2)
    return p, n // p


@functools.lru_cache(maxsize=None)
def _fft_consts(seq_len):
    pp, mm = _fft_dims(seq_len)
    n = pp * mm
    ph = pp // 2
    a = np.arange(ph)
    b = np.arange(mm)
    d = np.arange(pp)
    eye = np.eye(SUBLANES)
    ang = 2 * np.pi * np.outer(d, a) / pp
    fk = np.kron(np.concatenate([np.cos(ang), -np.sin(ang)], axis=0), eye)
    bt = b.reshape(mm // SUBLANES, 1, SUBLANES)
    angt = (2 * np.pi * d[None, :, None] * bt / n).reshape(mm // SUBLANES, pp * SUBLANES, 1)
    angc = -2 * np.pi * np.outer(b, b) / mm
    cr, ci = np.cos(angc), np.sin(angc)
    w2 = np.block([[cr, -ci], [ci, cr]])
    v2 = np.block([[cr, ci], [-ci, cr]])
    gk = np.kron(np.concatenate([np.cos(ang.T), -np.sin(ang.T)], axis=1) / n, eye)
    as_bf16 = lambda x: np.asarray(x, dtype=np.float32).astype(BF16)
    return dict(fk=as_bf16(fk), w2=as_bf16(w2), v2=as_bf16(v2), gk=as_bf16(gk),
                twc=np.cos(angt).astype(np.float32), tws=np.sin(angt).astype(np.float32), pp=pp, mm=mm)


def _stage1_to_scratch(a_ref, fk_ref, twc_ref, tws_ref, group_rows, t):
    pp = a_ref.shape[0] // 2
    groups = twc_ref.shape[0]
    for g in range(groups):
        zt = group_rows(g)
        r = jnp.dot(fk_ref[...], zt, preferred_element_type=F32)
        rr, ri = r[:pp * SUBLANES], r[pp * SUBLANES:]
        c, s = twc_ref[g], tws_ref[g]
        a_ref[:pp, t * groups + g] = (rr * c + ri * s).reshape(pp, SUBLANES, zt.shape[-1])
        a_ref[pp:, t * groups + g] = (ri * c - rr * s).reshape(pp, SUBLANES, zt.shape[-1])


def _group_of_8(ref, g):
    blk = ref[:, g * SUBLANES:(g + 1) * SUBLANES, :]
    return blk.reshape(blk.shape[0] * SUBLANES, blk.shape[-1])


def _scratch_rows(a_ref, d):
    pp = a_ref.shape[0] // 2
    mm = a_ref.shape[1] * SUBLANES
    lanes = a_ref.shape[-1]
    return jnp.concatenate([a_ref[d].reshape(mm, lanes), a_ref[pp + d].reshape(mm, lanes)], axis=0).astype(BF16)


def _bf16_bits(x):
    u = lax.bitcast_convert_type(x, jnp.uint32)
    u = u + jnp.uint32(0x7FFF) + ((u >> 16) & jnp.uint32(1))
    return u & jnp.uint32(0xFFFF0000)


def _pack_ri(re, im):
    return _bf16_bits(re) | (_bf16_bits(im) >> 16)


def _unpack_ri(word):
    re = lax.bitcast_convert_type(word & jnp.uint32(0xFFFF0000), F32)
    im = lax.bitcast_convert_type(word << 16, F32)
    return re, im


def _fft_spec_body(hf_ref, hg_ref, fk_ref, twc_ref, tws_ref, w2_ref, o_ref, a_ref, *, nb):
    t = pl.program_id(1)
    cb = hf_ref.shape[-1]
    mm = a_ref.shape[1] * SUBLANES

    @pl.when(t < nb)
    def _():
        rows = lambda g: jnp.concatenate([_group_of_8(hf_ref, g), _group_of_8(hg_ref, g)], axis=-1).astype(BF16)
        _stage1_to_scratch(a_ref, fk_ref, twc_ref, tws_ref, rows, t)

    @pl.when(t >= nb)
    def _():
        nd_step = o_ref.shape[0]
        for dl in range(nd_step):
            h = jnp.dot(w2_ref[...], _scratch_rows(a_ref, (t - nb) * nd_step + dl), preferred_element_type=F32)
            o_ref[dl, :mm, :] = (h[:mm, :cb] + h[:mm, cb:]).astype(o_ref.dtype)
            o_ref[dl, mm:, :] = (h[mm:, :cb] - h[mm:, cb:]).astype(o_ref.dtype)


def _phase_specs(k, cb, sb, nb):
    pp = k["pp"]
    ph = pp // 2
    step = lambda t: jnp.minimum(t, nb - 1)
    return (pl.BlockSpec((ph, sb, cb), lambda j, t: (0, step(t), j)),
            pl.BlockSpec(k["fk"].shape, lambda j, t: (0, 0)),
            pl.BlockSpec((sb // SUBLANES, pp * SUBLANES, 1), lambda j, t: (step(t), 0, 0)))


FFT_B_PER_STEP = 16
FFT_D_PER_STEP = 16


def _fft_filter_spectrum(hf, hg, k, cb=LANES):
    l, c = hf.shape
    pp, mm = k["pp"], k["mm"]
    ph = pp // 2
    cb = min(cb, c)
    sb, ds = min(2 * FFT_B_PER_STEP, mm), min(2 * FFT_D_PER_STEP, pp)
    nb, nd = mm // sb, pp // ds
    taps, fk_spec, tw_spec = _phase_specs(k, cb, sb, nb)
    vm = (_nbytes((2 * pp, mm // SUBLANES, SUBLANES, 2 * cb), F32) + 4 * _nbytes((ph, sb, cb), F32)
          + 2 * _nbytes(k["fk"].shape, BF16) + 2 * _nbytes((ds, 2 * mm, cb), BF16)
          + 5 * _nbytes((2 * pp * SUBLANES, 2 * cb), F32))
    return pl.pallas_call(
        functools.partial(_fft_spec_body, nb=nb),
        grid=(c // cb, nb + nd),
        in_specs=[taps, taps, fk_spec, tw_spec, tw_spec, pl.BlockSpec(k["w2"].shape, lambda j, t: (0, 0))],
        out_specs=pl.BlockSpec((ds, 2 * mm, cb), lambda j, t: (jnp.maximum(t - nb, 0), 0, j)),
        out_shape=jax.ShapeDtypeStruct((pp, 2 * mm, c), BF16),
        scratch_shapes=[pltpu.VMEM((2 * pp, mm // SUBLANES, SUBLANES, 2 * cb), F32)],
        compiler_params=_cparams(("parallel", "arbitrary"), vm),
        name="fft_filter_spectrum",
    )(hf.reshape(ph, mm, c), hg.reshape(ph, mm, c), k["fk"], k["twc"], k["tws"], k["w2"])


def _fft_mid_body(z_ref, fk_ref, twc_ref, tws_ref, h_ref, w2_ref, v2_ref, o_ref, a_ref, *, nb):
    t = pl.program_id(1)
    mm = a_ref.shape[1] * SUBLANES

    @pl.when(t < nb)
    def _():
        _stage1_to_scratch(a_ref, fk_ref, twc_ref, tws_ref, lambda g: _group_of_8(z_ref, g).astype(BF16), t)

    @pl.when(t >= nb)
    def _():
        nd_step = o_ref.shape[0]
        for dl in range(nd_step):
            x = jnp.dot(w2_ref[...], _scratch_rows(a_ref, (t - nb) * nd_step + dl), preferred_element_type=F32)
            xr, xi = x[:mm], x[mm:]
            hr, hi = h_ref[dl, :mm, :].astype(F32), h_ref[dl, mm:, :].astype(F32)
            y = jnp.concatenate([xr * hr - xi * hi, xr * hi + xi * hr], axis=0).astype(BF16)
            bd = jnp.dot(v2_ref[...], y, preferred_element_type=F32)
            o_ref[dl] = _pack_ri(bd[:mm], bd[mm:])


def _fft_mid(z, h, k, cb=256):
    l, c = z.shape
    pp, mm = k["pp"], k["mm"]
    cb = min(cb, c)
    sb, ds = min(FFT_B_PER_STEP, mm), min(FFT_D_PER_STEP, pp)
    nb, nd = mm // sb, pp // ds
    sig, fk_spec, tw_spec = _phase_specs(k, cb, sb, nb)
    const = pl.BlockSpec(k["w2"].shape, lambda j, t: (0, 0))
    dstep = lambda t: jnp.maximum(t - nb, 0)
    vm = (_nbytes((2 * pp, mm // SUBLANES, SUBLANES, cb), F32) + 2 * _nbytes((pp // 2, sb, cb), F32)
          + 2 * _nbytes(k["fk"].shape, BF16) + 2 * _nbytes((ds, 2 * mm, cb), BF16)
          + 2 * _nbytes((ds, mm, cb), F32) + 5 * _nbytes((2 * pp * SUBLANES, cb), F32))
    return pl.pallas_call(
        functools.partial(_fft_mid_body, nb=nb),
        grid=(c // cb, nb + nd),
        in_specs=[sig, fk_spec, tw_spec, tw_spec,
                  pl.BlockSpec((ds, 2 * mm, cb), lambda j, t: (dstep(t), 0, j)), const, const],
        out_specs=pl.BlockSpec((ds, mm, cb), lambda j, t: (dstep(t), 0, j)),
        out_shape=jax.ShapeDtypeStruct((pp, mm, c), jnp.uint32),
        scratch_shapes=[pltpu.VMEM((2 * pp, mm // SUBLANES, SUBLANES, cb), F32)],
        compiler_params=_cparams(("parallel", "arbitrary"), vm),
        name="fft_mid",
    )(z.reshape(pp // 2, mm, c), k["fk"], k["twc"], k["tws"], h, k["w2"], k["v2"])


def _fft_last_body(b_ref, gk_ref, twc_ref, tws_ref, x0_ref, z_ref, ss_ref, bias_ref, o_ref):
    scale = lax.rsqrt(ss_ref[...] + EPS)
    ys = []
    for g in range(twc_ref.shape[0]):
        br, bi = _unpack_ri(_group_of_8(b_ref, g))
        c, s = twc_ref[g], tws_ref[g]
        rhs = jnp.concatenate([br * c - bi * s, br * s + bi * c], axis=0).astype(BF16)
        y = jnp.dot(gk_ref[...], rhs, preferred_element_type=F32)
        ys.append(y.reshape(o_ref.shape[0], SUBLANES, o_ref.shape[2]))
    y = jnp.concatenate(ys, axis=1)
    o_ref[...] = (x0_ref[...].astype(F32) * (y * scale + z_ref[...] * bias_ref[...])).astype(o_ref.dtype)


def _fft_last(bmat, k, x0, z, ss, bias, cb=256):
    pp, mm, c = bmat.shape
    ph = pp // 2
    cb = min(cb, c)
    sb = min(FFT_B_PER_STEP, mm)
    view = pl.BlockSpec((ph, sb, cb), lambda j, b: (0, b, j))
    row = pl.BlockSpec((1, cb), lambda j, b: (0, j))
    tw_spec = pl.BlockSpec((sb // SUBLANES, pp * SUBLANES, 1), lambda j, b: (b, 0, 0))
    vm = 2 * (_nbytes((pp, sb, cb), F32) + 3 * _nbytes((ph, sb, cb), F32)
              + _nbytes(k["gk"].shape, BF16)) + 6 * _nbytes((2 * pp * SUBLANES, cb), F32)
    out = pl.pallas_call(
        _fft_last_body,
        grid=(c // cb, mm // sb),
        in_specs=[pl.BlockSpec((pp, sb, cb), lambda j, b: (0, b, j)),
                  pl.BlockSpec(k["gk"].shape, lambda j, b: (0, 0)),
                  tw_spec, tw_spec, view, view, row, row],
        out_specs=view,
        out_shape=jax.ShapeDtypeStruct((ph, mm, c), BF16),
        compiler_params=_cparams(("parallel", "parallel"), vm),
        name="fft_last",
    )(bmat, k["gk"], k["twc"], k["tws"], x0.reshape(ph, mm, c), z.reshape(ph, mm, c), ss, bias.reshape(1, c))
    return out.reshape(ph * mm, c)


def _hyena_longconv(x0, z, hf, hg, ss, bias):
    k = _fft_consts(z.shape[0])
    h = _fft_filter_spectrum(hf, hg, k)
    return _fft_last(_fft_mid(z, h, k), k, x0, z, ss, bias)


def _router_body(h_ref, w_ref, o_ref):
    logits = jnp.dot(h_ref[...], w_ref[...], preferred_element_type=F32, precision=HIGHEST)
    lane = lax.broadcasted_iota(jnp.int32, logits.shape, 1)
    lg = jnp.where(lane < N_EXPERTS, logits, -jnp.inf)
    m1 = jnp.max(lg, axis=-1, keepdims=True)
    i1 = jnp.min(jnp.where(lg == m1, lane, LANES), axis=-1, keepdims=True)
    l2 = jnp.where(lane == i1, -jnp.inf, lg)
    m2 = jnp.max(l2, axis=-1, keepdims=True)
    i2 = jnp.min(jnp.where(l2 == m2, lane, LANES), axis=-1, keepdims=True)
    e = jnp.exp(m2 - m1)
    w1 = 1.0 / (1.0 + e)
    w2 = e * w1
    o_ref[...] = jnp.where(lane == 0, i1.astype(F32), jnp.where(lane == 1, i2.astype(F32),
                           jnp.where(lane == 2, w1, jnp.where(lane == 3, w2, 0.0))))


def _router(h, router_w, tm=256):
    m, d = h.shape
    tm = min(tm, m)
    w = _pad2(router_w, d, LANES)
    vm = 2 * (_nbytes((tm, d), F32) + _nbytes((d, LANES), F32)) + 8 * _nbytes((tm, LANES), F32)
    return pl.pallas_call(
        _router_body,
        grid=(m // tm,),
        in_specs=[pl.BlockSpec((tm, d), lambda i: (i, 0)), pl.BlockSpec((d, LANES), lambda i: (0, 0))],
        out_specs=pl.BlockSpec((tm, LANES), lambda i: (i, 0)),
        out_shape=jax.ShapeDtypeStruct((m, LANES), F32),
        compiler_params=_cparams(("parallel",), vm),
        name="moe_router",
    )(h, w)


def _routing_tables(route, tile):
    t = route.shape[0]
    e_flat = route[:, :TOP_K].astype(jnp.int32).reshape(-1)
    w_flat = route[:, TOP_K:2 * TOP_K].reshape(-1)
    onehot = (e_flat[:, None] == jnp.arange(N_EXPERTS, dtype=jnp.int32)[None, :]).astype(jnp.int32)
    csum = jnp.cumsum(onehot, axis=0)
    rank = jnp.take_along_axis(csum, e_flat[:, None], axis=1)[:, 0] - 1
    counts = csum[-1]
    padded = ((counts + tile - 1) // tile) * tile
    ends = jnp.cumsum(padded)
    pos = (ends - padded)[e_flat] + rank
    n_tiles = (t * TOP_K) // tile + N_EXPERTS
    rows = n_tiles * tile
    token = jnp.arange(t * TOP_K, dtype=jnp.int32) // TOP_K
    w_bits = lax.bitcast_convert_type(w_flat, jnp.int32)
    table = jnp.zeros((rows, 2), jnp.int32).at[pos].set(jnp.stack([token, w_bits], axis=1))
    row_token = table[:, 0]
    row_w = lax.bitcast_convert_type(table[:, 1], F32)
    start = jnp.arange(n_tiles, dtype=jnp.int32) * tile
    valid = start < ends[-1]
    expert = jnp.minimum(jnp.sum((start[:, None] >= ends[None, :]).astype(jnp.int32), axis=1), N_EXPERTS - 1)
    last_valid = jnp.max(jnp.where(valid, expert, 0))
    expert = jnp.where(valid, expert, last_valid)
    changed = jnp.concatenate([jnp.ones((1,), bool), expert[1:] != expert[:-1]])
    is_start = changed & valid
    idx = jnp.arange(n_tiles, dtype=jnp.int32)
    later_start = jnp.where(is_start[None, :] & (idx[None, :] > idx[:, None]), idx[None, :], n_tiles)
    nxt = jnp.min(later_start, axis=1)
    wrap = (nxt == n_tiles).astype(jnp.int32)
    next_expert = expert[jnp.where(nxt == n_tiles, 0, nxt)]
    tiles = (expert, is_start.astype(jnp.int32), valid.astype(jnp.int32), next_expert, wrap)
    return row_token, row_w, pos.astype(jnp.int32), tiles


def _row_copy(src_hbm, dst, sem, src_row, dst_row):
    return pltpu.make_async_copy(src_hbm.at[pl.ds(src_row, 1)], dst.at[pl.ds(dst_row, 1)], sem)


DMA_LOOP_UNROLL = 8


def _gather_rows_body(tok_ref, h_hbm, o_ref, buf, sems):
    i = pl.program_id(0)
    gt = buf.shape[1]
    slot = i % 2

    def issue(step, s):
        def body(r, carry):
            _row_copy(h_hbm, buf.at[s], sems.at[s], tok_ref[step * gt + r], r).start()
            return carry
        lax.fori_loop(0, gt, body, 0, unroll=DMA_LOOP_UNROLL)

    def drain(s):
        def body(r, carry):
            _row_copy(h_hbm, buf.at[s], sems.at[s], 0, r).wait()
            return carry
        lax.fori_loop(0, gt, body, 0, unroll=DMA_LOOP_UNROLL)

    @pl.when(i == 0)
    def _():
        issue(0, 0)

    @pl.when(i + 1 < pl.num_programs(0))
    def _():
        issue(i + 1, 1 - slot)

    drain(slot)
    o_ref[...] = buf[slot].astype(o_ref.dtype)


def _gather_rows(h, row_token, gt=256):
    rows = row_token.shape[0]
    d = h.shape[1]
    vm = 4 * _nbytes((gt, d), F32)
    return pl.pallas_call(
        _gather_rows_body,
        grid_spec=pltpu.PrefetchScalarGridSpec(
            num_scalar_prefetch=1,
            grid=(rows // gt,),
            in_specs=[pl.BlockSpec(memory_space=pl.ANY)],
            out_specs=pl.BlockSpec((gt, d), lambda i, tok: (i, 0)),
            scratch_shapes=[pltpu.VMEM((2, gt, d), F32), pltpu.SemaphoreType.DMA((2,))],
        ),
        out_shape=jax.ShapeDtypeStruct((rows, d), BF16),
        compiler_params=_cparams(("arbitrary",), vm),
        name="moe_gather_rows",
    )(row_token, h)


def _group_weights(tiles, w_hbms, w_f32s, w_bf16s, sems):
    te_ref, ts_ref, _, ne_ref, wrap_ref = tiles
    j, i = pl.program_id(0), pl.program_id(1)
    tn = w_f32s[0].shape[1]

    def copies(expert, col_block):
        col = pl.multiple_of(col_block * tn, tn)
        return [pltpu.make_async_copy(w.at[expert, :, pl.ds(col, tn)], buf, sems.at[n])
                for n, (w, buf) in enumerate(zip(w_hbms, w_f32s))]

    @pl.when(ts_ref[i] == 1)
    def _():
        @pl.when((j == 0) & (i == 0))
        def _():
            for cp in copies(te_ref[i], j):
                cp.start()

        for cp in copies(te_ref[i], j):
            cp.wait()
        for src, dst in zip(w_f32s, w_bf16s):
            dst[...] = src[...].astype(BF16)
        nj = j + wrap_ref[i]

        @pl.when(nj < pl.num_programs(0))
        def _():
            for cp in copies(ne_ref[i], nj):
                cp.start()


def _moe_up_body(te, ts, tv, ne, wrap, x_ref, wg_hbm, wu_hbm, o_ref, wg32, wu32, wg_bf, wu_bf, sems):
    i = pl.program_id(1)
    _group_weights((te, ts, tv, ne, wrap), (wg_hbm, wu_hbm), (wg32, wu32), (wg_bf, wu_bf), sems)

    @pl.when(tv[i] == 1)
    def _():
        x = x_ref[...]
        g = jnp.dot(x, wg_bf[...], preferred_element_type=F32)
        u = jnp.dot(x, wu_bf[...], preferred_element_type=F32)
        o_ref[...] = (g * jax.nn.sigmoid(g) * u).astype(o_ref.dtype)

    @pl.when(tv[i] == 0)
    def _():
        o_ref[...] = jnp.zeros_like(o_ref)


def _moe_up(xs, wg, wu, tiles, tile, tn=1024):
    rows, d = xs.shape
    f = wg.shape[2]
    vm = 2 * _nbytes((d, tn), F32) + 2 * _nbytes((d, tn), BF16) + 2 * _nbytes((tile, d), BF16) + 6 * _nbytes((tile, tn), F32)
    return pl.pallas_call(
        _moe_up_body,
        grid_spec=pltpu.PrefetchScalarGridSpec(
            num_scalar_prefetch=len(tiles),
            grid=(f // tn, rows // tile),
            in_specs=[pl.BlockSpec((tile, d), lambda j, i, *_: (i, 0)),
                      pl.BlockSpec(memory_space=pl.ANY), pl.BlockSpec(memory_space=pl.ANY)],
            out_specs=pl.BlockSpec((tile, tn), lambda j, i, *_: (i, j)),
            scratch_shapes=[pltpu.VMEM((d, tn), F32), pltpu.VMEM((d, tn), F32),
                            pltpu.VMEM((d, tn), BF16), pltpu.VMEM((d, tn), BF16),
                            pltpu.SemaphoreType.DMA((2,))],
        ),
        out_shape=jax.ShapeDtypeStruct((rows, f), BF16),
        compiler_params=_cparams(("arbitrary", "arbitrary"), vm),
        name="moe_up",
    )(*tiles, xs, wg, wu)


def _moe_down_body(te, ts, tv, ne, wrap, a_ref, wd_hbm, rw_ref, o_ref, wd32, wd_bf, sems):
    i = pl.program_id(1)
    _group_weights((te, ts, tv, ne, wrap), (wd_hbm,), (wd32,), (wd_bf,), sems)

    @pl.when(tv[i] == 1)
    def _():
        o_ref[...] = jnp.dot(a_ref[...], wd_bf[...], preferred_element_type=F32) * rw_ref[...]

    @pl.when(tv[i] == 0)
    def _():
        o_ref[...] = jnp.zeros_like(o_ref)


def _moe_down(act, wd, row_w, tiles, tile, tn=512):
    rows, f = act.shape
    d = wd.shape[2]
    vm = _nbytes((f, tn), F32) + _nbytes((f, tn), BF16) + 2 * _nbytes((tile, f), BF16) + 4 * _nbytes((tile, tn), F32)
    return pl.pallas_call(
        _moe_down_body,
        grid_spec=pltpu.PrefetchScalarGridSpec(
            num_scalar_prefetch=len(tiles),
            grid=(d // tn, rows // tile),
            in_specs=[pl.BlockSpec((tile, f), lambda j, i, *_: (i, 0)),
                      pl.BlockSpec(memory_space=pl.ANY),
                      pl.BlockSpec((tile, 1), lambda j, i, *_: (i, 0))],
            out_specs=pl.BlockSpec((tile, tn), lambda j, i, *_: (i, j)),
            scratch_shapes=[pltpu.VMEM((f, tn), F32), pltpu.VMEM((f, tn), BF16), pltpu.SemaphoreType.DMA((1,))],
        ),
        out_shape=jax.ShapeDtypeStruct((rows, d), F32),
        compiler_params=_cparams(("arbitrary", "arbitrary"), vm),
        name="moe_down",
    )(*tiles, act, wd, row_w.reshape(rows, 1))


def _moe_combine_body(pos_ref, ys_hbm, x_ref, gate_ref, g_ref, o_ref, buf, sems, *, final_norm):
    i = pl.program_id(0)
    gt = x_ref.shape[0]
    slot = i % 2

    def issue(step, s):
        def body(r, carry):
            for k in range(TOP_K):
                _row_copy(ys_hbm, buf.at[s, k], sems.at[s], pos_ref[TOP_K * (step * gt + r) + k], r).start()
            return carry
        lax.fori_loop(0, gt, body, 0, unroll=DMA_LOOP_UNROLL)

    def drain(s):
        def body(r, carry):
            for k in range(TOP_K):
                _row_copy(ys_hbm, buf.at[s, k], sems.at[s], 0, r).wait()
            return carry
        lax.fori_loop(0, gt, body, 0, unroll=DMA_LOOP_UNROLL)

    @pl.when(i == 0)
    def _():
        issue(0, 0)

    @pl.when(i + 1 < pl.num_programs(0))
    def _():
        issue(i + 1, 1 - slot)

    drain(slot)
    y = buf[slot, 0]
    for k in range(1, TOP_K):
        y = y + buf[slot, k]
    x = x_ref[...] + gate_ref[...] * y
    if final_norm:
        x = x * lax.rsqrt(jnp.mean(x * x, axis=-1, keepdims=True) + EPS) * g_ref[...]
    o_ref[...] = x


def _moe_combine(ys, pos, x, gate, norm_g, final_norm, gt=128):
    t, d = x.shape
    gt = min(gt, t)
    row = pl.BlockSpec((1, d), lambda i, p: (0, 0))
    tilespec = pl.BlockSpec((gt, d), lambda i, p: (i, 0))
    vm = (2 * TOP_K + 6) * _nbytes((gt, d), F32)
    return pl.pallas_call(
        functools.partial(_moe_combine_body, final_norm=final_norm),
        grid_spec=pltpu.PrefetchScalarGridSpec(
            num_scalar_prefetch=1,
            grid=(t // gt,),
            in_specs=[pl.BlockSpec(memory_space=pl.ANY), tilespec, row, row],
            out_specs=tilespec,
            scratch_shapes=[pltpu.VMEM((2, TOP_K, gt, d), F32), pltpu.SemaphoreType.DMA((2,))],
        ),
        out_shape=jax.ShapeDtypeStruct((t, d), F32),
        compiler_params=_cparams(("arbitrary",), vm),
        name="moe_combine",
    )(pos, ys, x, gate.reshape(1, d), norm_g.reshape(1, d))


MOE_ROW_TILE = 256


def _moe_ffn(h32, x, gate, p, norm_g, final_norm):
    route = _router(h32, p["router_w"])
    row_token, row_w, pos, tiles = _routing_tables(route, MOE_ROW_TILE)
    xs = _gather_rows(h32, row_token)
    act = _moe_up(xs, p["exp_w_gate"], p["exp_w_up"], tiles, MOE_ROW_TILE)
    ys = _moe_down(act, p["exp_w_down"], row_w, tiles, MOE_ROW_TILE)
    return _moe_combine(ys, pos, x, gate, norm_g, final_norm)


def _silu(v):
    return v * jax.nn.sigmoid(v)


def _epi_plain(accs, ex):
    return [accs[0]]


def _in_weights(w_in_all, layer, d):
    w_in = w_in_all[layer]
    qk, vw, r = d // 2, d, GLA_GATE_RANK
    col_v = qk
    col_a = col_v + vw
    col_q = col_a + 2 * r
    col_g = col_q + qk
    col_hy = col_g + vw
    col_gate = col_hy + 3 * d
    rest = w_in[:, col_q:col_gate + 2 * d].astype(BF16)
    return dict(kv=w_in_all, layer=layer, k=(0, qk), v=(col_v, vw),
                a=_pad2(w_in[:, col_a:col_q].astype(BF16), d, LANES),
                rest=rest, q=(0, qk), g=(col_g - col_q, vw), hy=(col_hy - col_q, 3 * d),
                gate=(col_gate - col_q, 2 * d))


def _gate_matrices(p, d):
    r = GLA_GATE_RANK
    qk = d // 2
    awf = jnp.zeros((LANES, qk), F32).at[:r].set(p["gla_aw_f"])
    awb = jnp.zeros((LANES, qk), F32).at[r:2 * r].set(p["gla_aw_b"])
    return awf, awb, p["gla_ab_f"].reshape(1, qk), p["gla_ab_b"].reshape(1, qk)


GLA_CHUNK = 256


def _gla_states_only(h, wi, p, d):
    k, = _proj(h, wi, "kv", "k", _epi_plain, BF16)
    v, = _proj(h, wi, "kv", "v", _epi_plain, BF16)
    a, = _mm(h, [wi["a"]], _epi_plain, (F32,), name="proj_a")
    ef, eb = _gla_decay(a, *_gate_matrices(p, d), chunk=min(GLA_CHUNK, h.shape[0]))
    dk, dv = d // 2 // GLA_HEADS, d // GLA_HEADS
    zero = jnp.zeros((GLA_HEADS, dv, dk), F32)
    _, s_f = _gla_scan(k, k, v, ef, zero, fwd=True, chunk=GLA_CHUNK)
    _, s_b = _gla_scan(k, k, v, eb, zero, fwd=False, chunk=GLA_CHUNK)
    return s_f, s_b


def _token_mixer(x, h, mod_gate, s0_f, s0_b, wi, p, d):
    l = h.shape[0]
    dk = d // 2 // GLA_HEADS
    k, = _proj(h, wi, "kv", "k", _epi_plain, BF16)
    v, = _proj(h, wi, "kv", "v", _epi_plain, BF16)
    a, = _mm(h, [wi["a"]], _epi_plain, (F32,), name="proj_a")
    q, = _proj(h, wi, "rest", "q", lambda accs, ex: [accs[0] * (dk ** -0.5)], BF16)
    sg, = _proj(h, wi, "rest", "g", lambda accs, ex: [_silu(accs[0])], BF16)
    hy, = _proj(h, wi, "rest", "hy", _epi_plain, BF16)
    gates, = _proj(h, wi, "rest", "gate", lambda accs, ex: [jax.nn.sigmoid(accs[0])], BF16)

    ef, eb = _gla_decay(a, *_gate_matrices(p, d), chunk=min(GLA_CHUNK, l))
    o_f, s_f = _gla_scan(q, k, v, ef, s0_f, fwd=True, chunk=GLA_CHUNK)
    o_gla, s_b = _gla_scan(q, k, v, eb, s0_b, fwd=False, chunk=GLA_CHUNK, combine=(o_f, sg, p["gla_norm_g"]))

    x0, z = _hyena_shortconv(hy, p["hy_conv_w"], p["hy_conv_b"])
    hf, hg, ss = _hyena_filter(l, p)
    o_hy = _hyena_longconv(x0, z, hf, hg, ss, p["hy_bias"])

    layer = wi["layer"]
    t1, = _mm(o_hy, [wi["w_up_hy"]], lambda accs, ex: [ex[0].astype(F32) * accs[0]], (F32,),
              extras=[(gates, "tile", 0)], w_layer=layer, name="up_hy")
    merged, = _mm(o_gla, [wi["w_up_gla"]], lambda accs, ex: [ex[1] + ex[0].astype(F32) * accs[0]], (BF16,),
                  extras=[(gates, "tile", d), (t1, "tile", 0)], w_layer=layer, name="up_gla_merge")
    x_new, = _mm(merged, [wi["w_out"]], lambda accs, ex: [ex[0] + ex[1] * accs[0]], (F32,),
                 extras=[(x, "tile", 0), (mod_gate.reshape(1, d), "row", 0)], w_layer=layer, name="out_proj")
    return x_new, s_f, s_b


def _proj(h, wi, matrix, group, epi, out_dtype):
    col, width = wi[group]
    return _mm(h, [wi[matrix]], epi, (out_dtype,), n=width, w_col=col,
               w_layer=wi["layer"] if matrix == "kv" else None, name="proj_" + group)


def _dense_ffn(x, h, mod_gate, wg, wu, wd):
    d, f = wg.shape
    fp = -(-f // 512) * 512
    wg_b = _pad2(wg.astype(BF16), d, fp)
    wu_b = _pad2(wu.astype(BF16), d, fp)
    wd_b = _pad2(wd.astype(BF16), fp, d)
    act, = _mm(h, [wg_b, wu_b], lambda accs, ex: [_silu(accs[0]) * accs[1]], (BF16,), tn=512, name="ffn_up")
    x_new, = _mm(act, [wd_b], lambda accs, ex: [ex[0] + ex[1] * accs[0]], (F32,), tm=512,
                 extras=[(x, "tile", 0), (mod_gate.reshape(1, d), "row", 0)], name="ffn_down")
    return x_new


def kernel(x, c, ctx, c_ctx, ada_w, ada_b, norm_mix_g, norm_ffn_g, w_in, hy_conv_w, hy_conv_b, hy_w1, hy_b1, hy_w2, hy_b2, hy_w3, hy_b3, hy_w4, hy_freq, hy_bias, gla_aw_f, gla_ab_f, gla_aw_b, gla_ab_b, gla_norm_g, w_up_hy, w_up_gla, w_out, ffn_w_gate, ffn_w_up, ffn_w_down, router_w, exp_w_gate, exp_w_up, exp_w_down, final_norm_g):
    assert x.shape[0] == 1 and c.shape[0] == 1, "batch size 1 only"
    depth, d = norm_mix_g.shape
    x_lat, x_ctx = x[0], ctx[0]
    mods = _ada_modulation(jnp.concatenate([c, c_ctx.reshape(1, d)], axis=0), ada_w, ada_b)
    dk, dv = d // 2 // GLA_HEADS, d // GLA_HEADS
    per_layer = dict(hy_conv_w=hy_conv_w, hy_conv_b=hy_conv_b, hy_w1=hy_w1, hy_b1=hy_b1, hy_w2=hy_w2, hy_b2=hy_b2,
                     hy_w3=hy_w3, hy_b3=hy_b3, hy_w4=hy_w4, hy_freq=hy_freq, hy_bias=hy_bias,
                     gla_aw_f=gla_aw_f, gla_ab_f=gla_ab_f, gla_aw_b=gla_aw_b, gla_ab_b=gla_ab_b,
                     gla_norm_g=gla_norm_g)
    for l in range(depth):
        last = l == depth - 1
        p = {name: arr[l] for name, arr in per_layer.items()}
        wi = dict(_in_weights(w_in, l, d), w_up_hy=w_up_hy, w_up_gla=w_up_gla, w_out=w_out)
        lat = [mods[l, 0, i * d:(i + 1) * d] for i in range(N_ADA)]
        cxm = [mods[l, 1, i * d:(i + 1) * d] for i in range(N_ADA)]

        h_ctx, = _modnorm(x_ctx, norm_mix_g[l], cxm[0], cxm[1], (BF16,))
        if last:
            s_f, s_b = _gla_states_only(h_ctx, wi, p, d)
        else:
            zero = jnp.zeros((GLA_HEADS, dv, dk), F32)
            x_ctx, s_f, s_b = _token_mixer(x_ctx, h_ctx, cxm[2], zero, zero, wi, p, d)
        h_lat, = _modnorm(x_lat, norm_mix_g[l], lat[0], lat[1], (BF16,))
        x_lat, _, _ = _token_mixer(x_lat, h_lat, lat[2], s_f, s_b, wi, p, d)

        i = l // 2
        if l % 2 == 0:
            h2, = _modnorm(x_lat, norm_ffn_g[l], lat[3], lat[4], (BF16,))
            x_lat = _dense_ffn(x_lat, h2, lat[5], ffn_w_gate[i], ffn_w_up[i], ffn_w_down[i])
            if not last:
                h2c, = _modnorm(x_ctx, norm_ffn_g[l], cxm[3], cxm[4], (BF16,))
                x_ctx = _dense_ffn(x_ctx, h2c, cxm[5], ffn_w_gate[i], ffn_w_up[i], ffn_w_down[i])
        else:
            pm = dict(router_w=router_w[i], exp_w_gate=exp_w_gate[i], exp_w_up=exp_w_up[i], exp_w_down=exp_w_down[i])
            h2, = _modnorm(x_lat, norm_ffn_g[l], lat[3], lat[4], (F32,))
            x_lat = _moe_ffn(h2, x_lat, lat[5], pm, final_norm_g, final_norm=last)
            if not last:
                h2c, = _modnorm(x_ctx, norm_ffn_g[l], cxm[3], cxm[4], (F32,))
                x_ctx = _moe_ffn(h2c, x_ctx, cxm[5], pm, final_norm_g, final_norm=False)
    if depth % 2 == 1:
        x_lat, = _modnorm(x_lat, final_norm_g, jnp.zeros((d,), F32), jnp.zeros((d,), F32), (F32,))
    return x_lat[None]
```

```python
import functools
import math

import numpy as np
import jax
import jax.numpy as jnp
from jax import lax
from jax.experimental import pallas as pl
from jax.experimental.pallas import tpu as pltpu

F32 = jnp.float32
BF16 = jnp.bfloat16
HIGHEST = lax.Precision.HIGHEST

EPS = 1e-6
N_ADA = 6
LANES = 128
SUBLANES = 8
VMEM_LIMIT_CAP = 60 * 1024 * 1024

GLA_HEADS = 4
GLA_GATE_RANK = 16
GLA_GATE_TEMP = 16.0
HY_SHORT = 3
HY_EMB_BANDS = 16
HY_FILTER_HIDDEN = 64
HY_DECAY_TARGET = 1e-2
HY_FAST_DECAY = 0.3
HY_SLOW_DECAY = 1.5
N_EXPERTS = 8
TOP_K = 2


def _cparams(sem, vmem_bytes):
    limit = int(min(max(vmem_bytes * 5 // 4 + (2 << 20), 16 << 20), VMEM_LIMIT_CAP))
    return pltpu.CompilerParams(dimension_semantics=sem, vmem_limit_bytes=limit)


def _nbytes(shape, dtype):
    return int(np.prod(shape)) * jnp.dtype(dtype).itemsize


def _ada_body(c_ref, w_ref, b_ref, o_ref):
    tn = o_ref.shape[-1]
    d = c_ref.shape[1]
    rows = []
    for r in range(2):
        s = c_ref[r]
        s = s * jax.nn.sigmoid(s)
        chunks = []
        for n0 in range(0, tn, LANES):
            p = w_ref[:, n0:n0 + LANES] * s
            acc = p.reshape(d // SUBLANES, SUBLANES, LANES).sum(axis=0)
            chunks.append(acc.sum(axis=0, keepdims=True))
        rows.append(jnp.concatenate(chunks, axis=1))
    o_ref[...] = jnp.concatenate(rows, axis=0) + b_ref[...]


def _ada_modulation(cond2, ada_w, ada_b):
    depth, d, n = ada_w.shape
    tn = 1536 if n % 1536 == 0 else LANES
    cb = jnp.broadcast_to(cond2[:, :, None], (2, d, LANES))
    vm = 2 * _nbytes((d, tn), F32) + 2 * _nbytes((2, d, LANES), F32)
    return pl.pallas_call(
        _ada_body,
        grid=(depth, n // tn),
        in_specs=[
            pl.BlockSpec((2, d, LANES), lambda l, j: (0, 0, 0)),
            pl.BlockSpec((None, d, tn), lambda l, j: (l, 0, j)),
            pl.BlockSpec((None, 1, tn), lambda l, j: (l, 0, j)),
        ],
        out_specs=pl.BlockSpec((None, 2, tn), lambda l, j: (l, 0, j)),
        out_shape=jax.ShapeDtypeStruct((depth, 2, n), F32),
        compiler_params=_cparams(("parallel", "parallel"), vm),
        name="ada_modulation",
    )(cb, ada_w, ada_b.reshape(depth, 1, n))


def _modnorm_body(x_ref, g_ref, sh_ref, sc_ref, *o_refs):
    x = x_ref[...]
    ms = jnp.mean(x * x, axis=-1, keepdims=True)
    y = x * lax.rsqrt(ms + EPS) * g_ref[...]
    y = y * (1.0 + sc_ref[...]) + sh_ref[...]
    for o in o_refs:
        o[...] = y.astype(o.dtype)


def _modnorm(x, g, shift, scale, out_dtypes, tm=256):
    m, d = x.shape
    tm = min(tm, m)
    row = pl.BlockSpec((1, d), lambda i: (0, 0))
    tile = pl.BlockSpec((tm, d), lambda i: (i, 0))
    vm = 2 * _nbytes((tm, d), F32) * (1 + len(out_dtypes))
    outs = pl.pallas_call(
        _modnorm_body,
        grid=(m // tm,),
        in_specs=[tile, row, row, row],
        out_specs=[tile] * len(out_dtypes),
        out_shape=[jax.ShapeDtypeStruct((m, d), dt) for dt in out_dtypes],
        compiler_params=_cparams(("parallel",), vm),
        name="modnorm",
    )(x, g.reshape(1, d), shift.reshape(1, d), scale.reshape(1, d))
    return outs


def _mm_body(*refs, n_w, n_e, n_o, epi, cast_w):
    x_ref = refs[0]
    w_refs = refs[1:1 + n_w]
    e_refs = refs[1 + n_w:1 + n_w + n_e]
    o_refs = refs[1 + n_w + n_e:1 + n_w + n_e + n_o]
    w_bf16 = refs[1 + n_w + n_e + n_o:]
    x = x_ref[...].astype(BF16)
    if cast_w:
        @pl.when(pl.program_id(1) == 0)
        def _():
            for src, dst in zip(w_refs, w_bf16):
                dst[...] = src[...].astype(BF16)
        w_refs = w_bf16
    accs = [jnp.dot(x, w[...], preferred_element_type=F32) for w in w_refs]
    outs = epi(accs, [e[...] for e in e_refs])
    for o, v in zip(o_refs, outs):
        o[...] = v.astype(o.dtype)


def _mm(x, ws, epi, out_dtypes, extras=(), w_layer=None, tm=1024, tn=1024, name="mm"):
    m, k = x.shape
    n = ws[0].shape[-1]
    tm = min(tm, m)
    tn = min(tn, n)
    assert m % tm == 0 and n % tn == 0, (m, tm, n, tn)
    cast_w = ws[0].dtype == F32
    in_specs = [pl.BlockSpec((tm, k), lambda j, i: (i, 0))]
    if w_layer is None:
        in_specs += [pl.BlockSpec((k, tn), lambda j, i: (0, j))] * len(ws)
    else:
        in_specs += [pl.BlockSpec((None, k, tn), lambda j, i: (w_layer, 0, j))] * len(ws)
    vm = 2 * _nbytes((tm, k), x.dtype) + 2 * len(ws) * _nbytes((k, tn), ws[0].dtype)
    for arr, kind, col in extras:
        assert col % tn == 0
        if kind == "tile":
            in_specs.append(pl.BlockSpec((tm, tn), lambda j, i, c=col // tn: (i, j + c)))
            vm += 2 * _nbytes((tm, tn), arr.dtype)
        else:
            in_specs.append(pl.BlockSpec((1, tn), lambda j, i, c=col // tn: (0, j + c)))
    vm += sum(2 * _nbytes((tm, tn), dt) for dt in out_dtypes) + (1 + len(ws)) * _nbytes((tm, tn), F32)
    scratch = [pltpu.VMEM((k, tn), BF16)] * len(ws) if cast_w else []
    vm += len(scratch) * _nbytes((k, tn), BF16)
    return pl.pallas_call(
        functools.partial(_mm_body, n_w=len(ws), n_e=len(extras), n_o=len(out_dtypes), epi=epi, cast_w=cast_w),
        grid=(n // tn, m // tm),
        in_specs=in_specs,
        out_specs=[pl.BlockSpec((tm, tn), lambda j, i: (i, j))] * len(out_dtypes),
        out_shape=[jax.ShapeDtypeStruct((m, n), dt) for dt in out_dtypes],
        scratch_shapes=scratch,
        compiler_params=_cparams(("parallel", "arbitrary" if cast_w else "parallel"), vm),
        name=name,
    )(x, *ws, *[a for a, _, _ in extras])


def _log_sigmoid(z):
    return jnp.minimum(z, 0.0) - jnp.log1p(jnp.exp(-jnp.abs(z)))


def _split_bf16(x, terms):
    parts = []
    for _ in range(terms):
        p = x.astype(BF16)
        parts.append(p)
        x = x - p.astype(F32)
    return parts


def _dot_nt(a, b):
    return lax.dot_general(a, b, (((1,), (1,)), ((), ())), preferred_element_type=F32)


def _dot_tn(a, b):
    return lax.dot_general(a, b, (((0,), (0,)), ((), ())), preferred_element_type=F32)


def _bcast_rows(e, group, row):
    c, w = e.shape
    e3 = e.reshape(c // group, group, w)
    return jnp.broadcast_to(e3[:, row:row + 1, :], e3.shape).reshape(c, w)


def _gla_masks(c, base, fwd):
    i = lax.broadcasted_iota(jnp.int32, (c, c), 0)
    j = lax.broadcasted_iota(jnp.int32, (c, c), 1)
    sh = int(math.log2(base))
    order = (j <= i) if fwd else (j >= i)
    masks = [((i >> sh) == (j >> sh)) & order]
    s = base
    while 2 * s <= c:
        sh += 1
        masks.append((i >> sh) == (j >> sh))
        s *= 2
    return masks


def _gla_chunk_head(q, k, v, e, st, masks, *, fwd, base):
    c, dk = q.shape
    row = lax.broadcasted_iota(jnp.int32, (c, 1), 0)
    d0 = e - _bcast_rows(e, base, base // 2 - 1 if fwd else base // 2)
    q0 = (q * jnp.exp(d0)).astype(BF16)
    k0 = (k * jnp.exp(-d0)).astype(BF16)
    att = jnp.where(masks[0], _dot_nt(q0, k0), 0.0)
    s, lvl = base, 1
    while 2 * s <= c:
        d = e - _bcast_rows(e, 2 * s, s - 1 if fwd else s)
        later = ((row >> int(math.log2(s))) & 1) == (1 if fwd else 0)
        x = jnp.exp(jnp.where(later, d, -d))
        ql = jnp.where(later, q * x, 0.0).astype(BF16)
        kl = jnp.where(later, 0.0, k * x).astype(BF16)
        att = att + jnp.where(masks[lvl], _dot_nt(ql, kl), 0.0)
        s *= 2
        lvl += 1
    e_edge = e[c - 1:c] if fwd else e[0:1]
    qs = (q * jnp.exp(e)).astype(BF16)
    ks = (k * jnp.exp(e_edge - e)).astype(BF16)
    o = jnp.dot(att.astype(BF16), v, preferred_element_type=F32) + _dot_nt(qs, st.astype(BF16))
    st_new = st * jnp.exp(e_edge) + _dot_tn(v, ks)
    return o, st_new


def _gla_scan_body(*refs, fwd, base, combine):
    if combine:
        q_ref, k_ref, v_ref, a_ref, aw_ref, ab_ref, s0_ref, of_ref, sg_ref, gn_ref, o_ref, sfin_ref, st_ref = refs
    else:
        q_ref, k_ref, v_ref, a_ref, aw_ref, ab_ref, s0_ref, o_ref, sfin_ref, st_ref = refs
    step = pl.program_id(0)
    c = q_ref.shape[0]
    dk = q_ref.shape[1] // GLA_HEADS
    dv = v_ref.shape[1] // GLA_HEADS

    @pl.when(step == 0)
    def _():
        st_ref[...] = s0_ref[...]

    masks = _gla_masks(c, base, fwd)
    r = lax.broadcasted_iota(jnp.int32, (c, c), 0)
    s = lax.broadcasted_iota(jnp.int32, (c, c), 1)
    tri = ((s <= r) if fwd else (s >= r)).astype(BF16)
    a_parts = _split_bf16(a_ref[...], 2)
    for h in range(GLA_HEADS):
        ks = slice(h * dk, (h + 1) * dk)
        vs = slice(h * dv, (h + 1) * dv)
        w_hi, w_lo = _split_bf16(aw_ref[:, ks], 2)
        z = jnp.dot(a_parts[0], w_hi, preferred_element_type=F32)
        z = z + jnp.dot(a_parts[1], w_hi, preferred_element_type=F32)
        z = z + jnp.dot(a_parts[0], w_lo, preferred_element_type=F32) + ab_ref[:, ks]
        g = _log_sigmoid(z) * (1.0 / GLA_GATE_TEMP)
        e = sum(jnp.dot(tri, part, preferred_element_type=F32) for part in _split_bf16(g, 3))
        o, st_new = _gla_chunk_head(
            q_ref[:, ks].astype(F32), k_ref[:, ks].astype(F32), v_ref[:, vs], e, st_ref[h],
            masks, fwd=fwd, base=base)
        st_ref[h] = st_new
        if combine:
            t = o + of_ref[:, vs]
            t = t * lax.rsqrt(jnp.mean(t * t, axis=-1, keepdims=True) + EPS) * gn_ref[...]
            o_ref[:, vs] = (t * sg_ref[:, vs].astype(F32)).astype(o_ref.dtype)
        else:
            o_ref[:, vs] = o.astype(o_ref.dtype)

    @pl.when(step == pl.num_programs(0) - 1)
    def _():
        sfin_ref[...] = st_ref[...]


def _gla_scan(q, k, v, a, aw, ab, s0, *, fwd, chunk, base=32, combine=None):
    l, hdk = q.shape
    hdv = v.shape[1]
    dk, dv = hdk // GLA_HEADS, hdv // GLA_HEADS
    chunk = min(chunk, l)
    n = l // chunk
    idx = (lambda i: (i, 0)) if fwd else (lambda i: (n - 1 - i, 0))
    st_spec = pl.BlockSpec((GLA_HEADS, dv, dk), lambda i: (0, 0, 0))
    in_specs = [pl.BlockSpec((chunk, hdk), idx), pl.BlockSpec((chunk, hdk), idx),
                pl.BlockSpec((chunk, hdv), idx), pl.BlockSpec((chunk, a.shape[1]), idx),
                pl.BlockSpec(aw.shape, lambda i: (0, 0)), pl.BlockSpec(ab.shape, lambda i: (0, 0)), st_spec]
    args = [q, k, v, a, aw, ab, s0]
    vm = 2 * (2 * _nbytes((chunk, hdk), BF16) + _nbytes((chunk, hdv), BF16) + _nbytes(aw.shape, F32))
    vm += 3 * _nbytes((GLA_HEADS, dv, dk), F32) * 2 + 2 * _nbytes((chunk, hdv), F32)
    if combine is not None:
        o_other, gate, norm_g = combine
        in_specs += [pl.BlockSpec((chunk, hdv), idx), pl.BlockSpec((chunk, hdv), idx),
                     pl.BlockSpec((1, dv), lambda i: (0, 0))]
        args += [o_other, gate, norm_g.reshape(1, dv)]
        vm += 2 * (_nbytes((chunk, hdv), F32) + _nbytes((chunk, hdv), BF16))
    vm += 24 * _nbytes((chunk, max(dk, chunk)), F32)
    return pl.pallas_call(
        functools.partial(_gla_scan_body, fwd=fwd, base=base, combine=combine is not None),
        grid=(n,),
        in_specs=in_specs,
        out_specs=[pl.BlockSpec((chunk, hdv), idx), st_spec],
        out_shape=[jax.ShapeDtypeStruct((l, hdv), BF16 if combine is not None else F32),
                   jax.ShapeDtypeStruct((GLA_HEADS, dv, dk), F32)],
        scratch_shapes=[pltpu.VMEM((GLA_HEADS, dv, dk), F32)],
        compiler_params=_cparams(("arbitrary",), vm),
        name="gla_scan_fwd" if fwd else "gla_scan_bwd",
    )(*args)


def _shortconv_body(u0, u1, u2, p0, p1, p2, n0, n1, n2, w0, w1, w2, b0, b1, b2, x0_ref, z_ref):
    i = pl.program_id(0)
    last = pl.num_programs(0) - 1
    tm = u0.shape[0]
    row = lax.broadcasted_iota(jnp.int32, (tm, 1), 0)

    def conv(u_ref, p_ref, n_ref, w_ref, b_ref):
        u = u_ref[...].astype(F32)
        halo = p_ref.shape[0]
        prev_row = jnp.where(i == 0, 0.0, p_ref[...].astype(F32)[halo - 1:halo, :])
        next_row = jnp.where(i == last, 0.0, n_ref[...].astype(F32)[0:1, :])
        before = jnp.where(row == 0, prev_row, pltpu.roll(u, 1, axis=0))
        after = jnp.where(row == tm - 1, next_row, pltpu.roll(u, tm - 1, axis=0))
        return b_ref[...] + before * w_ref[0:1, :] + u * w_ref[1:2, :] + after * w_ref[2:3, :]

    x0_ref[...] = conv(u0, p0, n0, w0, b0).astype(x0_ref.dtype)
    z_ref[...] = conv(u1, p1, n1, w1, b1) * conv(u2, p2, n2, w2, b2)


def _hyena_shortconv(hy, conv_w, conv_b, tm=512, cb=512):
    l, w3 = hy.shape
    w = w3 // 3
    tm = min(tm, l)
    cb = min(cb, w)
    nb = w // cb
    halo = SUBLANES * 4 // jnp.dtype(hy.dtype).itemsize
    hb = tm // halo
    n_halo = l // halo
    cur = [pl.BlockSpec((tm, cb), lambda i, j, g=g: (i, g * nb + j)) for g in range(3)]
    prv = [pl.BlockSpec((halo, cb), lambda i, j, g=g: (jnp.maximum(i * hb - 1, 0), g * nb + j)) for g in range(3)]
    nxt = [pl.BlockSpec((halo, cb), lambda i, j, g=g: (jnp.minimum((i + 1) * hb, n_halo - 1), g * nb + j))
           for g in range(3)]
    wsp = [pl.BlockSpec((HY_SHORT, cb), lambda i, j, g=g: (0, g * nb + j)) for g in range(3)]
    bsp = [pl.BlockSpec((1, cb), lambda i, j, g=g: (0, g * nb + j)) for g in range(3)]
    out = pl.BlockSpec((tm, cb), lambda i, j: (i, j))
    vm = 2 * 5 * _nbytes((tm, cb), F32) + 8 * _nbytes((tm, cb), F32)
    return pl.pallas_call(
        _shortconv_body,
        grid=(l // tm, nb),
        in_specs=cur + prv + nxt + wsp + bsp,
        out_specs=[out, out],
        out_shape=[jax.ShapeDtypeStruct((l, w), BF16), jax.ShapeDtypeStruct((l, w), F32)],
        compiler_params=_cparams(("parallel", "parallel"), vm),
        name="hyena_shortconv",
    )(hy, hy, hy, hy, hy, hy, hy, hy, hy, conv_w, conv_w, conv_w,
      conv_b.reshape(1, w3), conv_b.reshape(1, w3), conv_b.reshape(1, w3))


def _filter_body(w1_ref, b1_ref, w2_ref, b2_ref, w3_ref, b3_ref, fr_ref, w4f_ref, w4b_ref,
                 hf_ref, hg_ref, ss_ref, *, seq_len):
    i = pl.program_id(0)
    tr = hf_ref.shape[0]
    wdt = hf_ref.shape[1]
    pos = (lax.broadcasted_iota(jnp.int32, (tr, 1), 0) + i * tr).astype(F32)
    t = pos / float(max(seq_len - 1, 1))
    lane = lax.broadcasted_iota(jnp.int32, (1, LANES), 1)
    band = ((lane - 1) & (HY_EMB_BANDS - 1)).astype(F32)
    bands = 1e-4 + band * ((HY_EMB_BANDS - 1 - 1e-4) / (HY_EMB_BANDS - 1))
    ang = ((2.0 * math.pi / seq_len) * pos) * bands
    trig = jnp.cos(ang + jnp.where(lane > HY_EMB_BANDS, 0.5 * math.pi, 0.0))
    emb = jnp.where(lane == 0, t, jnp.where(lane <= 2 * HY_EMB_BANDS, trig, 0.0))
    fr = fr_ref[...]
    h = jnp.sin(fr * (jnp.dot(emb, w1_ref[...], preferred_element_type=F32, precision=HIGHEST) + b1_ref[...]))
    h = jnp.sin(fr * (jnp.dot(h, w2_ref[...], preferred_element_type=F32, precision=HIGHEST) + b2_ref[...]))
    h = jnp.sin(fr * (jnp.dot(h, w3_ref[...], preferred_element_type=F32, precision=HIGHEST) + b3_ref[...]))
    ch = lax.broadcasted_iota(jnp.int32, (1, wdt), 1).astype(F32)
    lo = math.log(HY_DECAY_TARGET) / HY_SLOW_DECAY
    hi = math.log(HY_DECAY_TARGET) / HY_FAST_DECAY
    deltas = jnp.abs(lo + ch * ((hi - lo) / (wdt - 1)))
    window = jnp.exp(-t * deltas)
    hb = h.astype(BF16)
    hf = jnp.dot(hb, w4f_ref[...].astype(BF16), preferred_element_type=F32) * window
    hg = jnp.dot(hb, w4b_ref[...].astype(BF16), preferred_element_type=F32) * window
    hg = jnp.where(pos == 0.0, 0.0, hg)
    hf_ref[...] = hf
    hg_ref[...] = hg

    @pl.when(i == 0)
    def _():
        ss_ref[...] = jnp.zeros_like(ss_ref)

    ss_ref[...] += jnp.sum(hf * hf + hg * hg, axis=0, keepdims=True)


def _pad2(a, rows, cols):
    return jnp.zeros((rows, cols), a.dtype).at[:a.shape[0], :a.shape[1]].set(a)


def _hyena_filter(seq_len, p, tr=256):
    wdt = p["hy_w4"].shape[1] // 2
    tr = min(tr, seq_len)
    hid = LANES
    w1 = _pad2(p["hy_w1"], LANES, hid)
    w2 = _pad2(p["hy_w2"], hid, hid)
    w3 = _pad2(p["hy_w3"], hid, hid)
    b1, b2, b3, fr = (_pad2(p[k].reshape(1, -1), 1, hid) for k in ("hy_b1", "hy_b2", "hy_b3", "hy_freq"))
    w4f = _pad2(p["hy_w4"][:, :wdt], hid, wdt)
    w4b = _pad2(p["hy_w4"][:, wdt:], hid, wdt)
    full = lambda a: pl.BlockSpec(a.shape, lambda i: (0, 0))
    out = pl.BlockSpec((tr, wdt), lambda i: (i, 0))
    args = (w1, b1, w2, b2, w3, b3, fr, w4f, w4b)
    vm = 4 * _nbytes((hid, wdt), F32) + 8 * _nbytes((tr, wdt), F32)
    return pl.pallas_call(
        functools.partial(_filter_body, seq_len=seq_len),
        grid=(seq_len // tr,),
        in_specs=[full(a) for a in args],
        out_specs=[out, out, pl.BlockSpec((1, wdt), lambda i: (0, 0))],
        out_shape=[jax.ShapeDtypeStruct((seq_len, wdt), F32)] * 2 + [jax.ShapeDtypeStruct((1, wdt), F32)],
        compiler_params=_cparams(("arbitrary",), vm),
        name="hyena_filter",
    )(*args)


def _fft_dims(seq_len):
    n = 2 * seq_len
    p = 1 << ((n.bit_length() - 1) // 2)
    return p, n // p


@functools.lru_cache(maxsize=None)
def _fft_consts(seq_len):
    pp, mm = _fft_dims(seq_len)
    n = pp * mm
    ph = pp // 2
    a = np.arange(ph)
    b = np.arange(mm)
    d = np.arange(pp)
    eye = np.eye(SUBLANES)
    ang = 2 * np.pi * np.outer(d, a) / pp
    fk = np.kron(np.concatenate([np.cos(ang), -np.sin(ang)], axis=0), eye)
    bt = b.reshape(mm // SUBLANES, 1, SUBLANES)
    angt = (2 * np.pi * d[None, :, None] * bt / n).reshape(mm // SUBLANES, pp * SUBLANES, 1)
    angc = -2 * np.pi * np.outer(b, b) / mm
    cr, ci = np.cos(angc), np.sin(angc)
    w2 = np.block([[cr, -ci], [ci, cr]])
    v2 = np.block([[cr, ci], [-ci, cr]])
    gk = np.kron(np.concatenate([np.cos(ang.T), -np.sin(ang.T)], axis=1) / n, eye)
    as_bf16 = lambda x: np.asarray(x, dtype=np.float32).astype(BF16)
    return dict(fk=as_bf16(fk), w2=as_bf16(w2), v2=as_bf16(v2), gk=as_bf16(gk),
                twc=np.cos(angt).astype(np.float32), tws=np.sin(angt).astype(np.float32), pp=pp, mm=mm)


def _stage1_to_scratch(a_ref, fk_ref, twc_ref, tws_ref, group_rows, t):
    pp = a_ref.shape[0] // 2
    groups = twc_ref.shape[0]
    for g in range(groups):
        zt = group_rows(g)
        r = jnp.dot(fk_ref[...], zt, preferred_element_type=F32)
        rr, ri = r[:pp * SUBLANES], r[pp * SUBLANES:]
        c, s = twc_ref[g], tws_ref[g]
        a_ref[:pp, t * groups + g] = (rr * c + ri * s).reshape(pp, SUBLANES, zt.shape[-1])
        a_ref[pp:, t * groups + g] = (ri * c - rr * s).reshape(pp, SUBLANES, zt.shape[-1])


def _group_of_8(ref, g):
    blk = ref[:, g * SUBLANES:(g + 1) * SUBLANES, :]
    return blk.reshape(blk.shape[0] * SUBLANES, blk.shape[-1])


def _scratch_rows(a_ref, d):
    pp = a_ref.shape[0] // 2
    mm = a_ref.shape[1] * SUBLANES
    lanes = a_ref.shape[-1]
    return jnp.concatenate([a_ref[d].reshape(mm, lanes), a_ref[pp + d].reshape(mm, lanes)], axis=0).astype(BF16)


def _bf16_bits(x):
    u = lax.bitcast_convert_type(x, jnp.uint32)
    u = u + jnp.uint32(0x7FFF) + ((u >> 16) & jnp.uint32(1))
    return u & jnp.uint32(0xFFFF0000)


def _pack_ri(re, im):
    return _bf16_bits(re) | (_bf16_bits(im) >> 16)


def _unpack_ri(word):
    re = lax.bitcast_convert_type(word & jnp.uint32(0xFFFF0000), F32)
    im = lax.bitcast_convert_type(word << 16, F32)
    return re, im


def _fft_spec_body(hf_ref, hg_ref, fk_ref, twc_ref, tws_ref, w2_ref, o_ref, a_ref, *, nb):
    t = pl.program_id(1)
    cb = hf_ref.shape[-1]
    mm = a_ref.shape[1] * SUBLANES

    @pl.when(t < nb)
    def _():
        rows = lambda g: jnp.concatenate([_group_of_8(hf_ref, g), _group_of_8(hg_ref, g)], axis=-1).astype(BF16)
        _stage1_to_scratch(a_ref, fk_ref, twc_ref, tws_ref, rows, t)

    @pl.when(t >= nb)
    def _():
        nd_step = o_ref.shape[0]
        for dl in range(nd_step):
            h = jnp.dot(w2_ref[...], _scratch_rows(a_ref, (t - nb) * nd_step + dl), preferred_element_type=F32)
            o_ref[dl, :mm, :] = (h[:mm, :cb] + h[:mm, cb:]).astype(o_ref.dtype)
            o_ref[dl, mm:, :] = (h[mm:, :cb] - h[mm:, cb:]).astype(o_ref.dtype)


def _phase_specs(k, cb, sb, nb):
    pp = k["pp"]
    ph = pp // 2
    step = lambda t: jnp.minimum(t, nb - 1)
    return (pl.BlockSpec((ph, sb, cb), lambda j, t: (0, step(t), j)),
            pl.BlockSpec(k["fk"].shape, lambda j, t: (0, 0)),
            pl.BlockSpec((sb // SUBLANES, pp * SUBLANES, 1), lambda j, t: (step(t), 0, 0)))


FFT_B_PER_STEP = 16
FFT_D_PER_STEP = 16


def _fft_filter_spectrum(hf, hg, k, cb=LANES):
    l, c = hf.shape
    pp, mm = k["pp"], k["mm"]
    ph = pp // 2
    cb = min(cb, c)
    sb, ds = min(2 * FFT_B_PER_STEP, mm), min(2 * FFT_D_PER_STEP, pp)
    nb, nd = mm // sb, pp // ds
    taps, fk_spec, tw_spec = _phase_specs(k, cb, sb, nb)
    vm = (_nbytes((2 * pp, mm // SUBLANES, SUBLANES, 2 * cb), F32) + 4 * _nbytes((ph, sb, cb), F32)
          + 2 * _nbytes(k["fk"].shape, BF16) + 2 * _nbytes((ds, 2 * mm, cb), BF16)
          + 5 * _nbytes((2 * pp * SUBLANES, 2 * cb), F32))
    return pl.pallas_call(
        functools.partial(_fft_spec_body, nb=nb),
        grid=(c // cb, nb + nd),
        in_specs=[taps, taps, fk_spec, tw_spec, tw_spec, pl.BlockSpec(k["w2"].shape, lambda j, t: (0, 0))],
        out_specs=pl.BlockSpec((ds, 2 * mm, cb), lambda j, t: (jnp.maximum(t - nb, 0), 0, j)),
        out_shape=jax.ShapeDtypeStruct((pp, 2 * mm, c), BF16),
        scratch_shapes=[pltpu.VMEM((2 * pp, mm // SUBLANES, SUBLANES, 2 * cb), F32)],
        compiler_params=_cparams(("parallel", "arbitrary"), vm),
        name="fft_filter_spectrum",
    )(hf.reshape(ph, mm, c), hg.reshape(ph, mm, c), k["fk"], k["twc"], k["tws"], k["w2"])


def _fft_mid_body(z_ref, fk_ref, twc_ref, tws_ref, h_ref, w2_ref, v2_ref, o_ref, a_ref, *, nb):
    t = pl.program_id(1)
    mm = a_ref.shape[1] * SUBLANES

    @pl.when(t < nb)
    def _():
        _stage1_to_scratch(a_ref, fk_ref, twc_ref, tws_ref, lambda g: _group_of_8(z_ref, g).astype(BF16), t)

    @pl.when(t >= nb)
    def _():
        nd_step = o_ref.shape[0]
        for dl in range(nd_step):
            x = jnp.dot(w2_ref[...], _scratch_rows(a_ref, (t - nb) * nd_step + dl), preferred_element_type=F32)
            xr, xi = x[:mm], x[mm:]
            hr, hi = h_ref[dl, :mm, :].astype(F32), h_ref[dl, mm:, :].astype(F32)
            y = jnp.concatenate([xr * hr - xi * hi, xr * hi + xi * hr], axis=0).astype(BF16)
            bd = jnp.dot(v2_ref[...], y, preferred_element_type=F32)
            o_ref[dl] = _pack_ri(bd[:mm], bd[mm:])


def _fft_mid(z, h, k, cb=256):
    l, c = z.shape
    pp, mm = k["pp"], k["mm"]
    cb = min(cb, c)
    sb, ds = min(FFT_B_PER_STEP, mm), min(FFT_D_PER_STEP, pp)
    nb, nd = mm // sb, pp // ds
    sig, fk_spec, tw_spec = _phase_specs(k, cb, sb, nb)
    const = pl.BlockSpec(k["w2"].shape, lambda j, t: (0, 0))
    dstep = lambda t: jnp.maximum(t - nb, 0)
    vm = (_nbytes((2 * pp, mm // SUBLANES, SUBLANES, cb), F32) + 2 * _nbytes((pp // 2, sb, cb), F32)
          + 2 * _nbytes(k["fk"].shape, BF16) + 2 * _nbytes((ds, 2 * mm, cb), BF16)
          + 2 * _nbytes((ds, mm, cb), F32) + 5 * _nbytes((2 * pp * SUBLANES, cb), F32))
    return pl.pallas_call(
        functools.partial(_fft_mid_body, nb=nb),
        grid=(c // cb, nb + nd),
        in_specs=[sig, fk_spec, tw_spec, tw_spec,
                  pl.BlockSpec((ds, 2 * mm, cb), lambda j, t: (dstep(t), 0, j)), const, const],
        out_specs=pl.BlockSpec((ds, mm, cb), lambda j, t: (dstep(t), 0, j)),
        out_shape=jax.ShapeDtypeStruct((pp, mm, c), jnp.uint32),
        scratch_shapes=[pltpu.VMEM((2 * pp, mm // SUBLANES, SUBLANES, cb), F32)],
        compiler_params=_cparams(("parallel", "arbitrary"), vm),
        name="fft_mid",
    )(z.reshape(pp // 2, mm, c), k["fk"], k["twc"], k["tws"], h, k["w2"], k["v2"])


def _fft_last_body(b_ref, gk_ref, twc_ref, tws_ref, x0_ref, z_ref, ss_ref, bias_ref, o_ref):
    scale = lax.rsqrt(ss_ref[...] + EPS)
    ys = []
    for g in range(twc_ref.shape[0]):
        br, bi = _unpack_ri(_group_of_8(b_ref, g))
        c, s = twc_ref[g], tws_ref[g]
        rhs = jnp.concatenate([br * c - bi * s, br * s + bi * c], axis=0).astype(BF16)
        y = jnp.dot(gk_ref[...], rhs, preferred_element_type=F32)
        ys.append(y.reshape(o_ref.shape[0], SUBLANES, o_ref.shape[2]))
    y = jnp.concatenate(ys, axis=1)
    o_ref[...] = (x0_ref[...].astype(F32) * (y * scale + z_ref[...] * bias_ref[...])).astype(o_ref.dtype)


def _fft_last(bmat, k, x0, z, ss, bias, cb=256):
    pp, mm, c = bmat.shape
    ph = pp // 2
    cb = min(cb, c)
    sb = min(FFT_B_PER_STEP, mm)
    view = pl.BlockSpec((ph, sb, cb), lambda j, b: (0, b, j))
    row = pl.BlockSpec((1, cb), lambda j, b: (0, j))
    tw_spec = pl.BlockSpec((sb // SUBLANES, pp * SUBLANES, 1), lambda j, b: (b, 0, 0))
    vm = 2 * (_nbytes((pp, sb, cb), F32) + 3 * _nbytes((ph, sb, cb), F32)
              + _nbytes(k["gk"].shape, BF16)) + 6 * _nbytes((2 * pp * SUBLANES, cb), F32)
    out = pl.pallas_call(
        _fft_last_body,
        grid=(c // cb, mm // sb),
        in_specs=[pl.BlockSpec((pp, sb, cb), lambda j, b: (0, b, j)),
                  pl.BlockSpec(k["gk"].shape, lambda j, b: (0, 0)),
                  tw_spec, tw_spec, view, view, row, row],
        out_specs=view,
        out_shape=jax.ShapeDtypeStruct((ph, mm, c), BF16),
        compiler_params=_cparams(("parallel", "parallel"), vm),
        name="fft_last",
    )(bmat, k["gk"], k["twc"], k["tws"], x0.reshape(ph, mm, c), z.reshape(ph, mm, c), ss, bias.reshape(1, c))
    return out.reshape(ph * mm, c)


def _hyena_longconv(x0, z, hf, hg, ss, bias):
    k = _fft_consts(z.shape[0])
    h = _fft_filter_spectrum(hf, hg, k)
    return _fft_last(_fft_mid(z, h, k), k, x0, z, ss, bias)


def _router_body(h_ref, w_ref, o_ref):
    logits = jnp.dot(h_ref[...], w_ref[...], preferred_element_type=F32, precision=HIGHEST)
    lane = lax.broadcasted_iota(jnp.int32, logits.shape, 1)
    lg = jnp.where(lane < N_EXPERTS, logits, -jnp.inf)
    m1 = jnp.max(lg, axis=-1, keepdims=True)
    i1 = jnp.min(jnp.where(lg == m1, lane, LANES), axis=-1, keepdims=True)
    l2 = jnp.where(lane == i1, -jnp.inf, lg)
    m2 = jnp.max(l2, axis=-1, keepdims=True)
    i2 = jnp.min(jnp.where(l2 == m2, lane, LANES), axis=-1, keepdims=True)
    e = jnp.exp(m2 - m1)
    w1 = 1.0 / (1.0 + e)
    w2 = e * w1
    o_ref[...] = jnp.where(lane == 0, i1.astype(F32), jnp.where(lane == 1, i2.astype(F32),
                           jnp.where(lane == 2, w1, jnp.where(lane == 3, w2, 0.0))))


def _router(h, router_w, tm=256):
    m, d = h.shape
    tm = min(tm, m)
    w = _pad2(router_w, d, LANES)
    vm = 2 * (_nbytes((tm, d), F32) + _nbytes((d, LANES), F32)) + 8 * _nbytes((tm, LANES), F32)
    return pl.pallas_call(
        _router_body,
        grid=(m // tm,),
        in_specs=[pl.BlockSpec((tm, d), lambda i: (i, 0)), pl.BlockSpec((d, LANES), lambda i: (0, 0))],
        out_specs=pl.BlockSpec((tm, LANES), lambda i: (i, 0)),
        out_shape=jax.ShapeDtypeStruct((m, LANES), F32),
        compiler_params=_cparams(("parallel",), vm),
        name="moe_router",
    )(h, w)


def _routing_tables(route, tile):
    t = route.shape[0]
    e_flat = route[:, :TOP_K].astype(jnp.int32).reshape(-1)
    w_flat = route[:, TOP_K:2 * TOP_K].reshape(-1)
    onehot = (e_flat[:, None] == jnp.arange(N_EXPERTS, dtype=jnp.int32)[None, :]).astype(jnp.int32)
    csum = jnp.cumsum(onehot, axis=0)
    rank = jnp.take_along_axis(csum, e_flat[:, None], axis=1)[:, 0] - 1
    counts = csum[-1]
    padded = ((counts + tile - 1) // tile) * tile
    ends = jnp.cumsum(padded)
    pos = (ends - padded)[e_flat] + rank
    n_tiles = (t * TOP_K) // tile + N_EXPERTS
    rows = n_tiles * tile
    token = jnp.arange(t * TOP_K, dtype=jnp.int32) // TOP_K
    w_bits = lax.bitcast_convert_type(w_flat, jnp.int32)
    table = jnp.zeros((rows, 2), jnp.int32).at[pos].set(jnp.stack([token, w_bits], axis=1))
    row_token = table[:, 0]
    row_w = lax.bitcast_convert_type(table[:, 1], F32)
    start = jnp.arange(n_tiles, dtype=jnp.int32) * tile
    valid = start < ends[-1]
    expert = jnp.minimum(jnp.sum((start[:, None] >= ends[None, :]).astype(jnp.int32), axis=1), N_EXPERTS - 1)
    last_valid = jnp.max(jnp.where(valid, expert, 0))
    expert = jnp.where(valid, expert, last_valid)
    changed = jnp.concatenate([jnp.ones((1,), bool), expert[1:] != expert[:-1]])
    is_start = changed & valid
    idx = jnp.arange(n_tiles, dtype=jnp.int32)
    later_start = jnp.where(is_start[None, :] & (idx[None, :] > idx[:, None]), idx[None, :], n_tiles)
    nxt = jnp.min(later_start, axis=1)
    wrap = (nxt == n_tiles).astype(jnp.int32)
    next_expert = expert[jnp.where(nxt == n_tiles, 0, nxt)]
    tiles = (expert, is_start.astype(jnp.int32), valid.astype(jnp.int32), next_expert, wrap)
    return row_token, row_w, pos.astype(jnp.int32), tiles


def _row_copy(src_hbm, dst, sem, src_row, dst_row):
    return pltpu.make_async_copy(src_hbm.at[pl.ds(src_row, 1)], dst.at[pl.ds(dst_row, 1)], sem)


DMA_LOOP_UNROLL = 8


def _gather_rows_body(tok_ref, h_hbm, o_ref, buf, sems):
    i = pl.program_id(0)
    gt = buf.shape[1]
    slot = i % 2

    def issue(step, s):
        def body(r, carry):
            _row_copy(h_hbm, buf.at[s], sems.at[s], tok_ref[step * gt + r], r).start()
            return carry
        lax.fori_loop(0, gt, body, 0, unroll=DMA_LOOP_UNROLL)

    def drain(s):
        def body(r, carry):
            _row_copy(h_hbm, buf.at[s], sems.at[s], 0, r).wait()
            return carry
        lax.fori_loop(0, gt, body, 0, unroll=DMA_LOOP_UNROLL)

    @pl.when(i == 0)
    def _():
        issue(0, 0)

    @pl.when(i + 1 < pl.num_programs(0))
    def _():
        issue(i + 1, 1 - slot)

    drain(slot)
    o_ref[...] = buf[slot].astype(o_ref.dtype)


def _gather_rows(h, row_token, gt=256):
    rows = row_token.shape[0]
    d = h.shape[1]
    vm = 4 * _nbytes((gt, d), F32)
    return pl.pallas_call(
        _gather_rows_body,
        grid_spec=pltpu.PrefetchScalarGridSpec(
            num_scalar_prefetch=1,
            grid=(rows // gt,),
            in_specs=[pl.BlockSpec(memory_space=pl.ANY)],
            out_specs=pl.BlockSpec((gt, d), lambda i, tok: (i, 0)),
            scratch_shapes=[pltpu.VMEM((2, gt, d), F32), pltpu.SemaphoreType.DMA((2,))],
        ),
        out_shape=jax.ShapeDtypeStruct((rows, d), BF16),
        compiler_params=_cparams(("arbitrary",), vm),
        name="moe_gather_rows",
    )(row_token, h)


def _group_weights(tiles, w_hbms, w_f32s, w_bf16s, sems):
    te_ref, ts_ref, _, ne_ref, wrap_ref = tiles
    j, i = pl.program_id(0), pl.program_id(1)
    tn = w_f32s[0].shape[1]

    def copies(expert, col_block):
        col = pl.multiple_of(col_block * tn, tn)
        return [pltpu.make_async_copy(w.at[expert, :, pl.ds(col, tn)], buf, sems.at[n])
                for n, (w, buf) in enumerate(zip(w_hbms, w_f32s))]

    @pl.when(ts_ref[i] == 1)
    def _():
        @pl.when((j == 0) & (i == 0))
        def _():
            for cp in copies(te_ref[i], j):
                cp.start()

        for cp in copies(te_ref[i], j):
            cp.wait()
        for src, dst in zip(w_f32s, w_bf16s):
            dst[...] = src[...].astype(BF16)
        nj = j + wrap_ref[i]

        @pl.when(nj < pl.num_programs(0))
        def _():
            for cp in copies(ne_ref[i], nj):
                cp.start()


def _moe_up_body(te, ts, tv, ne, wrap, x_ref, wg_hbm, wu_hbm, o_ref, wg32, wu32, wg_bf, wu_bf, sems):
    i = pl.program_id(1)
    _group_weights((te, ts, tv, ne, wrap), (wg_hbm, wu_hbm), (wg32, wu32), (wg_bf, wu_bf), sems)

    @pl.when(tv[i] == 1)
    def _():
        x = x_ref[...]
        g = jnp.dot(x, wg_bf[...], preferred_element_type=F32)
        u = jnp.dot(x, wu_bf[...], preferred_element_type=F32)
        o_ref[...] = (g * jax.nn.sigmoid(g) * u).astype(o_ref.dtype)

    @pl.when(tv[i] == 0)
    def _():
        o_ref[...] = jnp.zeros_like(o_ref)


def _moe_up(xs, wg, wu, tiles, tile, tn=1024):
    rows, d = xs.shape
    f = wg.shape[2]
    vm = 2 * _nbytes((d, tn), F32) + 2 * _nbytes((d, tn), BF16) + 2 * _nbytes((tile, d), BF16) + 6 * _nbytes((tile, tn), F32)
    return pl.pallas_call(
        _moe_up_body,
        grid_spec=pltpu.PrefetchScalarGridSpec(
            num_scalar_prefetch=len(tiles),
            grid=(f // tn, rows // tile),
            in_specs=[pl.BlockSpec((tile, d), lambda j, i, *_: (i, 0)),
                      pl.BlockSpec(memory_space=pl.ANY), pl.BlockSpec(memory_space=pl.ANY)],
            out_specs=pl.BlockSpec((tile, tn), lambda j, i, *_: (i, j)),
            scratch_shapes=[pltpu.VMEM((d, tn), F32), pltpu.VMEM((d, tn), F32),
                            pltpu.VMEM((d, tn), BF16), pltpu.VMEM((d, tn), BF16),
                            pltpu.SemaphoreType.DMA((2,))],
        ),
        out_shape=jax.ShapeDtypeStruct((rows, f), BF16),
        compiler_params=_cparams(("arbitrary", "arbitrary"), vm),
        name="moe_up",
    )(*tiles, xs, wg, wu)


def _moe_down_body(te, ts, tv, ne, wrap, a_ref, wd_hbm, rw_ref, o_ref, wd32, wd_bf, sems):
    i = pl.program_id(1)
    _group_weights((te, ts, tv, ne, wrap), (wd_hbm,), (wd32,), (wd_bf,), sems)

    @pl.when(tv[i] == 1)
    def _():
        o_ref[...] = jnp.dot(a_ref[...], wd_bf[...], preferred_element_type=F32) * rw_ref[...]

    @pl.when(tv[i] == 0)
    def _():
        o_ref[...] = jnp.zeros_like(o_ref)


def _moe_down(act, wd, row_w, tiles, tile, tn=512):
    rows, f = act.shape
    d = wd.shape[2]
    vm = _nbytes((f, tn), F32) + _nbytes((f, tn), BF16) + 2 * _nbytes((tile, f), BF16) + 4 * _nbytes((tile, tn), F32)
    return pl.pallas_call(
        _moe_down_body,
        grid_spec=pltpu.PrefetchScalarGridSpec(
            num_scalar_prefetch=len(tiles),
            grid=(d // tn, rows // tile),
            in_specs=[pl.BlockSpec((tile, f), lambda j, i, *_: (i, 0)),
                      pl.BlockSpec(memory_space=pl.ANY),
                      pl.BlockSpec((tile, 1), lambda j, i, *_: (i, 0))],
            out_specs=pl.BlockSpec((tile, tn), lambda j, i, *_: (i, j)),
            scratch_shapes=[pltpu.VMEM((f, tn), F32), pltpu.VMEM((f, tn), BF16), pltpu.SemaphoreType.DMA((1,))],
        ),
        out_shape=jax.ShapeDtypeStruct((rows, d), F32),
        compiler_params=_cparams(("arbitrary", "arbitrary"), vm),
        name="moe_down",
    )(*tiles, act, wd, row_w.reshape(rows, 1))


def _moe_combine_body(pos_ref, ys_hbm, x_ref, gate_ref, g_ref, o_ref, buf, sems, *, final_norm):
    i = pl.program_id(0)
    gt = x_ref.shape[0]
    slot = i % 2

    def issue(step, s):
        def body(r, carry):
            for k in range(TOP_K):
                _row_copy(ys_hbm, buf.at[s, k], sems.at[s], pos_ref[TOP_K * (step * gt + r) + k], r).start()
            return carry
        lax.fori_loop(0, gt, body, 0, unroll=DMA_LOOP_UNROLL)

    def drain(s):
        def body(r, carry):
            for k in range(TOP_K):
                _row_copy(ys_hbm, buf.at[s, k], sems.at[s], 0, r).wait()
            return carry
        lax.fori_loop(0, gt, body, 0, unroll=DMA_LOOP_UNROLL)

    @pl.when(i == 0)
    def _():
        issue(0, 0)

    @pl.when(i + 1 < pl.num_programs(0))
    def _():
        issue(i + 1, 1 - slot)

    drain(slot)
    y = buf[slot, 0]
    for k in range(1, TOP_K):
        y = y + buf[slot, k]
    x = x_ref[...] + gate_ref[...] * y
    if final_norm:
        x = x * lax.rsqrt(jnp.mean(x * x, axis=-1, keepdims=True) + EPS) * g_ref[...]
    o_ref[...] = x


def _moe_combine(ys, pos, x, gate, norm_g, final_norm, gt=128):
    t, d = x.shape
    gt = min(gt, t)
    row = pl.BlockSpec((1, d), lambda i, p: (0, 0))
    tilespec = pl.BlockSpec((gt, d), lambda i, p: (i, 0))
    vm = (2 * TOP_K + 6) * _nbytes((gt, d), F32)
    return pl.pallas_call(
        functools.partial(_moe_combine_body, final_norm=final_norm),
        grid_spec=pltpu.PrefetchScalarGridSpec(
            num_scalar_prefetch=1,
            grid=(t // gt,),
            in_specs=[pl.BlockSpec(memory_space=pl.ANY), tilespec, row, row],
            out_specs=tilespec,
            scratch_shapes=[pltpu.VMEM((2, TOP_K, gt, d), F32), pltpu.SemaphoreType.DMA((2,))],
        ),
        out_shape=jax.ShapeDtypeStruct((t, d), F32),
        compiler_params=_cparams(("arbitrary",), vm),
        name="moe_combine",
    )(pos, ys, x, gate.reshape(1, d), norm_g.reshape(1, d))


MOE_ROW_TILE = 256


def _moe_ffn(h32, x, gate, p, norm_g, final_norm):
    route = _router(h32, p["router_w"])
    row_token, row_w, pos, tiles = _routing_tables(route, MOE_ROW_TILE)
    xs = _gather_rows(h32, row_token)
    act = _moe_up(xs, p["exp_w_gate"], p["exp_w_up"], tiles, MOE_ROW_TILE)
    ys = _moe_down(act, p["exp_w_down"], row_w, tiles, MOE_ROW_TILE)
    return _moe_combine(ys, pos, x, gate, norm_g, final_norm)


def _silu(v):
    return v * jax.nn.sigmoid(v)


def _in_weights(w_in_all, layer, d):
    qk, vw, r = d // 2, d, GLA_GATE_RANK
    col_v = qk
    col_a = col_v + vw
    col_q = col_a + 2 * r
    col_g = col_q + qk
    col_hy = col_g + vw
    col_gate = col_hy + 3 * d
    return dict(wt=jnp.swapaxes(w_in_all, 1, 2), layer=layer, k=(0, qk), v=(col_v, vw), a=(col_a, 2 * r),
                q=(col_q, qk), g=(col_g, vw), hy=(col_hy, 3 * d), gate=(col_gate, 2 * d))


def _proj_body(x_ref, wt_hbm, o_ref, w32, wbf, sem, *, layer, row0, epi):
    j, i = pl.program_id(0), pl.program_id(1)
    rows = w32.shape[0]

    def fetch(block):
        start = pl.multiple_of(row0 + block * rows, SUBLANES)
        return pltpu.make_async_copy(wt_hbm.at[layer, pl.ds(start, rows), :], w32, sem)

    @pl.when(i == 0)
    def _():
        @pl.when(j == 0)
        def _():
            fetch(j).start()

        fetch(j).wait()
        if rows == wbf.shape[0]:
            wbf[...] = w32[...].astype(BF16)
        else:
            wbf[...] = jnp.zeros_like(wbf)
            wbf[:rows, :] = w32[...].astype(BF16)

        @pl.when(j + 1 < pl.num_programs(0))
        def _():
            fetch(j + 1).start()

    o_ref[...] = epi(_dot_nt(x_ref[...], wbf[...])).astype(o_ref.dtype)


def _proj(h, wi, group, epi, out_dtype, tm=1024, tn=1024):
    m, d = h.shape
    col, width = wi[group]
    rows = min(tn, width)
    tn = max(rows, LANES)
    tm = min(tm, m)
    assert m % tm == 0 and width % rows == 0 and col % SUBLANES == 0 and rows % SUBLANES == 0
    n_out = width // rows * tn
    vm = _nbytes((rows, d), F32) + _nbytes((tn, d), BF16) + 2 * _nbytes((tm, d), BF16) + 4 * _nbytes((tm, tn), F32)
    out, = pl.pallas_call(
        functools.partial(_proj_body, layer=wi["layer"], row0=col, epi=epi),
        grid=(width // rows, m // tm),
        in_specs=[pl.BlockSpec((tm, d), lambda j, i: (i, 0)), pl.BlockSpec(memory_space=pl.ANY)],
        out_specs=[pl.BlockSpec((tm, tn), lambda j, i: (i, j))],
        out_shape=[jax.ShapeDtypeStruct((m, n_out), out_dtype)],
        scratch_shapes=[pltpu.VMEM((rows, d), F32), pltpu.VMEM((tn, d), BF16), pltpu.SemaphoreType.DMA(())],
        compiler_params=_cparams(("arbitrary", "arbitrary"), vm),
        name="proj_" + group,
    )(h, wi["wt"])
    return out


def _gate_matrices(p, d):
    r = GLA_GATE_RANK
    qk = d // 2
    awf = jnp.zeros((LANES, qk), F32).at[:r].set(p["gla_aw_f"])
    awb = jnp.zeros((LANES, qk), F32).at[r:2 * r].set(p["gla_aw_b"])
    return awf, awb, p["gla_ab_f"].reshape(1, qk), p["gla_ab_b"].reshape(1, qk)


GLA_CHUNK = 256


def _gla_states_only(h, wi, p, d):
    ident = lambda acc: acc
    k = _proj(h, wi, "k", ident, BF16)
    v = _proj(h, wi, "v", ident, BF16)
    a = _proj(h, wi, "a", ident, F32)
    awf, awb, abf, abb = _gate_matrices(p, d)
    dk, dv = d // 2 // GLA_HEADS, d // GLA_HEADS
    zero = jnp.zeros((GLA_HEADS, dv, dk), F32)
    _, s_f = _gla_scan(k, k, v, a, awf, abf, zero, fwd=True, chunk=GLA_CHUNK)
    _, s_b = _gla_scan(k, k, v, a, awb, abb, zero, fwd=False, chunk=GLA_CHUNK)
    return s_f, s_b


def _token_mixer(x, h, mod_gate, s0_f, s0_b, wi, p, d):
    l = h.shape[0]
    dk = d // 2 // GLA_HEADS
    ident = lambda acc: acc
    k = _proj(h, wi, "k", ident, BF16)
    v = _proj(h, wi, "v", ident, BF16)
    a = _proj(h, wi, "a", ident, F32)
    q = _proj(h, wi, "q", lambda acc: acc * (dk ** -0.5), BF16)
    sg = _proj(h, wi, "g", _silu, BF16)
    hy = _proj(h, wi, "hy", ident, BF16)
    gates = _proj(h, wi, "gate", jax.nn.sigmoid, BF16)

    awf, awb, abf, abb = _gate_matrices(p, d)
    o_f, s_f = _gla_scan(q, k, v, a, awf, abf, s0_f, fwd=True, chunk=GLA_CHUNK)
    o_gla, s_b = _gla_scan(q, k, v, a, awb, abb, s0_b, fwd=False, chunk=GLA_CHUNK,
                           combine=(o_f, sg, p["gla_norm_g"]))

    x0, z = _hyena_shortconv(hy, p["hy_conv_w"], p["hy_conv_b"])
    hf, hg, ss = _hyena_filter(l, p)
    o_hy = _hyena_longconv(x0, z, hf, hg, ss, p["hy_bias"])

    layer = wi["layer"]
    t1, = _mm(o_hy, [wi["w_up_hy"]], lambda accs, ex: [ex[0].astype(F32) * accs[0]], (F32,),
              extras=[(gates, "tile", 0)], w_layer=layer, name="up_hy")
    merged, = _mm(o_gla, [wi["w_up_gla"]], lambda accs, ex: [ex[1] + ex[0].astype(F32) * accs[0]], (BF16,),
                  extras=[(gates, "tile", d), (t1, "tile", 0)], w_layer=layer, name="up_gla_merge")
    x_new, = _mm(merged, [wi["w_out"]], lambda accs, ex: [ex[0] + ex[1] * accs[0]], (F32,),
                 extras=[(x, "tile", 0), (mod_gate.reshape(1, d), "row", 0)], w_layer=layer, name="out_proj")
    return x_new, s_f, s_b


def _dense_ffn(x, h, mod_gate, wg, wu, wd):
    d, f = wg.shape
    fp = -(-f // 512) * 512
    wg_b = _pad2(wg.astype(BF16), d, fp)
    wu_b = _pad2(wu.astype(BF16), d, fp)
    wd_b = _pad2(wd.astype(BF16), fp, d)
    act, = _mm(h, [wg_b, wu_b], lambda accs, ex: [_silu(accs[0]) * accs[1]], (BF16,), tn=512, name="ffn_up")
    x_new, = _mm(act, [wd_b], lambda accs, ex: [ex[0] + ex[1] * accs[0]], (F32,), tm=512,
                 extras=[(x, "tile", 0), (mod_gate.reshape(1, d), "row", 0)], name="ffn_down")
    return x_new


def kernel(x, c, ctx, c_ctx, ada_w, ada_b, norm_mix_g, norm_ffn_g, w_in, hy_conv_w, hy_conv_b, hy_w1, hy_b1, hy_w2, hy_b2, hy_w3, hy_b3, hy_w4, hy_freq, hy_bias, gla_aw_f, gla_ab_f, gla_aw_b, gla_ab_b, gla_norm_g, w_up_hy, w_up_gla, w_out, ffn_w_gate, ffn_w_up, ffn_w_down, router_w, exp_w_gate, exp_w_up, exp_w_down, final_norm_g):
    assert x.shape[0] == 1 and c.shape[0] == 1, "batch size 1 only"
    depth, d = norm_mix_g.shape
    x_lat, x_ctx = x[0], ctx[0]
    mods = _ada_modulation(jnp.concatenate([c, c_ctx.reshape(1, d)], axis=0), ada_w, ada_b)
    dk, dv = d // 2 // GLA_HEADS, d // GLA_HEADS
    per_layer = dict(hy_conv_w=hy_conv_w, hy_conv_b=hy_conv_b, hy_w1=hy_w1, hy_b1=hy_b1, hy_w2=hy_w2, hy_b2=hy_b2,
                     hy_w3=hy_w3, hy_b3=hy_b3, hy_w4=hy_w4, hy_freq=hy_freq, hy_bias=hy_bias,
                     gla_aw_f=gla_aw_f, gla_ab_f=gla_ab_f, gla_aw_b=gla_aw_b, gla_ab_b=gla_ab_b,
                     gla_norm_g=gla_norm_g)
    for l in range(depth):
        last = l == depth - 1
        p = {name: arr[l] for name, arr in per_layer.items()}
        wi = dict(_in_weights(w_in, l, d), w_up_hy=w_up_hy, w_up_gla=w_up_gla, w_out=w_out)
        lat = [mods[l, 0, i * d:(i + 1) * d] for i in range(N_ADA)]
        cxm = [mods[l, 1, i * d:(i + 1) * d] for i in range(N_ADA)]

        h_ctx, = _modnorm(x_ctx, norm_mix_g[l], cxm[0], cxm[1], (BF16,))
        if last:
            s_f, s_b = _gla_states_only(h_ctx, wi, p, d)
        else:
            zero = jnp.zeros((GLA_HEADS, dv, dk), F32)
            x_ctx, s_f, s_b = _token_mixer(x_ctx, h_ctx, cxm[2], zero, zero, wi, p, d)
        h_lat, = _modnorm(x_lat, norm_mix_g[l], lat[0], lat[1], (BF16,))
        x_lat, _, _ = _token_mixer(x_lat, h_lat, lat[2], s_f, s_b, wi, p, d)

        i = l // 2
        if l % 2 == 0:
            h2, = _modnorm(x_lat, norm_ffn_g[l], lat[3], lat[4], (BF16,))
            x_lat = _dense_ffn(x_lat, h2, lat[5], ffn_w_gate[i], ffn_w_up[i], ffn_w_down[i])
            if not last:
                h2c, = _modnorm(x_ctx, norm_ffn_g[l], cxm[3], cxm[4], (BF16,))
                x_ctx = _dense_ffn(x_ctx, h2c, cxm[5], ffn_w_gate[i], ffn_w_up[i], ffn_w_down[i])
        else:
            pm = dict(router_w=router_w[i], exp_w_gate=exp_w_gate[i], exp_w_up=exp_w_up[i], exp_w_down=exp_w_down[i])
            h2, = _modnorm(x_lat, norm_ffn_g[l], lat[3], lat[4], (F32,))
            x_lat = _moe_ffn(h2, x_lat, lat[5], pm, final_norm_g, final_norm=last)
            if not last:
                h2c, = _modnorm(x_ctx, norm_ffn_g[l], cxm[3], cxm[4], (F32,))
                x_ctx = _moe_ffn(h2c, x_ctx, cxm[5], pm, final_norm_g, final_norm=False)
    if depth % 2 == 1:
        x_lat, = _modnorm(x_lat, final_norm_g, jnp.zeros((d,), F32), jnp.zeros((d,), F32), (F32,))
    return x_lat[None]
```

```python
import functools
import math

import numpy as np
import jax
import jax.numpy as jnp
from jax import lax
from jax.experimental import pallas as pl
from jax.experimental.pallas import tpu as pltpu

F32 = jnp.float32
BF16 = jnp.bfloat16
HIGHEST = lax.Precision.HIGHEST

EPS = 1e-6
N_ADA = 6
LANES = 128
SUBLANES = 8
VMEM_LIMIT_CAP = 60 * 1024 * 1024

GLA_HEADS = 4
GLA_GATE_RANK = 16
GLA_GATE_TEMP = 16.0
HY_SHORT = 3
HY_EMB_BANDS = 16
HY_FILTER_HIDDEN = 64
HY_DECAY_TARGET = 1e-2
HY_FAST_DECAY = 0.3
HY_SLOW_DECAY = 1.5
N_EXPERTS = 8
TOP_K = 2


def _cparams(sem, vmem_bytes):
    limit = int(min(max(vmem_bytes * 5 // 4 + (2 << 20), 16 << 20), VMEM_LIMIT_CAP))
    return pltpu.CompilerParams(dimension_semantics=sem, vmem_limit_bytes=limit)


def _nbytes(shape, dtype):
    return int(np.prod(shape)) * jnp.dtype(dtype).itemsize


def _ada_body(c_ref, w_ref, b_ref, o_ref):
    tn = o_ref.shape[-1]
    d = c_ref.shape[1]
    rows = []
    for r in range(2):
        s = c_ref[r]
        s = s * jax.nn.sigmoid(s)
        chunks = []
        for n0 in range(0, tn, LANES):
            p = w_ref[:, n0:n0 + LANES] * s
            acc = p.reshape(d // SUBLANES, SUBLANES, LANES).sum(axis=0)
            chunks.append(acc.sum(axis=0, keepdims=True))
        rows.append(jnp.concatenate(chunks, axis=1))
    o_ref[...] = jnp.concatenate(rows, axis=0) + b_ref[...]


def _ada_modulation(cond2, ada_w, ada_b):
    depth, d, n = ada_w.shape
    tn = 1536 if n % 1536 == 0 else LANES
    cb = jnp.broadcast_to(cond2[:, :, None], (2, d, LANES))
    vm = 2 * _nbytes((d, tn), F32) + 2 * _nbytes((2, d, LANES), F32)
    return pl.pallas_call(
        _ada_body,
        grid=(depth, n // tn),
        in_specs=[
            pl.BlockSpec((2, d, LANES), lambda l, j: (0, 0, 0)),
            pl.BlockSpec((None, d, tn), lambda l, j: (l, 0, j)),
            pl.BlockSpec((None, 1, tn), lambda l, j: (l, 0, j)),
        ],
        out_specs=pl.BlockSpec((None, 2, tn), lambda l, j: (l, 0, j)),
        out_shape=jax.ShapeDtypeStruct((depth, 2, n), F32),
        compiler_params=_cparams(("parallel", "parallel"), vm),
        name="ada_modulation",
    )(cb, ada_w, ada_b.reshape(depth, 1, n))


def _modnorm_body(x_ref, g_ref, sh_ref, sc_ref, *o_refs):
    x = x_ref[...]
    ms = jnp.mean(x * x, axis=-1, keepdims=True)
    y = x * lax.rsqrt(ms + EPS) * g_ref[...]
    y = y * (1.0 + sc_ref[...]) + sh_ref[...]
    for o in o_refs:
        o[...] = y.astype(o.dtype)


def _modnorm(x, g, shift, scale, out_dtypes, tm=256):
    m, d = x.shape
    tm = min(tm, m)
    row = pl.BlockSpec((1, d), lambda i: (0, 0))
    tile = pl.BlockSpec((tm, d), lambda i: (i, 0))
    vm = 2 * _nbytes((tm, d), F32) * (1 + len(out_dtypes))
    outs = pl.pallas_call(
        _modnorm_body,
        grid=(m // tm,),
        in_specs=[tile, row, row, row],
        out_specs=[tile] * len(out_dtypes),
        out_shape=[jax.ShapeDtypeStruct((m, d), dt) for dt in out_dtypes],
        compiler_params=_cparams(("parallel",), vm),
        name="modnorm",
    )(x, g.reshape(1, d), shift.reshape(1, d), scale.reshape(1, d))
    return outs


def _mm_body(*refs, n_w, n_e, n_o, epi, cast_w):
    x_ref = refs[0]
    w_refs = refs[1:1 + n_w]
    e_refs = refs[1 + n_w:1 + n_w + n_e]
    o_refs = refs[1 + n_w + n_e:1 + n_w + n_e + n_o]
    w_bf16 = refs[1 + n_w + n_e + n_o:]
    x = x_ref[...].astype(BF16)
    if cast_w:
        @pl.when(pl.program_id(1) == 0)
        def _():
            for src, dst in zip(w_refs, w_bf16):
                dst[...] = src[...].astype(BF16)
        w_refs = w_bf16
    accs = [jnp.dot(x, w[...], preferred_element_type=F32) for w in w_refs]
    outs = epi(accs, [e[...] for e in e_refs])
    for o, v in zip(o_refs, outs):
        o[...] = v.astype(o.dtype)


def _mm(x, ws, epi, out_dtypes, extras=(), w_layer=None, tm=1024, tn=1024, name="mm"):
    m, k = x.shape
    n = ws[0].shape[-1]
    tm = min(tm, m)
    tn = min(tn, n)
    assert m % tm == 0 and n % tn == 0, (m, tm, n, tn)
    cast_w = ws[0].dtype == F32
    in_specs = [pl.BlockSpec((tm, k), lambda j, i: (i, 0))]
    if w_layer is None:
        in_specs += [pl.BlockSpec((k, tn), lambda j, i: (0, j))] * len(ws)
    else:
        in_specs += [pl.BlockSpec((None, k, tn), lambda j, i: (w_layer, 0, j))] * len(ws)
    vm = 2 * _nbytes((tm, k), x.dtype) + 2 * len(ws) * _nbytes((k, tn), ws[0].dtype)
    for arr, kind, col in extras:
        assert col % tn == 0
        if kind == "tile":
            in_specs.append(pl.BlockSpec((tm, tn), lambda j, i, c=col // tn: (i, j + c)))
            vm += 2 * _nbytes((tm, tn), arr.dtype)
        else:
            in_specs.append(pl.BlockSpec((1, tn), lambda j, i, c=col // tn: (0, j + c)))
    vm += sum(2 * _nbytes((tm, tn), dt) for dt in out_dtypes) + (1 + len(ws)) * _nbytes((tm, tn), F32)
    scratch = [pltpu.VMEM((k, tn), BF16)] * len(ws) if cast_w else []
    vm += len(scratch) * _nbytes((k, tn), BF16)
    return pl.pallas_call(
        functools.partial(_mm_body, n_w=len(ws), n_e=len(extras), n_o=len(out_dtypes), epi=epi, cast_w=cast_w),
        grid=(n // tn, m // tm),
        in_specs=in_specs,
        out_specs=[pl.BlockSpec((tm, tn), lambda j, i: (i, j))] * len(out_dtypes),
        out_shape=[jax.ShapeDtypeStruct((m, n), dt) for dt in out_dtypes],
        scratch_shapes=scratch,
        compiler_params=_cparams(("parallel", "arbitrary" if cast_w else "parallel"), vm),
        name=name,
    )(x, *ws, *[a for a, _, _ in extras])


def _log_sigmoid(z):
    return jnp.minimum(z, 0.0) - jnp.log1p(jnp.exp(-jnp.abs(z)))


def _split_bf16(x, terms):
    parts = []
    for _ in range(terms):
        p = x.astype(BF16)
        parts.append(p)
        x = x - p.astype(F32)
    return parts


def _gla_decay_body(a_ref, awf_ref, awb_ref, abf_ref, abb_ref, ef_ref, eb_ref):
    c = a_ref.shape[0]
    a = a_ref[...]
    r = lax.broadcasted_iota(jnp.int32, (c, c), 0)
    s = lax.broadcasted_iota(jnp.int32, (c, c), 1)
    lower = (s <= r).astype(BF16)
    upper = (s >= r).astype(BF16)
    a_parts = _split_bf16(a, 2)

    def gate_logits(w_ref, b_ref):
        w_hi, w_lo = _split_bf16(w_ref[...], 2)
        z = jnp.dot(a_parts[0], w_hi, preferred_element_type=F32)
        z = z + jnp.dot(a_parts[1], w_hi, preferred_element_type=F32)
        z = z + jnp.dot(a_parts[0], w_lo, preferred_element_type=F32)
        return z + b_ref[...]

    def chunk_sums(tri, g):
        return sum(jnp.dot(tri, part, preferred_element_type=F32) for part in _split_bf16(g, 3))

    gf = _log_sigmoid(gate_logits(awf_ref, abf_ref)) * (1.0 / GLA_GATE_TEMP)
    gb = _log_sigmoid(gate_logits(awb_ref, abb_ref)) * (1.0 / GLA_GATE_TEMP)
    ef_ref[...] = chunk_sums(lower, gf)
    eb_ref[...] = chunk_sums(upper, gb)


def _gla_decay(a, awf, awb, abf, abb, chunk, tn=512):
    l = a.shape[0]
    n = awf.shape[1]
    tn = min(tn, n)
    col = pl.BlockSpec((a.shape[1], tn), lambda i, j: (0, j))
    row = pl.BlockSpec((1, tn), lambda i, j: (0, j))
    out = pl.BlockSpec((chunk, tn), lambda i, j: (i, j))
    vm = 4 * _nbytes((chunk, tn), F32) * 3 + 4 * _nbytes((a.shape[1], tn), F32)
    return pl.pallas_call(
        _gla_decay_body,
        grid=(l // chunk, n // tn),
        in_specs=[pl.BlockSpec((chunk, a.shape[1]), lambda i, j: (i, 0)), col, col, row, row],
        out_specs=[out, out],
        out_shape=[jax.ShapeDtypeStruct((l, n), F32)] * 2,
        compiler_params=_cparams(("parallel", "parallel"), vm),
        name="gla_decay",
    )(a, awf, awb, abf, abb)


def _dot_nt(a, b):
    return lax.dot_general(a, b, (((1,), (1,)), ((), ())), preferred_element_type=F32)


def _dot_tn(a, b):
    return lax.dot_general(a, b, (((0,), (0,)), ((), ())), preferred_element_type=F32)


def _bcast_rows(e, group, row):
    c, w = e.shape
    e3 = e.reshape(c // group, group, w)
    return jnp.broadcast_to(e3[:, row:row + 1, :], e3.shape).reshape(c, w)


def _gla_masks(c, base, fwd):
    i = lax.broadcasted_iota(jnp.int32, (c, c), 0)
    j = lax.broadcasted_iota(jnp.int32, (c, c), 1)
    sh = int(math.log2(base))
    order = (j <= i) if fwd else (j >= i)
    masks = [((i >> sh) == (j >> sh)) & order]
    s = base
    while 2 * s <= c:
        sh += 1
        masks.append((i >> sh) == (j >> sh))
        s *= 2
    return masks


def _gla_chunk_head(q, k, v, e, st, masks, *, fwd, base):
    c, dk = q.shape
    row = lax.broadcasted_iota(jnp.int32, (c, 1), 0)
    d0 = e - _bcast_rows(e, base, base // 2 - 1 if fwd else base // 2)
    q0 = (q * jnp.exp(d0)).astype(BF16)
    k0 = (k * jnp.exp(-d0)).astype(BF16)
    att = jnp.where(masks[0], _dot_nt(q0, k0), 0.0)
    s, lvl = base, 1
    while 2 * s <= c:
        d = e - _bcast_rows(e, 2 * s, s - 1 if fwd else s)
        later = ((row >> int(math.log2(s))) & 1) == (1 if fwd else 0)
        x = jnp.exp(jnp.where(later, d, -d))
        ql = jnp.where(later, q * x, 0.0).astype(BF16)
        kl = jnp.where(later, 0.0, k * x).astype(BF16)
        att = att + jnp.where(masks[lvl], _dot_nt(ql, kl), 0.0)
        s *= 2
        lvl += 1
    e_edge = e[c - 1:c] if fwd else e[0:1]
    qs = (q * jnp.exp(e)).astype(BF16)
    ks = (k * jnp.exp(e_edge - e)).astype(BF16)
    o = jnp.dot(att.astype(BF16), v, preferred_element_type=F32) + _dot_nt(qs, st.astype(BF16))
    st_new = st * jnp.exp(e_edge) + _dot_tn(v, ks)
    return o, st_new


def _gla_scan_body(*refs, fwd, base, combine):
    if combine:
        q_ref, k_ref, v_ref, e_ref, s0_ref, of_ref, sg_ref, gn_ref, o_ref, sfin_ref, st_ref = refs
    else:
        q_ref, k_ref, v_ref, e_ref, s0_ref, o_ref, sfin_ref, st_ref = refs
    step = pl.program_id(0)
    c = q_ref.shape[0]
    dk = q_ref.shape[1] // GLA_HEADS
    dv = v_ref.shape[1] // GLA_HEADS

    @pl.when(step == 0)
    def _():
        st_ref[...] = s0_ref[...]

    masks = _gla_masks(c, base, fwd)
    for h in range(GLA_HEADS):
        ks = slice(h * dk, (h + 1) * dk)
        vs = slice(h * dv, (h + 1) * dv)
        o, st_new = _gla_chunk_head(
            q_ref[:, ks].astype(F32), k_ref[:, ks].astype(F32), v_ref[:, vs], e_ref[:, ks], st_ref[h],
            masks, fwd=fwd, base=base)
        st_ref[h] = st_new
        if combine:
            t = o + of_ref[:, vs]
            t = t * lax.rsqrt(jnp.mean(t * t, axis=-1, keepdims=True) + EPS) * gn_ref[...]
            o_ref[:, vs] = (t * sg_ref[:, vs].astype(F32)).astype(o_ref.dtype)
        else:
            o_ref[:, vs] = o.astype(o_ref.dtype)

    @pl.when(step == pl.num_programs(0) - 1)
    def _():
        sfin_ref[...] = st_ref[...]


def _gla_scan(q, k, v, e, s0, *, fwd, chunk, base=32, combine=None):
    l, hdk = q.shape
    hdv = v.shape[1]
    dk, dv = hdk // GLA_HEADS, hdv // GLA_HEADS
    chunk = min(chunk, l)
    n = l // chunk
    idx = (lambda i: (i, 0)) if fwd else (lambda i: (n - 1 - i, 0))
    st_spec = pl.BlockSpec((GLA_HEADS, dv, dk), lambda i: (0, 0, 0))
    in_specs = [pl.BlockSpec((chunk, hdk), idx), pl.BlockSpec((chunk, hdk), idx),
                pl.BlockSpec((chunk, hdv), idx), pl.BlockSpec((chunk, hdk), idx), st_spec]
    args = [q, k, v, e, s0]
    vm = 2 * (2 * _nbytes((chunk, hdk), BF16) + _nbytes((chunk, hdv), BF16) + _nbytes((chunk, hdk), F32))
    vm += 3 * _nbytes((GLA_HEADS, dv, dk), F32) * 2 + 2 * _nbytes((chunk, hdv), F32)
    if combine is not None:
        o_other, gate, norm_g = combine
        in_specs += [pl.BlockSpec((chunk, hdv), idx), pl.BlockSpec((chunk, hdv), idx),
                     pl.BlockSpec((1, dv), lambda i: (0, 0))]
        args += [o_other, gate, norm_g.reshape(1, dv)]
        vm += 2 * (_nbytes((chunk, hdv), F32) + _nbytes((chunk, hdv), BF16))
    vm += 24 * _nbytes((chunk, max(dk, chunk)), F32)
    return pl.pallas_call(
        functools.partial(_gla_scan_body, fwd=fwd, base=base, combine=combine is not None),
        grid=(n,),
        in_specs=in_specs,
        out_specs=[pl.BlockSpec((chunk, hdv), idx), st_spec],
        out_shape=[jax.ShapeDtypeStruct((l, hdv), BF16 if combine is not None else F32),
                   jax.ShapeDtypeStruct((GLA_HEADS, dv, dk), F32)],
        scratch_shapes=[pltpu.VMEM((GLA_HEADS, dv, dk), F32)],
        compiler_params=_cparams(("arbitrary",), vm),
        name="gla_scan_fwd" if fwd else "gla_scan_bwd",
    )(*args)


def _shortconv_body(u0, u1, u2, p0, p1, p2, n0, n1, n2, w0, w1, w2, b0, b1, b2, x0_ref, z_ref):
    i = pl.program_id(0)
    last = pl.num_programs(0) - 1
    tm = u0.shape[0]
    row = lax.broadcasted_iota(jnp.int32, (tm, 1), 0)

    def conv(u_ref, p_ref, n_ref, w_ref, b_ref):
        u = u_ref[...].astype(F32)
        halo = p_ref.shape[0]
        prev_row = jnp.where(i == 0, 0.0, p_ref[...].astype(F32)[halo - 1:halo, :])
        next_row = jnp.where(i == last, 0.0, n_ref[...].astype(F32)[0:1, :])
        before = jnp.where(row == 0, prev_row, pltpu.roll(u, 1, axis=0))
        after = jnp.where(row == tm - 1, next_row, pltpu.roll(u, tm - 1, axis=0))
        return b_ref[...] + before * w_ref[0:1, :] + u * w_ref[1:2, :] + after * w_ref[2:3, :]

    x0_ref[...] = conv(u0, p0, n0, w0, b0).astype(x0_ref.dtype)
    z_ref[...] = conv(u1, p1, n1, w1, b1) * conv(u2, p2, n2, w2, b2)


def _hyena_shortconv(hy, conv_w, conv_b, tm=512, cb=512):
    l, w3 = hy.shape
    w = w3 // 3
    tm = min(tm, l)
    cb = min(cb, w)
    nb = w // cb
    halo = SUBLANES * 4 // jnp.dtype(hy.dtype).itemsize
    hb = tm // halo
    n_halo = l // halo
    cur = [pl.BlockSpec((tm, cb), lambda i, j, g=g: (i, g * nb + j)) for g in range(3)]
    prv = [pl.BlockSpec((halo, cb), lambda i, j, g=g: (jnp.maximum(i * hb - 1, 0), g * nb + j)) for g in range(3)]
    nxt = [pl.BlockSpec((halo, cb), lambda i, j, g=g: (jnp.minimum((i + 1) * hb, n_halo - 1), g * nb + j))
           for g in range(3)]
    wsp = [pl.BlockSpec((HY_SHORT, cb), lambda i, j, g=g: (0, g * nb + j)) for g in range(3)]
    bsp = [pl.BlockSpec((1, cb), lambda i, j, g=g: (0, g * nb + j)) for g in range(3)]
    out = pl.BlockSpec((tm, cb), lambda i, j: (i, j))
    vm = 2 * 5 * _nbytes((tm, cb), F32) + 8 * _nbytes((tm, cb), F32)
    return pl.pallas_call(
        _shortconv_body,
        grid=(l // tm, nb),
        in_specs=cur + prv + nxt + wsp + bsp,
        out_specs=[out, out],
        out_shape=[jax.ShapeDtypeStruct((l, w), BF16), jax.ShapeDtypeStruct((l, w), F32)],
        compiler_params=_cparams(("parallel", "parallel"), vm),
        name="hyena_shortconv",
    )(hy, hy, hy, hy, hy, hy, hy, hy, hy, conv_w, conv_w, conv_w,
      conv_b.reshape(1, w3), conv_b.reshape(1, w3), conv_b.reshape(1, w3))


def _filter_body(w1_ref, b1_ref, w2_ref, b2_ref, w3_ref, b3_ref, fr_ref, w4f_ref, w4b_ref,
                 hf_ref, hg_ref, ss_ref, *, seq_len):
    i = pl.program_id(0)
    tr = hf_ref.shape[0]
    wdt = hf_ref.shape[1]
    half = tr // 2
    side_w = LANES // 2
    lane = lax.broadcasted_iota(jnp.int32, (1, LANES), 1)
    local = lane & (side_w - 1)
    row = lax.broadcasted_iota(jnp.int32, (half, 1), 0) + i * tr
    pos = (row + jnp.where(lane >= side_w, half, 0)).astype(F32)
    last_pos = float(max(seq_len - 1, 1))
    band = ((local - 1) & (HY_EMB_BANDS - 1)).astype(F32)
    bands = 1e-4 + band * ((HY_EMB_BANDS - 1 - 1e-4) / (HY_EMB_BANDS - 1))
    ang = ((2.0 * math.pi / seq_len) * pos) * bands
    trig = jnp.cos(ang + jnp.where(local > HY_EMB_BANDS, 0.5 * math.pi, 0.0))
    emb = jnp.where(local == 0, pos / last_pos, jnp.where(local <= 2 * HY_EMB_BANDS, trig, 0.0))
    fr = fr_ref[...]
    h = jnp.sin(fr * (jnp.dot(emb, w1_ref[...], preferred_element_type=F32, precision=HIGHEST) + b1_ref[...]))
    h = jnp.sin(fr * (jnp.dot(h, w2_ref[...], preferred_element_type=F32, precision=HIGHEST) + b2_ref[...]))
    h = jnp.sin(fr * (jnp.dot(h, w3_ref[...], preferred_element_type=F32, precision=HIGHEST) + b3_ref[...]))
    hb = h.astype(BF16)
    ch = lax.broadcasted_iota(jnp.int32, (1, wdt), 1).astype(F32)
    lo = math.log(HY_DECAY_TARGET) / HY_SLOW_DECAY
    hi = math.log(HY_DECAY_TARGET) / HY_FAST_DECAY
    deltas = jnp.abs(lo + ch * ((hi - lo) / (wdt - 1)))
    ss = jnp.zeros((1, wdt), F32)
    for side in range(2):
        rows = slice(side * half, (side + 1) * half)
        p_side = (lax.broadcasted_iota(jnp.int32, (half, 1), 0) + i * tr + side * half).astype(F32)
        window = jnp.exp(-(p_side / last_pos) * deltas)
        hf = jnp.dot(hb, w4f_ref[side].astype(BF16), preferred_element_type=F32) * window
        hg = jnp.dot(hb, w4b_ref[side].astype(BF16), preferred_element_type=F32) * window
        hg = jnp.where(p_side == 0.0, 0.0, hg)
        hf_ref[rows, :] = hf
        hg_ref[rows, :] = hg
        ss = ss + jnp.sum(hf * hf + hg * hg, axis=0, keepdims=True)

    @pl.when(i == 0)
    def _():
        ss_ref[...] = jnp.zeros_like(ss_ref)

    ss_ref[...] += ss


def _pad2(a, rows, cols):
    return jnp.zeros((rows, cols), a.dtype).at[:a.shape[0], :a.shape[1]].set(a)


def _hyena_filter(seq_len, p, tr=256):
    wdt = p["hy_w4"].shape[1] // 2
    tr = min(tr, seq_len)
    side = LANES // 2
    hidden = p["hy_w2"].shape[0]
    assert hidden <= side and p["hy_w1"].shape[0] <= side and tr % (2 * SUBLANES) == 0

    def two_sided(w):
        blk = _pad2(w, side, side)
        zero = jnp.zeros_like(blk)
        return jnp.concatenate([jnp.concatenate([blk, zero], axis=1), jnp.concatenate([zero, blk], axis=1)], axis=0)

    def one_side(w, s):
        blk = _pad2(w, side, w.shape[1])
        zero = jnp.zeros_like(blk)
        return jnp.concatenate([blk, zero] if s == 0 else [zero, blk], axis=0)

    w1, w2, w3 = (two_sided(p[k]) for k in ("hy_w1", "hy_w2", "hy_w3"))
    b1, b2, b3, fr = (jnp.tile(_pad2(p[k].reshape(1, -1), 1, side), (1, 2))
                      for k in ("hy_b1", "hy_b2", "hy_b3", "hy_freq"))
    w4f = jnp.stack([one_side(p["hy_w4"][:, :wdt], s) for s in range(2)])
    w4b = jnp.stack([one_side(p["hy_w4"][:, wdt:], s) for s in range(2)])
    full = lambda a: pl.BlockSpec(a.shape, lambda i: (0,) * a.ndim)
    out = pl.BlockSpec((tr, wdt), lambda i: (i, 0))
    args = (w1, b1, w2, b2, w3, b3, fr, w4f, w4b)
    vm = 8 * _nbytes((LANES, wdt), F32) + 8 * _nbytes((tr, wdt), F32)
    return pl.pallas_call(
        functools.partial(_filter_body, seq_len=seq_len),
        grid=(seq_len // tr,),
        in_specs=[full(a) for a in args],
        out_specs=[out, out, pl.BlockSpec((1, wdt), lambda i: (0, 0))],
        out_shape=[jax.ShapeDtypeStruct((seq_len, wdt), F32)] * 2 + [jax.ShapeDtypeStruct((1, wdt), F32)],
        compiler_params=_cparams(("arbitrary",), vm),
        name="hyena_filter",
    )(*args)


def _fft_dims(seq_len):
    n = 2 * seq_len
    p = 1 << ((n.bit_length() - 1) // 2)
    return p, n // p


@functools.lru_cache(maxsize=None)
def _fft_consts(seq_len):
    pp, mm = _fft_dims(seq_len)
    n = pp * mm
    ph = pp // 2
    a = np.arange(ph)
    b = np.arange(mm)
    d = np.arange(pp)
    eye = np.eye(SUBLANES)
    ang = 2 * np.pi * np.outer(d, a) / pp
    fk = np.kron(np.concatenate([np.cos(ang), -np.sin(ang)], axis=0), eye)
    bt = b.reshape(mm // SUBLANES, 1, SUBLANES)
    angt = (2 * np.pi * d[None, :, None] * bt / n).reshape(mm // SUBLANES, pp * SUBLANES, 1)
    angc = -2 * np.pi * np.outer(b, b) / mm
    cr, ci = np.cos(angc), np.sin(angc)
    w2 = np.block([[cr, -ci], [ci, cr]])
    v2 = np.block([[cr, ci], [-ci, cr]])
    gk = np.kron(np.concatenate([np.cos(ang.T), -np.sin(ang.T)], axis=1) / n, eye)
    as_bf16 = lambda x: np.asarray(x, dtype=np.float32).astype(BF16)
    return dict(fk=as_bf16(fk), w2=as_bf16(w2), v2=as_bf16(v2), gk=as_bf16(gk),
                twc=np.cos(angt).astype(np.float32), tws=np.sin(angt).astype(np.float32), pp=pp, mm=mm)


def _stage1_to_scratch(a_ref, fk_ref, twc_ref, tws_ref, group_rows, t):
    pp = a_ref.shape[0] // 2
    groups = twc_ref.shape[0]
    for g in range(groups):
        zt = group_rows(g)
        r = jnp.dot(fk_ref[...], zt, preferred_element_type=F32)
        rr, ri = r[:pp * SUBLANES], r[pp * SUBLANES:]
        c, s = twc_ref[g], tws_ref[g]
        a_ref[:pp, t * groups + g] = (rr * c + ri * s).reshape(pp, SUBLANES, zt.shape[-1])
        a_ref[pp:, t * groups + g] = (ri * c - rr * s).reshape(pp, SUBLANES, zt.shape[-1])


def _group_of_8(ref, g):
    blk = ref[:, g * SUBLANES:(g + 1) * SUBLANES, :]
    return blk.reshape(blk.shape[0] * SUBLANES, blk.shape[-1])


def _scratch_rows(a_ref, d):
    pp = a_ref.shape[0] // 2
    mm = a_ref.shape[1] * SUBLANES
    lanes = a_ref.shape[-1]
    return jnp.concatenate([a_ref[d].reshape(mm, lanes), a_ref[pp + d].reshape(mm, lanes)], axis=0).astype(BF16)


def _bf16_bits(x):
    u = lax.bitcast_convert_type(x, jnp.uint32)
    u = u + jnp.uint32(0x7FFF) + ((u >> 16) & jnp.uint32(1))
    return u & jnp.uint32(0xFFFF0000)


def _pack_ri(re, im):
    return _bf16_bits(re) | (_bf16_bits(im) >> 16)


def _unpack_ri(word):
    re = lax.bitcast_convert_type(word & jnp.uint32(0xFFFF0000), F32)
    im = lax.bitcast_convert_type(word << 16, F32)
    return re, im


def _fft_spec_body(hf_ref, hg_ref, fk_ref, twc_ref, tws_ref, w2_ref, o_ref, a_ref, *, nb):
    t = pl.program_id(1)
    cb = hf_ref.shape[-1]
    mm = a_ref.shape[1] * SUBLANES

    @pl.when(t < nb)
    def _():
        rows = lambda g: jnp.concatenate([_group_of_8(hf_ref, g), _group_of_8(hg_ref, g)], axis=-1).astype(BF16)
        _stage1_to_scratch(a_ref, fk_ref, twc_ref, tws_ref, rows, t)

    @pl.when(t >= nb)
    def _():
        nd_step = o_ref.shape[0]
        for dl in range(nd_step):
            h = jnp.dot(w2_ref[...], _scratch_rows(a_ref, (t - nb) * nd_step + dl), preferred_element_type=F32)
            o_ref[dl, :mm, :] = (h[:mm, :cb] + h[:mm, cb:]).astype(o_ref.dtype)
            o_ref[dl, mm:, :] = (h[mm:, :cb] - h[mm:, cb:]).astype(o_ref.dtype)


def _phase_specs(k, cb, sb, nb):
    pp = k["pp"]
    ph = pp // 2
    step = lambda t: jnp.minimum(t, nb - 1)
    return (pl.BlockSpec((ph, sb, cb), lambda j, t: (0, step(t), j)),
            pl.BlockSpec(k["fk"].shape, lambda j, t: (0, 0)),
            pl.BlockSpec((sb // SUBLANES, pp * SUBLANES, 1), lambda j, t: (step(t), 0, 0)))


FFT_B_PER_STEP = 16
FFT_D_PER_STEP = 16


def _fft_filter_spectrum(hf, hg, k, cb=LANES):
    l, c = hf.shape
    pp, mm = k["pp"], k["mm"]
    ph = pp // 2
    cb = min(cb, c)
    sb, ds = min(2 * FFT_B_PER_STEP, mm), min(2 * FFT_D_PER_STEP, pp)
    nb, nd = mm // sb, pp // ds
    taps, fk_spec, tw_spec = _phase_specs(k, cb, sb, nb)
    vm = (_nbytes((2 * pp, mm // SUBLANES, SUBLANES, 2 * cb), F32) + 4 * _nbytes((ph, sb, cb), F32)
          + 2 * _nbytes(k["fk"].shape, BF16) + 2 * _nbytes((ds, 2 * mm, cb), BF16)
          + 5 * _nbytes((2 * pp * SUBLANES, 2 * cb), F32))
    return pl.pallas_call(
        functools.partial(_fft_spec_body, nb=nb),
        grid=(c // cb, nb + nd),
        in_specs=[taps, taps, fk_spec, tw_spec, tw_spec, pl.BlockSpec(k["w2"].shape, lambda j, t: (0, 0))],
        out_specs=pl.BlockSpec((ds, 2 * mm, cb), lambda j, t: (jnp.maximum(t - nb, 0), 0, j)),
        out_shape=jax.ShapeDtypeStruct((pp, 2 * mm, c), BF16),
        scratch_shapes=[pltpu.VMEM((2 * pp, mm // SUBLANES, SUBLANES, 2 * cb), F32)],
        compiler_params=_cparams(("parallel", "arbitrary"), vm),
        name="fft_filter_spectrum",
    )(hf.reshape(ph, mm, c), hg.reshape(ph, mm, c), k["fk"], k["twc"], k["tws"], k["w2"])


def _fft_mid_body(z_ref, fk_ref, twc_ref, tws_ref, h_ref, w2_ref, v2_ref, o_ref, a_ref, *, nb):
    t = pl.program_id(1)
    mm = a_ref.shape[1] * SUBLANES

    @pl.when(t < nb)
    def _():
        _stage1_to_scratch(a_ref, fk_ref, twc_ref, tws_ref, lambda g: _group_of_8(z_ref, g).astype(BF16), t)

    @pl.when(t >= nb)
    def _():
        nd_step = o_ref.shape[0]
        for dl in range(nd_step):
            x = jnp.dot(w2_ref[...], _scratch_rows(a_ref, (t - nb) * nd_step + dl), preferred_element_type=F32)
            xr, xi = x[:mm], x[mm:]
            hr, hi = h_ref[dl, :mm, :].astype(F32), h_ref[dl, mm:, :].astype(F32)
            y = jnp.concatenate([xr * hr - xi * hi, xr * hi + xi * hr], axis=0).astype(BF16)
            bd = jnp.dot(v2_ref[...], y, preferred_element_type=F32)
            o_ref[dl] = _pack_ri(bd[:mm], bd[mm:])


def _fft_mid(z, h, k, cb=256):
    l, c = z.shape
    pp, mm = k["pp"], k["mm"]
    cb = min(cb, c)
    sb, ds = min(FFT_B_PER_STEP, mm), min(FFT_D_PER_STEP, pp)
    nb, nd = mm // sb, pp // ds
    sig, fk_spec, tw_spec = _phase_specs(k, cb, sb, nb)
    const = pl.BlockSpec(k["w2"].shape, lambda j, t: (0, 0))
    dstep = lambda t: jnp.maximum(t - nb, 0)
    vm = (_nbytes((2 * pp, mm // SUBLANES, SUBLANES, cb), F32) + 2 * _nbytes((pp // 2, sb, cb), F32)
          + 2 * _nbytes(k["fk"].shape, BF16) + 2 * _nbytes((ds, 2 * mm, cb), BF16)
          + 2 * _nbytes((ds, mm, cb), F32) + 5 * _nbytes((2 * pp * SUBLANES, cb), F32))
    return pl.pallas_call(
        functools.partial(_fft_mid_body, nb=nb),
        grid=(c // cb, nb + nd),
        in_specs=[sig, fk_spec, tw_spec, tw_spec,
                  pl.BlockSpec((ds, 2 * mm, cb), lambda j, t: (dstep(t), 0, j)), const, const],
        out_specs=pl.BlockSpec((ds, mm, cb), lambda j, t: (dstep(t), 0, j)),
        out_shape=jax.ShapeDtypeStruct((pp, mm, c), jnp.uint32),
        scratch_shapes=[pltpu.VMEM((2 * pp, mm // SUBLANES, SUBLANES, cb), F32)],
        compiler_params=_cparams(("parallel", "arbitrary"), vm),
        name="fft_mid",
    )(z.reshape(pp // 2, mm, c), k["fk"], k["twc"], k["tws"], h, k["w2"], k["v2"])


def _fft_last_body(b_ref, gk_ref, twc_ref, tws_ref, x0_ref, z_ref, ss_ref, bias_ref, o_ref):
    scale = lax.rsqrt(ss_ref[...] + EPS)
    ys = []
    for g in range(twc_ref.shape[0]):
        br, bi = _unpack_ri(_group_of_8(b_ref, g))
        c, s = twc_ref[g], tws_ref[g]
        rhs = jnp.concatenate([br * c - bi * s, br * s + bi * c], axis=0).astype(BF16)
        y = jnp.dot(gk_ref[...], rhs, preferred_element_type=F32)
        ys.append(y.reshape(o_ref.shape[0], SUBLANES, o_ref.shape[2]))
    y = jnp.concatenate(ys, axis=1)
    o_ref[...] = (x0_ref[...].astype(F32) * (y * scale + z_ref[...] * bias_ref[...])).astype(o_ref.dtype)


def _fft_last(bmat, k, x0, z, ss, bias, cb=256):
    pp, mm, c = bmat.shape
    ph = pp // 2
    cb = min(cb, c)
    sb = min(FFT_B_PER_STEP, mm)
    view = pl.BlockSpec((ph, sb, cb), lambda j, b: (0, b, j))
    row = pl.BlockSpec((1, cb), lambda j, b: (0, j))
    tw_spec = pl.BlockSpec((sb // SUBLANES, pp * SUBLANES, 1), lambda j, b: (b, 0, 0))
    vm = 2 * (_nbytes((pp, sb, cb), F32) + 3 * _nbytes((ph, sb, cb), F32)
              + _nbytes(k["gk"].shape, BF16)) + 6 * _nbytes((2 * pp * SUBLANES, cb), F32)
    out = pl.pallas_call(
        _fft_last_body,
        grid=(c // cb, mm // sb),
        in_specs=[pl.BlockSpec((pp, sb, cb), lambda j, b: (0, b, j)),
                  pl.BlockSpec(k["gk"].shape, lambda j, b: (0, 0)),
                  tw_spec, tw_spec, view, view, row, row],
        out_specs=view,
        out_shape=jax.ShapeDtypeStruct((ph, mm, c), BF16),
        compiler_params=_cparams(("parallel", "parallel"), vm),
        name="fft_last",
    )(bmat, k["gk"], k["twc"], k["tws"], x0.reshape(ph, mm, c), z.reshape(ph, mm, c), ss, bias.reshape(1, c))
    return out.reshape(ph * mm, c)


def _hyena_longconv(x0, z, hf, hg, ss, bias):
    k = _fft_consts(z.shape[0])
    h = _fft_filter_spectrum(hf, hg, k)
    return _fft_last(_fft_mid(z, h, k), k, x0, z, ss, bias)


def _router_body(h_ref, w_ref, o_ref):
    logits = jnp.dot(h_ref[...], w_ref[...], preferred_element_type=F32, precision=HIGHEST)
    lane = lax.broadcasted_iota(jnp.int32, logits.shape, 1)
    lg = jnp.where(lane < N_EXPERTS, logits, -jnp.inf)
    m1 = jnp.max(lg, axis=-1, keepdims=True)
    i1 = jnp.min(jnp.where(lg == m1, lane, LANES), axis=-1, keepdims=True)
    l2 = jnp.where(lane == i1, -jnp.inf, lg)
    m2 = jnp.max(l2, axis=-1, keepdims=True)
    i2 = jnp.min(jnp.where(l2 == m2, lane, LANES), axis=-1, keepdims=True)
    e = jnp.exp(m2 - m1)
    w1 = 1.0 / (1.0 + e)
    w2 = e * w1
    o_ref[...] = jnp.where(lane == 0, i1.astype(F32), jnp.where(lane == 1, i2.astype(F32),
                           jnp.where(lane == 2, w1, jnp.where(lane == 3, w2, 0.0))))


def _router(h, router_w, tm=256):
    m, d = h.shape
    tm = min(tm, m)
    w = _pad2(router_w, d, LANES)
    vm = 2 * (_nbytes((tm, d), F32) + _nbytes((d, LANES), F32)) + 8 * _nbytes((tm, LANES), F32)
    return pl.pallas_call(
        _router_body,
        grid=(m // tm,),
        in_specs=[pl.BlockSpec((tm, d), lambda i: (i, 0)), pl.BlockSpec((d, LANES), lambda i: (0, 0))],
        out_specs=pl.BlockSpec((tm, LANES), lambda i: (i, 0)),
        out_shape=jax.ShapeDtypeStruct((m, LANES), F32),
        compiler_params=_cparams(("parallel",), vm),
        name="moe_router",
    )(h, w)


def _routing_tables(route, tile):
    t = route.shape[0]
    e_flat = route[:, :TOP_K].astype(jnp.int32).reshape(-1)
    w_flat = route[:, TOP_K:2 * TOP_K].reshape(-1)
    onehot = (e_flat[:, None] == jnp.arange(N_EXPERTS, dtype=jnp.int32)[None, :]).astype(jnp.int32)
    csum = jnp.cumsum(onehot, axis=0)
    rank = jnp.take_along_axis(csum, e_flat[:, None], axis=1)[:, 0] - 1
    counts = csum[-1]
    padded = ((counts + tile - 1) // tile) * tile
    ends = jnp.cumsum(padded)
    pos = (ends - padded)[e_flat] + rank
    n_tiles = (t * TOP_K) // tile + N_EXPERTS
    rows = n_tiles * tile
    token = jnp.arange(t * TOP_K, dtype=jnp.int32) // TOP_K
    w_bits = lax.bitcast_convert_type(w_flat, jnp.int32)
    table = jnp.zeros((rows, 2), jnp.int32).at[pos].set(jnp.stack([token, w_bits], axis=1))
    row_token = table[:, 0]
    row_w = lax.bitcast_convert_type(table[:, 1], F32)
    start = jnp.arange(n_tiles, dtype=jnp.int32) * tile
    valid = start < ends[-1]
    expert = jnp.minimum(jnp.sum((start[:, None] >= ends[None, :]).astype(jnp.int32), axis=1), N_EXPERTS - 1)
    last_valid = jnp.max(jnp.where(valid, expert, 0))
    expert = jnp.where(valid, expert, last_valid)
    changed = jnp.concatenate([jnp.ones((1,), bool), expert[1:] != expert[:-1]])
    is_start = changed & valid
    idx = jnp.arange(n_tiles, dtype=jnp.int32)
    later_start = jnp.where(is_start[None, :] & (idx[None, :] > idx[:, None]), idx[None, :], n_tiles)
    nxt = jnp.min(later_start, axis=1)
    wrap = (nxt == n_tiles).astype(jnp.int32)
    next_expert = expert[jnp.where(nxt == n_tiles, 0, nxt)]
    tiles = (expert, is_start.astype(jnp.int32), valid.astype(jnp.int32), next_expert, wrap)
    return row_token, row_w, pos.astype(jnp.int32), tiles


def _row_copy(src_hbm, dst, sem, src_row, dst_row):
    return pltpu.make_async_copy(src_hbm.at[pl.ds(src_row, 1)], dst.at[pl.ds(dst_row, 1)], sem)


DMA_LOOP_UNROLL = 8


def _gather_rows_body(tok_ref, h_hbm, o_ref, buf, sems):
    i = pl.program_id(0)
    gt = buf.shape[1]
    slot = i % 2

    def issue(step, s):
        def body(r, carry):
            _row_copy(h_hbm, buf.at[s], sems.at[s], tok_ref[step * gt + r], r).start()
            return carry
        lax.fori_loop(0, gt, body, 0, unroll=DMA_LOOP_UNROLL)

    def drain(s):
        def body(r, carry):
            _row_copy(h_hbm, buf.at[s], sems.at[s], 0, r).wait()
            return carry
        lax.fori_loop(0, gt, body, 0, unroll=DMA_LOOP_UNROLL)

    @pl.when(i == 0)
    def _():
        issue(0, 0)

    @pl.when(i + 1 < pl.num_programs(0))
    def _():
        issue(i + 1, 1 - slot)

    drain(slot)
    o_ref[...] = buf[slot].astype(o_ref.dtype)


def _gather_rows(h, row_token, gt=256):
    rows = row_token.shape[0]
    d = h.shape[1]
    vm = 4 * _nbytes((gt, d), F32)
    return pl.pallas_call(
        _gather_rows_body,
        grid_spec=pltpu.PrefetchScalarGridSpec(
            num_scalar_prefetch=1,
            grid=(rows // gt,),
            in_specs=[pl.BlockSpec(memory_space=pl.ANY)],
            out_specs=pl.BlockSpec((gt, d), lambda i, tok: (i, 0)),
            scratch_shapes=[pltpu.VMEM((2, gt, d), F32), pltpu.SemaphoreType.DMA((2,))],
        ),
        out_shape=jax.ShapeDtypeStruct((rows, d), BF16),
        compiler_params=_cparams(("arbitrary",), vm),
        name="moe_gather_rows",
    )(row_token, h)


def _group_weights(tiles, w_hbms, w_f32s, w_bf16s, sems):
    te_ref, ts_ref, _, ne_ref, wrap_ref = tiles
    j, i = pl.program_id(0), pl.program_id(1)
    tn = w_f32s[0].shape[1]

    def copies(expert, col_block):
        col = pl.multiple_of(col_block * tn, tn)
        return [pltpu.make_async_copy(w.at[expert, :, pl.ds(col, tn)], buf, sems.at[n])
                for n, (w, buf) in enumerate(zip(w_hbms, w_f32s))]

    @pl.when(ts_ref[i] == 1)
    def _():
        @pl.when((j == 0) & (i == 0))
        def _():
            for cp in copies(te_ref[i], j):
                cp.start()

        for cp in copies(te_ref[i], j):
            cp.wait()
        for src, dst in zip(w_f32s, w_bf16s):
            dst[...] = src[...].astype(BF16)
        nj = j + wrap_ref[i]

        @pl.when(nj < pl.num_programs(0))
        def _():
            for cp in copies(ne_ref[i], nj):
                cp.start()


def _moe_up_body(te, ts, tv, ne, wrap, x_ref, wg_hbm, wu_hbm, o_ref, wg32, wu32, wg_bf, wu_bf, sems):
    i = pl.program_id(1)
    _group_weights((te, ts, tv, ne, wrap), (wg_hbm, wu_hbm), (wg32, wu32), (wg_bf, wu_bf), sems)

    @pl.when(tv[i] == 1)
    def _():
        x = x_ref[...]
        g = jnp.dot(x, wg_bf[...], preferred_element_type=F32)
        u = jnp.dot(x, wu_bf[...], preferred_element_type=F32)
        o_ref[...] = (g * jax.nn.sigmoid(g) * u).astype(o_ref.dtype)

    @pl.when(tv[i] == 0)
    def _():
        o_ref[...] = jnp.zeros_like(o_ref)


def _moe_up(xs, wg, wu, tiles, tile, tn=1024):
    rows, d = xs.shape
    f = wg.shape[2]
    vm = 2 * _nbytes((d, tn), F32) + 2 * _nbytes((d, tn), BF16) + 2 * _nbytes((tile, d), BF16) + 6 * _nbytes((tile, tn), F32)
    return pl.pallas_call(
        _moe_up_body,
        grid_spec=pltpu.PrefetchScalarGridSpec(
            num_scalar_prefetch=len(tiles),
            grid=(f // tn, rows // tile),
            in_specs=[pl.BlockSpec((tile, d), lambda j, i, *_: (i, 0)),
                      pl.BlockSpec(memory_space=pl.ANY), pl.BlockSpec(memory_space=pl.ANY)],
            out_specs=pl.BlockSpec((tile, tn), lambda j, i, *_: (i, j)),
            scratch_shapes=[pltpu.VMEM((d, tn), F32), pltpu.VMEM((d, tn), F32),
                            pltpu.VMEM((d, tn), BF16), pltpu.VMEM((d, tn), BF16),
                            pltpu.SemaphoreType.DMA((2,))],
        ),
        out_shape=jax.ShapeDtypeStruct((rows, f), BF16),
        compiler_params=_cparams(("arbitrary", "arbitrary"), vm),
        name="moe_up",
    )(*tiles, xs, wg, wu)


def _moe_down_body(te, ts, tv, ne, wrap, a_ref, wd_hbm, rw_ref, o_ref, wd32, wd_bf, sems):
    i = pl.program_id(1)
    _group_weights((te, ts, tv, ne, wrap), (wd_hbm,), (wd32,), (wd_bf,), sems)

    @pl.when(tv[i] == 1)
    def _():
        o_ref[...] = jnp.dot(a_ref[...], wd_bf[...], preferred_element_type=F32) * rw_ref[...]

    @pl.when(tv[i] == 0)
    def _():
        o_ref[...] = jnp.zeros_like(o_ref)


def _moe_down(act, wd, row_w, tiles, tile, tn=512):
    rows, f = act.shape
    d = wd.shape[2]
    vm = _nbytes((f, tn), F32) + _nbytes((f, tn), BF16) + 2 * _nbytes((tile, f), BF16) + 4 * _nbytes((tile, tn), F32)
    return pl.pallas_call(
        _moe_down_body,
        grid_spec=pltpu.PrefetchScalarGridSpec(
            num_scalar_prefetch=len(tiles),
            grid=(d // tn, rows // tile),
            in_specs=[pl.BlockSpec((tile, f), lambda j, i, *_: (i, 0)),
                      pl.BlockSpec(memory_space=pl.ANY),
                      pl.BlockSpec((tile, 1), lambda j, i, *_: (i, 0))],
            out_specs=pl.BlockSpec((tile, tn), lambda j, i, *_: (i, j)),
            scratch_shapes=[pltpu.VMEM((f, tn), F32), pltpu.VMEM((f, tn), BF16), pltpu.SemaphoreType.DMA((1,))],
        ),
        out_shape=jax.ShapeDtypeStruct((rows, d), F32),
        compiler_params=_cparams(("arbitrary", "arbitrary"), vm),
        name="moe_down",
    )(*tiles, act, wd, row_w.reshape(rows, 1))


def _moe_combine_body(pos_ref, ys_hbm, x_ref, gate_ref, g_ref, o_ref, buf, sems, *, final_norm):
    i = pl.program_id(0)
    gt = x_ref.shape[0]
    slot = i % 2

    def issue(step, s):
        def body(r, carry):
            for k in range(TOP_K):
                _row_copy(ys_hbm, buf.at[s, k], sems.at[s], pos_ref[TOP_K * (step * gt + r) + k], r).start()
            return carry
        lax.fori_loop(0, gt, body, 0, unroll=DMA_LOOP_UNROLL)

    def drain(s):
        def body(r, carry):
            for k in range(TOP_K):
                _row_copy(ys_hbm, buf.at[s, k], sems.at[s], 0, r).wait()
            return carry
        lax.fori_loop(0, gt, body, 0, unroll=DMA_LOOP_UNROLL)

    @pl.when(i == 0)
    def _():
        issue(0, 0)

    @pl.when(i + 1 < pl.num_programs(0))
    def _():
        issue(i + 1, 1 - slot)

    drain(slot)
    y = buf[slot, 0]
    for k in range(1, TOP_K):
        y = y + buf[slot, k]
    x = x_ref[...] + gate_ref[...] * y
    if final_norm:
        x = x * lax.rsqrt(jnp.mean(x * x, axis=-1, keepdims=True) + EPS) * g_ref[...]
    o_ref[...] = x


def _moe_combine(ys, pos, x, gate, norm_g, final_norm, gt=128):
    t, d = x.shape
    gt = min(gt, t)
    row = pl.BlockSpec((1, d), lambda i, p: (0, 0))
    tilespec = pl.BlockSpec((gt, d), lambda i, p: (i, 0))
    vm = (2 * TOP_K + 6) * _nbytes((gt, d), F32)
    return pl.pallas_call(
        functools.partial(_moe_combine_body, final_norm=final_norm),
        grid_spec=pltpu.PrefetchScalarGridSpec(
            num_scalar_prefetch=1,
            grid=(t // gt,),
            in_specs=[pl.BlockSpec(memory_space=pl.ANY), tilespec, row, row],
            out_specs=tilespec,
            scratch_shapes=[pltpu.VMEM((2, TOP_K, gt, d), F32), pltpu.SemaphoreType.DMA((2,))],
        ),
        out_shape=jax.ShapeDtypeStruct((t, d), F32),
        compiler_params=_cparams(("arbitrary",), vm),
        name="moe_combine",
    )(pos, ys, x, gate.reshape(1, d), norm_g.reshape(1, d))


MOE_ROW_TILE = 512


def _moe_ffn(h32, x, gate, p, norm_g, final_norm):
    route = _router(h32, p["router_w"])
    row_token, row_w, pos, tiles = _routing_tables(route, MOE_ROW_TILE)
    xs = _gather_rows(h32, row_token)
    act = _moe_up(xs, p["exp_w_gate"], p["exp_w_up"], tiles, MOE_ROW_TILE)
    ys = _moe_down(act, p["exp_w_down"], row_w, tiles, MOE_ROW_TILE)
    return _moe_combine(ys, pos, x, gate, norm_g, final_norm)


def _silu(v):
    return v * jax.nn.sigmoid(v)


def _in_weights(w_in_all, layer, d):
    qk, vw, r = d // 2, d, GLA_GATE_RANK
    col_v = qk
    col_a = col_v + vw
    col_q = col_a + 2 * r
    col_g = col_q + qk
    col_hy = col_g + vw
    col_gate = col_hy + 3 * d
    return dict(wt=jnp.swapaxes(w_in_all, 1, 2), layer=layer, k=(0, qk), v=(col_v, vw), a=(col_a, 2 * r),
                q=(col_q, qk), g=(col_g, vw), hy=(col_hy, 3 * d), gate=(col_gate, 2 * d))


def _proj_body(x_ref, wt_hbm, o_ref, w32, wbf, sem, *, layer, row0, epi):
    j, i = pl.program_id(0), pl.program_id(1)
    rows = w32.shape[0]

    def fetch(block):
        start = pl.multiple_of(row0 + block * rows, SUBLANES)
        return pltpu.make_async_copy(wt_hbm.at[layer, pl.ds(start, rows), :], w32, sem)

    @pl.when(i == 0)
    def _():
        @pl.when(j == 0)
        def _():
            fetch(j).start()

        fetch(j).wait()
        if rows == wbf.shape[0]:
            wbf[...] = w32[...].astype(BF16)
        else:
            wbf[...] = jnp.zeros_like(wbf)
            wbf[:rows, :] = w32[...].astype(BF16)

        @pl.when(j + 1 < pl.num_programs(0))
        def _():
            fetch(j + 1).start()

    o_ref[...] = epi(_dot_nt(x_ref[...], wbf[...])).astype(o_ref.dtype)


def _proj(h, wi, group, epi, out_dtype, tm=1024, tn=1024):
    m, d = h.shape
    col, width = wi[group]
    rows = min(tn, width)
    tn = max(rows, LANES)
    tm = min(tm, m)
    assert m % tm == 0 and width % rows == 0 and col % SUBLANES == 0 and rows % SUBLANES == 0
    n_out = width // rows * tn
    vm = _nbytes((rows, d), F32) + _nbytes((tn, d), BF16) + 2 * _nbytes((tm, d), BF16) + 4 * _nbytes((tm, tn), F32)
    out, = pl.pallas_call(
        functools.partial(_proj_body, layer=wi["layer"], row0=col, epi=epi),
        grid=(width // rows, m // tm),
        in_specs=[pl.BlockSpec((tm, d), lambda j, i: (i, 0)), pl.BlockSpec(memory_space=pl.ANY)],
        out_specs=[pl.BlockSpec((tm, tn), lambda j, i: (i, j))],
        out_shape=[jax.ShapeDtypeStruct((m, n_out), out_dtype)],
        scratch_shapes=[pltpu.VMEM((rows, d), F32), pltpu.VMEM((tn, d), BF16), pltpu.SemaphoreType.DMA(())],
        compiler_params=_cparams(("arbitrary", "arbitrary"), vm),
        name="proj_" + group,
    )(h, wi["wt"])
    return out


def _gate_matrices(p, d):
    r = GLA_GATE_RANK
    qk = d // 2
    awf = jnp.zeros((LANES, qk), F32).at[:r].set(p["gla_aw_f"])
    awb = jnp.zeros((LANES, qk), F32).at[r:2 * r].set(p["gla_aw_b"])
    return awf, awb, p["gla_ab_f"].reshape(1, qk), p["gla_ab_b"].reshape(1, qk)


GLA_CHUNK = 256


def _gla_states_only(h, wi, p, d):
    ident = lambda acc: acc
    k = _proj(h, wi, "k", ident, BF16)
    v = _proj(h, wi, "v", ident, BF16)
    a = _proj(h, wi, "a", ident, F32)
    ef, eb = _gla_decay(a, *_gate_matrices(p, d), chunk=min(GLA_CHUNK, h.shape[0]))
    dk, dv = d // 2 // GLA_HEADS, d // GLA_HEADS
    zero = jnp.zeros((GLA_HEADS, dv, dk), F32)
    _, s_f = _gla_scan(k, k, v, ef, zero, fwd=True, chunk=GLA_CHUNK)
    _, s_b = _gla_scan(k, k, v, eb, zero, fwd=False, chunk=GLA_CHUNK)
    return s_f, s_b


def _token_mixer(x, h, mod_gate, s0_f, s0_b, wi, p, d):
    l = h.shape[0]
    dk = d // 2 // GLA_HEADS
    ident = lambda acc: acc
    k = _proj(h, wi, "k", ident, BF16)
    v = _proj(h, wi, "v", ident, BF16)
    a = _proj(h, wi, "a", ident, F32)
    q = _proj(h, wi, "q", lambda acc: acc * (dk ** -0.5), BF16)
    sg = _proj(h, wi, "g", _silu, BF16)
    hy = _proj(h, wi, "hy", ident, BF16)
    gates = _proj(h, wi, "gate", jax.nn.sigmoid, BF16)

    ef, eb = _gla_decay(a, *_gate_matrices(p, d), chunk=min(GLA_CHUNK, l))
    o_f, s_f = _gla_scan(q, k, v, ef, s0_f, fwd=True, chunk=GLA_CHUNK)
    o_gla, s_b = _gla_scan(q, k, v, eb, s0_b, fwd=False, chunk=GLA_CHUNK, combine=(o_f, sg, p["gla_norm_g"]))

    x0, z = _hyena_shortconv(hy, p["hy_conv_w"], p["hy_conv_b"])
    hf, hg, ss = _hyena_filter(l, p)
    o_hy = _hyena_longconv(x0, z, hf, hg, ss, p["hy_bias"])

    layer = wi["layer"]
    t1, = _mm(o_hy, [wi["w_up_hy"]], lambda accs, ex: [ex[0].astype(F32) * accs[0]], (F32,),
              extras=[(gates, "tile", 0)], w_layer=layer, name="up_hy")
    merged, = _mm(o_gla, [wi["w_up_gla"]], lambda accs, ex: [ex[1] + ex[0].astype(F32) * accs[0]], (BF16,),
                  extras=[(gates, "tile", d), (t1, "tile", 0)], w_layer=layer, name="up_gla_merge")
    x_new, = _mm(merged, [wi["w_out"]], lambda accs, ex: [ex[0] + ex[1] * accs[0]], (F32,),
                 extras=[(x, "tile", 0), (mod_gate.reshape(1, d), "row", 0)], w_layer=layer, name="out_proj")
    return x_new, s_f, s_b


def _dense_ffn(x, h, mod_gate, wg, wu, wd):
    d, f = wg.shape
    fp = -(-f // 512) * 512
    wg_b = _pad2(wg.astype(BF16), d, fp)
    wu_b = _pad2(wu.astype(BF16), d, fp)
    wd_b = _pad2(wd.astype(BF16), fp, d)
    act, = _mm(h, [wg_b, wu_b], lambda accs, ex: [_silu(accs[0]) * accs[1]], (BF16,), tn=512, name="ffn_up")
    x_new, = _mm(act, [wd_b], lambda accs, ex: [ex[0] + ex[1] * accs[0]], (F32,), tm=512,
                 extras=[(x, "tile", 0), (mod_gate.reshape(1, d), "row", 0)], name="ffn_down")
    return x_new


def kernel(x, c, ctx, c_ctx, ada_w, ada_b, norm_mix_g, norm_ffn_g, w_in, hy_conv_w, hy_conv_b, hy_w1, hy_b1, hy_w2, hy_b2, hy_w3, hy_b3, hy_w4, hy_freq, hy_bias, gla_aw_f, gla_ab_f, gla_aw_b, gla_ab_b, gla_norm_g, w_up_hy, w_up_gla, w_out, ffn_w_gate, ffn_w_up, ffn_w_down, router_w, exp_w_gate, exp_w_up, exp_w_down, final_norm_g):
    assert x.shape[0] == 1 and c.shape[0] == 1, "batch size 1 only"
    depth, d = norm_mix_g.shape
    x_lat, x_ctx = x[0], ctx[0]
    mods = _ada_modulation(jnp.concatenate([c, c_ctx.reshape(1, d)], axis=0), ada_w, ada_b)
    dk, dv = d // 2 // GLA_HEADS, d // GLA_HEADS
    per_layer = dict(hy_conv_w=hy_conv_w, hy_conv_b=hy_conv_b, hy_w1=hy_w1, hy_b1=hy_b1, hy_w2=hy_w2, hy_b2=hy_b2,
                     hy_w3=hy_w3, hy_b3=hy_b3, hy_w4=hy_w4, hy_freq=hy_freq, hy_bias=hy_bias,
                     gla_aw_f=gla_aw_f, gla_ab_f=gla_ab_f, gla_aw_b=gla_aw_b, gla_ab_b=gla_ab_b,
                     gla_norm_g=gla_norm_g)
    for l in range(depth):
        last = l == depth - 1
        p = {name: arr[l] for name, arr in per_layer.items()}
        wi = dict(_in_weights(w_in, l, d), w_up_hy=w_up_hy, w_up_gla=w_up_gla, w_out=w_out)
        lat = [mods[l, 0, i * d:(i + 1) * d] for i in range(N_ADA)]
        cxm = [mods[l, 1, i * d:(i + 1) * d] for i in range(N_ADA)]

        h_ctx, = _modnorm(x_ctx, norm_mix_g[l], cxm[0], cxm[1], (BF16,))
        if last:
            s_f, s_b = _gla_states_only(h_ctx, wi, p, d)
        else:
            zero = jnp.zeros((GLA_HEADS, dv, dk), F32)
            x_ctx, s_f, s_b = _token_mixer(x_ctx, h_ctx, cxm[2], zero, zero, wi, p, d)
        h_lat, = _modnorm(x_lat, norm_mix_g[l], lat[0], lat[1], (BF16,))
        x_lat, _, _ = _token_mixer(x_lat, h_lat, lat[2], s_f, s_b, wi, p, d)

        i = l // 2
        if l % 2 == 0:
            h2, = _modnorm(x_lat, norm_ffn_g[l], lat[3], lat[4], (BF16,))
            x_lat = _dense_ffn(x_lat, h2, lat[5], ffn_w_gate[i], ffn_w_up[i], ffn_w_down[i])
            if not last:
                h2c, = _modnorm(x_ctx, norm_ffn_g[l], cxm[3], cxm[4], (BF16,))
                x_ctx = _dense_ffn(x_ctx, h2c, cxm[5], ffn_w_gate[i], ffn_w_up[i], ffn_w_down[i])
        else:
            pm = dict(router_w=router_w[i], exp_w_gate=exp_w_gate[i], exp_w_up=exp_w_up[i], exp_w_down=exp_w_down[i])
            h2, = _modnorm(x_lat, norm_ffn_g[l], lat[3], lat[4], (F32,))
            x_lat = _moe_ffn(h2, x_lat, lat[5], pm, final_norm_g, final_norm=last)
            if not last:
                h2c, = _modnorm(x_ctx, norm_ffn_g[l], cxm[3], cxm[4], (F32,))
                x_ctx = _moe_ffn(h2c, x_ctx, cxm[5], pm, final_norm_g, final_norm=False)
    if depth % 2 == 1:
        x_lat, = _modnorm(x_lat, final_norm_g, jnp.zeros((d,), F32), jnp.zeros((d,), F32), (F32,))
    return x_lat[None]
```

```python
import functools
import math

import numpy as np
import jax
import jax.numpy as jnp
from jax import lax
from jax.experimental import pallas as pl
from jax.experimental.pallas import tpu as pltpu

F32 = jnp.float32
BF16 = jnp.bfloat16
HIGHEST = lax.Precision.HIGHEST

EPS = 1e-6
N_ADA = 6
LANES = 128
SUBLANES = 8
VMEM_LIMIT_CAP = 60 * 1024 * 1024

GLA_HEADS = 4
GLA_GATE_RANK = 16
GLA_GATE_TEMP = 16.0
HY_SHORT = 3
HY_EMB_BANDS = 16
HY_FILTER_HIDDEN = 64
HY_DECAY_TARGET = 1e-2
HY_FAST_DECAY = 0.3
HY_SLOW_DECAY = 1.5
N_EXPERTS = 8
TOP_K = 2


def _cparams(sem, vmem_bytes):
    limit = int(min(max(vmem_bytes * 5 // 4 + (2 << 20), 16 << 20), VMEM_LIMIT_CAP))
    return pltpu.CompilerParams(dimension_semantics=sem, vmem_limit_bytes=limit)


def _nbytes(shape, dtype):
    return int(np.prod(shape)) * jnp.dtype(dtype).itemsize


def _ada_body(c_ref, w_ref, b_ref, o_ref):
    tn = o_ref.shape[-1]
    d = c_ref.shape[1]
    rows = []
    for r in range(2):
        s = c_ref[r]
        s = s * jax.nn.sigmoid(s)
        chunks = []
        for n0 in range(0, tn, LANES):
            p = w_ref[:, n0:n0 + LANES] * s
            acc = p.reshape(d // SUBLANES, SUBLANES, LANES).sum(axis=0)
            chunks.append(acc.sum(axis=0, keepdims=True))
        rows.append(jnp.concatenate(chunks, axis=1))
    o_ref[...] = jnp.concatenate(rows, axis=0) + b_ref[...]


def _ada_modulation(cond2, ada_w, ada_b):
    depth, d, n = ada_w.shape
    tn = 1536 if n % 1536 == 0 else LANES
    cb = jnp.broadcast_to(cond2[:, :, None], (2, d, LANES))
    vm = 2 * _nbytes((d, tn), F32) + 2 * _nbytes((2, d, LANES), F32)
    return pl.pallas_call(
        _ada_body,
        grid=(depth, n // tn),
        in_specs=[
            pl.BlockSpec((2, d, LANES), lambda l, j: (0, 0, 0)),
            pl.BlockSpec((None, d, tn), lambda l, j: (l, 0, j)),
            pl.BlockSpec((None, 1, tn), lambda l, j: (l, 0, j)),
        ],
        out_specs=pl.BlockSpec((None, 2, tn), lambda l, j: (l, 0, j)),
        out_shape=jax.ShapeDtypeStruct((depth, 2, n), F32),
        compiler_params=_cparams(("parallel", "parallel"), vm),
        name="ada_modulation",
    )(cb, ada_w, ada_b.reshape(depth, 1, n))


def _modnorm_body(x_ref, g_ref, sh_ref, sc_ref, *o_refs):
    x = x_ref[...]
    ms = jnp.mean(x * x, axis=-1, keepdims=True)
    y = x * lax.rsqrt(ms + EPS) * g_ref[...]
    y = y * (1.0 + sc_ref[...]) + sh_ref[...]
    half = y.shape[1] // 2
    for o in o_refs:
        if o.dtype == jnp.uint32:
            o[...] = _pack_ri(y[:, :half], y[:, half:])
        else:
            o[...] = y.astype(o.dtype)


def _modnorm(x, g, shift, scale, out_dtypes, tm=256):
    m, d = x.shape
    tm = min(tm, m)
    row = pl.BlockSpec((1, d), lambda i: (0, 0))
    tile = pl.BlockSpec((tm, d), lambda i: (i, 0))
    width = lambda dt: d // 2 if dt == jnp.uint32 else d
    vm = 2 * _nbytes((tm, d), F32) * (1 + len(out_dtypes))
    outs = pl.pallas_call(
        _modnorm_body,
        grid=(m // tm,),
        in_specs=[tile, row, row, row],
        out_specs=[pl.BlockSpec((tm, width(dt)), lambda i: (i, 0)) for dt in out_dtypes],
        out_shape=[jax.ShapeDtypeStruct((m, width(dt)), dt) for dt in out_dtypes],
        compiler_params=_cparams(("parallel",), vm),
        name="modnorm",
    )(x, g.reshape(1, d), shift.reshape(1, d), scale.reshape(1, d))
    return outs


def _mm_body(*refs, n_w, n_e, n_o, epi, cast_w):
    x_ref = refs[0]
    w_refs = refs[1:1 + n_w]
    e_refs = refs[1 + n_w:1 + n_w + n_e]
    o_refs = refs[1 + n_w + n_e:1 + n_w + n_e + n_o]
    w_bf16 = refs[1 + n_w + n_e + n_o:]
    x = x_ref[...].astype(BF16)
    if cast_w:
        @pl.when(pl.program_id(1) == 0)
        def _():
            for src, dst in zip(w_refs, w_bf16):
                dst[...] = src[...].astype(BF16)
        w_refs = w_bf16
    accs = [jnp.dot(x, w[...], preferred_element_type=F32) for w in w_refs]
    outs = epi(accs, [e[...] for e in e_refs])
    for o, v in zip(o_refs, outs):
        o[...] = v.astype(o.dtype)


def _mm(x, ws, epi, out_dtypes, extras=(), w_layer=None, tm=1024, tn=1024, name="mm"):
    m, k = x.shape
    n = ws[0].shape[-1]
    tm = min(tm, m)
    tn = min(tn, n)
    assert m % tm == 0 and n % tn == 0, (m, tm, n, tn)
    cast_w = ws[0].dtype == F32
    in_specs = [pl.BlockSpec((tm, k), lambda j, i: (i, 0))]
    if w_layer is None:
        in_specs += [pl.BlockSpec((k, tn), lambda j, i: (0, j))] * len(ws)
    else:
        in_specs += [pl.BlockSpec((None, k, tn), lambda j, i: (w_layer, 0, j))] * len(ws)
    vm = 2 * _nbytes((tm, k), x.dtype) + 2 * len(ws) * _nbytes((k, tn), ws[0].dtype)
    for arr, kind, col in extras:
        assert col % tn == 0
        if kind == "tile":
            in_specs.append(pl.BlockSpec((tm, tn), lambda j, i, c=col // tn: (i, j + c)))
            vm += 2 * _nbytes((tm, tn), arr.dtype)
        else:
            in_specs.append(pl.BlockSpec((1, tn), lambda j, i, c=col // tn: (0, j + c)))
    vm += sum(2 * _nbytes((tm, tn), dt) for dt in out_dtypes) + (1 + len(ws)) * _nbytes((tm, tn), F32)
    scratch = [pltpu.VMEM((k, tn), BF16)] * len(ws) if cast_w else []
    vm += len(scratch) * _nbytes((k, tn), BF16)
    return pl.pallas_call(
        functools.partial(_mm_body, n_w=len(ws), n_e=len(extras), n_o=len(out_dtypes), epi=epi, cast_w=cast_w),
        grid=(n // tn, m // tm),
        in_specs=in_specs,
        out_specs=[pl.BlockSpec((tm, tn), lambda j, i: (i, j))] * len(out_dtypes),
        out_shape=[jax.ShapeDtypeStruct((m, n), dt) for dt in out_dtypes],
        scratch_shapes=scratch,
        compiler_params=_cparams(("parallel", "arbitrary" if cast_w else "parallel"), vm),
        name=name,
    )(x, *ws, *[a for a, _, _ in extras])


def _log_sigmoid(z):
    return jnp.minimum(z, 0.0) - jnp.log1p(jnp.exp(-jnp.abs(z)))


def _split_bf16(x, terms):
    parts = []
    for _ in range(terms):
        p = x.astype(BF16)
        parts.append(p)
        x = x - p.astype(F32)
    return parts


def _gla_decay_body(a_ref, awf_ref, awb_ref, abf_ref, abb_ref, ef_ref, eb_ref):
    c = a_ref.shape[0]
    a = a_ref[...]
    r = lax.broadcasted_iota(jnp.int32, (c, c), 0)
    s = lax.broadcasted_iota(jnp.int32, (c, c), 1)
    lower = (s <= r).astype(BF16)
    upper = (s >= r).astype(BF16)
    a_parts = _split_bf16(a, 2)

    def gate_logits(w_ref, b_ref):
        w_hi, w_lo = _split_bf16(w_ref[...], 2)
        z = jnp.dot(a_parts[0], w_hi, preferred_element_type=F32)
        z = z + jnp.dot(a_parts[1], w_hi, preferred_element_type=F32)
        z = z + jnp.dot(a_parts[0], w_lo, preferred_element_type=F32)
        return z + b_ref[...]

    def chunk_sums(tri, g):
        return sum(jnp.dot(tri, part, preferred_element_type=F32) for part in _split_bf16(g, 3))

    gf = _log_sigmoid(gate_logits(awf_ref, abf_ref)) * (1.0 / GLA_GATE_TEMP)
    gb = _log_sigmoid(gate_logits(awb_ref, abb_ref)) * (1.0 / GLA_GATE_TEMP)
    ef_ref[...] = chunk_sums(lower, gf)
    eb_ref[...] = chunk_sums(upper, gb)


def _gla_decay(a, awf, awb, abf, abb, chunk, tn=512):
    l = a.shape[0]
    n = awf.shape[1]
    tn = min(tn, n)
    col = pl.BlockSpec((a.shape[1], tn), lambda i, j: (0, j))
    row = pl.BlockSpec((1, tn), lambda i, j: (0, j))
    out = pl.BlockSpec((chunk, tn), lambda i, j: (i, j))
    vm = 4 * _nbytes((chunk, tn), F32) * 3 + 4 * _nbytes((a.shape[1], tn), F32)
    return pl.pallas_call(
        _gla_decay_body,
        grid=(l // chunk, n // tn),
        in_specs=[pl.BlockSpec((chunk, a.shape[1]), lambda i, j: (i, 0)), col, col, row, row],
        out_specs=[out, out],
        out_shape=[jax.ShapeDtypeStruct((l, n), F32)] * 2,
        compiler_params=_cparams(("parallel", "parallel"), vm),
        name="gla_decay",
    )(a, awf, awb, abf, abb)


def _dot_nt(a, b):
    return lax.dot_general(a, b, (((1,), (1,)), ((), ())), preferred_element_type=F32)


def _dot_tn(a, b):
    return lax.dot_general(a, b, (((0,), (0,)), ((), ())), preferred_element_type=F32)


def _bcast_rows(e, group, row):
    c, w = e.shape
    e3 = e.reshape(c // group, group, w)
    return jnp.broadcast_to(e3[:, row:row + 1, :], e3.shape).reshape(c, w)


def _gla_masks(c, base, fwd):
    i = lax.broadcasted_iota(jnp.int32, (c, c), 0)
    j = lax.broadcasted_iota(jnp.int32, (c, c), 1)
    sh = int(math.log2(base))
    order = (j <= i) if fwd else (j >= i)
    masks = [((i >> sh) == (j >> sh)) & order]
    s = base
    while 2 * s <= c:
        sh += 1
        masks.append((i >> sh) == (j >> sh))
        s *= 2
    return masks


def _gla_chunk_head(q, k, v, e, st, masks, *, fwd, base):
    c, dk = q.shape
    row = lax.broadcasted_iota(jnp.int32, (c, 1), 0)
    d0 = e - _bcast_rows(e, base, base // 2 - 1 if fwd else base // 2)
    q0 = (q * jnp.exp(d0)).astype(BF16)
    k0 = (k * jnp.exp(-d0)).astype(BF16)
    att = jnp.where(masks[0], _dot_nt(q0, k0), 0.0)
    s, lvl = base, 1
    while 2 * s <= c:
        d = e - _bcast_rows(e, 2 * s, s - 1 if fwd else s)
        later = ((row >> int(math.log2(s))) & 1) == (1 if fwd else 0)
        x = jnp.exp(jnp.where(later, d, -d))
        ql = jnp.where(later, q * x, 0.0).astype(BF16)
        kl = jnp.where(later, 0.0, k * x).astype(BF16)
        att = att + jnp.where(masks[lvl], _dot_nt(ql, kl), 0.0)
        s *= 2
        lvl += 1
    e_edge = e[c - 1:c] if fwd else e[0:1]
    qs = (q * jnp.exp(e)).astype(BF16)
    ks = (k * jnp.exp(e_edge - e)).astype(BF16)
    o = jnp.dot(att.astype(BF16), v, preferred_element_type=F32) + _dot_nt(qs, st.astype(BF16))
    st_new = st * jnp.exp(e_edge) + _dot_tn(v, ks)
    return o, st_new


def _gla_scan_body(*refs, fwd, base, combine):
    if combine:
        q_ref, k_ref, v_ref, e_ref, s0_ref, of_ref, sg_ref, gn_ref, o_ref, sfin_ref, st_ref = refs
    else:
        q_ref, k_ref, v_ref, e_ref, s0_ref, o_ref, sfin_ref, st_ref = refs
    step = pl.program_id(0)
    c = q_ref.shape[0]
    dk = q_ref.shape[1] // GLA_HEADS
    dv = v_ref.shape[1] // GLA_HEADS

    @pl.when(step == 0)
    def _():
        st_ref[...] = s0_ref[...]

    masks = _gla_masks(c, base, fwd)
    for h in range(GLA_HEADS):
        ks = slice(h * dk, (h + 1) * dk)
        vs = slice(h * dv, (h + 1) * dv)
        o, st_new = _gla_chunk_head(
            q_ref[:, ks].astype(F32), k_ref[:, ks].astype(F32), v_ref[:, vs], e_ref[:, ks], st_ref[h],
            masks, fwd=fwd, base=base)
        st_ref[h] = st_new
        if combine:
            t = o + of_ref[:, vs]
            t = t * lax.rsqrt(jnp.mean(t * t, axis=-1, keepdims=True) + EPS) * gn_ref[...]
            o_ref[:, vs] = (t * sg_ref[:, vs].astype(F32)).astype(o_ref.dtype)
        else:
            o_ref[:, vs] = o.astype(o_ref.dtype)

    @pl.when(step == pl.num_programs(0) - 1)
    def _():
        sfin_ref[...] = st_ref[...]


def _gla_scan(q, k, v, e, s0, *, fwd, chunk, base=32, combine=None):
    l, hdk = q.shape
    hdv = v.shape[1]
    dk, dv = hdk // GLA_HEADS, hdv // GLA_HEADS
    chunk = min(chunk, l)
    n = l // chunk
    idx = (lambda i: (i, 0)) if fwd else (lambda i: (n - 1 - i, 0))
    st_spec = pl.BlockSpec((GLA_HEADS, dv, dk), lambda i: (0, 0, 0))
    in_specs = [pl.BlockSpec((chunk, hdk), idx), pl.BlockSpec((chunk, hdk), idx),
                pl.BlockSpec((chunk, hdv), idx), pl.BlockSpec((chunk, hdk), idx), st_spec]
    args = [q, k, v, e, s0]
    vm = 2 * (2 * _nbytes((chunk, hdk), BF16) + _nbytes((chunk, hdv), BF16) + _nbytes((chunk, hdk), F32))
    vm += 3 * _nbytes((GLA_HEADS, dv, dk), F32) * 2 + 2 * _nbytes((chunk, hdv), F32)
    if combine is not None:
        o_other, gate, norm_g = combine
        in_specs += [pl.BlockSpec((chunk, hdv), idx), pl.BlockSpec((chunk, hdv), idx),
                     pl.BlockSpec((1, dv), lambda i: (0, 0))]
        args += [o_other, gate, norm_g.reshape(1, dv)]
        vm += 2 * (_nbytes((chunk, hdv), F32) + _nbytes((chunk, hdv), BF16))
    vm += 24 * _nbytes((chunk, max(dk, chunk)), F32)
    return pl.pallas_call(
        functools.partial(_gla_scan_body, fwd=fwd, base=base, combine=combine is not None),
        grid=(n,),
        in_specs=in_specs,
        out_specs=[pl.BlockSpec((chunk, hdv), idx), st_spec],
        out_shape=[jax.ShapeDtypeStruct((l, hdv), BF16 if combine is not None else F32),
                   jax.ShapeDtypeStruct((GLA_HEADS, dv, dk), F32)],
        scratch_shapes=[pltpu.VMEM((GLA_HEADS, dv, dk), F32)],
        compiler_params=_cparams(("arbitrary",), vm),
        name="gla_scan_fwd" if fwd else "gla_scan_bwd",
    )(*args)


def _shortconv_body(u0, u1, u2, p0, p1, p2, n0, n1, n2, w0, w1, w2, b0, b1, b2, x0_ref, z_ref):
    i = pl.program_id(0)
    last = pl.num_programs(0) - 1
    tm = u0.shape[0]
    row = lax.broadcasted_iota(jnp.int32, (tm, 1), 0)

    def conv(u_ref, p_ref, n_ref, w_ref, b_ref):
        u = u_ref[...].astype(F32)
        halo = p_ref.shape[0]
        prev_row = jnp.where(i == 0, 0.0, p_ref[...].astype(F32)[halo - 1:halo, :])
        next_row = jnp.where(i == last, 0.0, n_ref[...].astype(F32)[0:1, :])
        before = jnp.where(row == 0, prev_row, pltpu.roll(u, 1, axis=0))
        after = jnp.where(row == tm - 1, next_row, pltpu.roll(u, tm - 1, axis=0))
        return b_ref[...] + before * w_ref[0:1, :] + u * w_ref[1:2, :] + after * w_ref[2:3, :]

    x0_ref[...] = conv(u0, p0, n0, w0, b0).astype(x0_ref.dtype)
    z_ref[...] = conv(u1, p1, n1, w1, b1) * conv(u2, p2, n2, w2, b2)


def _hyena_shortconv(hy, conv_w, conv_b, tm=512, cb=512):
    l, w3 = hy.shape
    w = w3 // 3
    tm = min(tm, l)
    cb = min(cb, w)
    nb = w // cb
    halo = SUBLANES * 4 // jnp.dtype(hy.dtype).itemsize
    hb = tm // halo
    n_halo = l // halo
    cur = [pl.BlockSpec((tm, cb), lambda i, j, g=g: (i, g * nb + j)) for g in range(3)]
    prv = [pl.BlockSpec((halo, cb), lambda i, j, g=g: (jnp.maximum(i * hb - 1, 0), g * nb + j)) for g in range(3)]
    nxt = [pl.BlockSpec((halo, cb), lambda i, j, g=g: (jnp.minimum((i + 1) * hb, n_halo - 1), g * nb + j))
           for g in range(3)]
    wsp = [pl.BlockSpec((HY_SHORT, cb), lambda i, j, g=g: (0, g * nb + j)) for g in range(3)]
    bsp = [pl.BlockSpec((1, cb), lambda i, j, g=g: (0, g * nb + j)) for g in range(3)]
    out = pl.BlockSpec((tm, cb), lambda i, j: (i, j))
    vm = 2 * 5 * _nbytes((tm, cb), F32) + 8 * _nbytes((tm, cb), F32)
    return pl.pallas_call(
        _shortconv_body,
        grid=(l // tm, nb),
        in_specs=cur + prv + nxt + wsp + bsp,
        out_specs=[out, out],
        out_shape=[jax.ShapeDtypeStruct((l, w), BF16), jax.ShapeDtypeStruct((l, w), F32)],
        compiler_params=_cparams(("parallel", "parallel"), vm),
        name="hyena_shortconv",
    )(hy, hy, hy, hy, hy, hy, hy, hy, hy, conv_w, conv_w, conv_w,
      conv_b.reshape(1, w3), conv_b.reshape(1, w3), conv_b.reshape(1, w3))


def _filter_body(w1_ref, b1_ref, w2_ref, b2_ref, w3_ref, b3_ref, fr_ref, w4f_ref, w4b_ref,
                 hf_ref, hg_ref, ss_ref, *, seq_len):
    i = pl.program_id(0)
    tr = hf_ref.shape[0]
    wdt = hf_ref.shape[1]
    half = tr // 2
    side_w = LANES // 2
    lane = lax.broadcasted_iota(jnp.int32, (1, LANES), 1)
    local = lane & (side_w - 1)
    row = lax.broadcasted_iota(jnp.int32, (half, 1), 0) + i * tr
    pos = (row + jnp.where(lane >= side_w, half, 0)).astype(F32)
    last_pos = float(max(seq_len - 1, 1))
    band = ((local - 1) & (HY_EMB_BANDS - 1)).astype(F32)
    bands = 1e-4 + band * ((HY_EMB_BANDS - 1 - 1e-4) / (HY_EMB_BANDS - 1))
    ang = ((2.0 * math.pi / seq_len) * pos) * bands
    trig = jnp.cos(ang + jnp.where(local > HY_EMB_BANDS, 0.5 * math.pi, 0.0))
    emb = jnp.where(local == 0, pos / last_pos, jnp.where(local <= 2 * HY_EMB_BANDS, trig, 0.0))
    fr = fr_ref[...]
    h = jnp.sin(fr * (jnp.dot(emb, w1_ref[...], preferred_element_type=F32, precision=HIGHEST) + b1_ref[...]))
    h = jnp.sin(fr * (jnp.dot(h, w2_ref[...], preferred_element_type=F32, precision=HIGHEST) + b2_ref[...]))
    h = jnp.sin(fr * (jnp.dot(h, w3_ref[...], preferred_element_type=F32, precision=HIGHEST) + b3_ref[...]))
    hb = h.astype(BF16)
    ch = lax.broadcasted_iota(jnp.int32, (1, wdt), 1).astype(F32)
    lo = math.log(HY_DECAY_TARGET) / HY_SLOW_DECAY
    hi = math.log(HY_DECAY_TARGET) / HY_FAST_DECAY
    deltas = jnp.abs(lo + ch * ((hi - lo) / (wdt - 1)))
    ss = jnp.zeros((1, wdt), F32)
    for side in range(2):
        rows = slice(side * half, (side + 1) * half)
        p_side = (lax.broadcasted_iota(jnp.int32, (half, 1), 0) + i * tr + side * half).astype(F32)
        window = jnp.exp(-(p_side / last_pos) * deltas)
        hf = jnp.dot(hb, w4f_ref[side].astype(BF16), preferred_element_type=F32) * window
        hg = jnp.dot(hb, w4b_ref[side].astype(BF16), preferred_element_type=F32) * window
        hg = jnp.where(p_side == 0.0, 0.0, hg)
        hf_ref[rows, :] = hf
        hg_ref[rows, :] = hg
        ss = ss + jnp.sum(hf * hf + hg * hg, axis=0, keepdims=True)

    @pl.when(i == 0)
    def _():
        ss_ref[...] = jnp.zeros_like(ss_ref)

    ss_ref[...] += ss


def _pad2(a, rows, cols):
    return jnp.zeros((rows, cols), a.dtype).at[:a.shape[0], :a.shape[1]].set(a)


def _hyena_filter(seq_len, p, tr=256):
    wdt = p["hy_w4"].shape[1] // 2
    tr = min(tr, seq_len)
    side = LANES // 2
    hidden = p["hy_w2"].shape[0]
    assert hidden <= side and p["hy_w1"].shape[0] <= side and tr % (2 * SUBLANES) == 0

    def two_sided(w):
        blk = _pad2(w, side, side)
        zero = jnp.zeros_like(blk)
        return jnp.concatenate([jnp.concatenate([blk, zero], axis=1), jnp.concatenate([zero, blk], axis=1)], axis=0)

    def one_side(w, s):
        blk = _pad2(w, side, w.shape[1])
        zero = jnp.zeros_like(blk)
        return jnp.concatenate([blk, zero] if s == 0 else [zero, blk], axis=0)

    w1, w2, w3 = (two_sided(p[k]) for k in ("hy_w1", "hy_w2", "hy_w3"))
    b1, b2, b3, fr = (jnp.tile(_pad2(p[k].reshape(1, -1), 1, side), (1, 2))
                      for k in ("hy_b1", "hy_b2", "hy_b3", "hy_freq"))
    w4f = jnp.stack([one_side(p["hy_w4"][:, :wdt], s) for s in range(2)])
    w4b = jnp.stack([one_side(p["hy_w4"][:, wdt:], s) for s in range(2)])
    full = lambda a: pl.BlockSpec(a.shape, lambda i: (0,) * a.ndim)
    out = pl.BlockSpec((tr, wdt), lambda i: (i, 0))
    args = (w1, b1, w2, b2, w3, b3, fr, w4f, w4b)
    vm = 8 * _nbytes((LANES, wdt), F32) + 8 * _nbytes((tr, wdt), F32)
    return pl.pallas_call(
        functools.partial(_filter_body, seq_len=seq_len),
        grid=(seq_len // tr,),
        in_specs=[full(a) for a in args],
        out_specs=[out, out, pl.BlockSpec((1, wdt), lambda i: (0, 0))],
        out_shape=[jax.ShapeDtypeStruct((seq_len, wdt), F32)] * 2 + [jax.ShapeDtypeStruct((1, wdt), F32)],
        compiler_params=_cparams(("arbitrary",), vm),
        name="hyena_filter",
    )(*args)


def _fft_dims(seq_len):
    n = 2 * seq_len
    p = 1 << ((n.bit_length() - 1) // 2)
    return p, n // p


@functools.lru_cache(maxsize=None)
def _fft_consts(seq_len):
    pp, mm = _fft_dims(seq_len)
    n = pp * mm
    ph = pp // 2
    a = np.arange(ph)
    b = np.arange(mm)
    d = np.arange(pp)
    eye = np.eye(SUBLANES)
    ang = 2 * np.pi * np.outer(d, a) / pp
    fk = np.kron(np.concatenate([np.cos(ang), -np.sin(ang)], axis=0), eye)
    bt = b.reshape(mm // SUBLANES, 1, SUBLANES)
    angt = (2 * np.pi * d[None, :, None] * bt / n).reshape(mm // SUBLANES, pp * SUBLANES, 1)
    angc = -2 * np.pi * np.outer(b, b) / mm
    cr, ci = np.cos(angc), np.sin(angc)
    w2 = np.block([[cr, -ci], [ci, cr]])
    v2 = np.block([[cr, ci], [-ci, cr]])
    gk = np.kron(np.concatenate([np.cos(ang.T), -np.sin(ang.T)], axis=1) / n, eye)
    as_bf16 = lambda x: np.asarray(x, dtype=np.float32).astype(BF16)
    return dict(fk=as_bf16(fk), w2=as_bf16(w2), v2=as_bf16(v2), gk=as_bf16(gk),
                twc=np.cos(angt).astype(np.float32), tws=np.sin(angt).astype(np.float32), pp=pp, mm=mm)


def _stage1_to_scratch(a_ref, fk_ref, twc_ref, tws_ref, group_rows, t):
    pp = a_ref.shape[0] // 2
    groups = twc_ref.shape[0]
    for g in range(groups):
        zt = group_rows(g)
        r = jnp.dot(fk_ref[...], zt, preferred_element_type=F32)
        rr, ri = r[:pp * SUBLANES], r[pp * SUBLANES:]
        c, s = twc_ref[g], tws_ref[g]
        a_ref[:pp, t * groups + g] = (rr * c + ri * s).reshape(pp, SUBLANES, zt.shape[-1])
        a_ref[pp:, t * groups + g] = (ri * c - rr * s).reshape(pp, SUBLANES, zt.shape[-1])


def _group_of_8(ref, g):
    blk = ref[:, g * SUBLANES:(g + 1) * SUBLANES, :]
    return blk.reshape(blk.shape[0] * SUBLANES, blk.shape[-1])


def _scratch_rows(a_ref, d):
    pp = a_ref.shape[0] // 2
    mm = a_ref.shape[1] * SUBLANES
    lanes = a_ref.shape[-1]
    return jnp.concatenate([a_ref[d].reshape(mm, lanes), a_ref[pp + d].reshape(mm, lanes)], axis=0).astype(BF16)


def _bf16_bits(x):
    u = lax.bitcast_convert_type(x, jnp.uint32)
    u = u + jnp.uint32(0x7FFF) + ((u >> 16) & jnp.uint32(1))
    return u & jnp.uint32(0xFFFF0000)


def _pack_ri(re, im):
    return _bf16_bits(re) | (_bf16_bits(im) >> 16)


def _unpack_ri(word):
    re = lax.bitcast_convert_type(word & jnp.uint32(0xFFFF0000), F32)
    im = lax.bitcast_convert_type(word << 16, F32)
    return re, im


def _fft_spec_body(hf_ref, hg_ref, fk_ref, twc_ref, tws_ref, w2_ref, o_ref, a_ref, *, nb):
    t = pl.program_id(1)
    cb = hf_ref.shape[-1]
    mm = a_ref.shape[1] * SUBLANES

    @pl.when(t < nb)
    def _():
        rows = lambda g: jnp.concatenate([_group_of_8(hf_ref, g), _group_of_8(hg_ref, g)], axis=-1).astype(BF16)
        _stage1_to_scratch(a_ref, fk_ref, twc_ref, tws_ref, rows, t)

    @pl.when(t >= nb)
    def _():
        nd_step = o_ref.shape[0]
        for dl in range(nd_step):
            h = jnp.dot(w2_ref[...], _scratch_rows(a_ref, (t - nb) * nd_step + dl), preferred_element_type=F32)
            o_ref[dl, :mm, :] = (h[:mm, :cb] + h[:mm, cb:]).astype(o_ref.dtype)
            o_ref[dl, mm:, :] = (h[mm:, :cb] - h[mm:, cb:]).astype(o_ref.dtype)


def _phase_specs(k, cb, sb, nb):
    pp = k["pp"]
    ph = pp // 2
    step = lambda t: jnp.minimum(t, nb - 1)
    return (pl.BlockSpec((ph, sb, cb), lambda j, t: (0, step(t), j)),
            pl.BlockSpec(k["fk"].shape, lambda j, t: (0, 0)),
            pl.BlockSpec((sb // SUBLANES, pp * SUBLANES, 1), lambda j, t: (step(t), 0, 0)))


FFT_B_PER_STEP = 16
FFT_D_PER_STEP = 16


def _fft_filter_spectrum(hf, hg, k, cb=LANES):
    l, c = hf.shape
    pp, mm = k["pp"], k["mm"]
    ph = pp // 2
    cb = min(cb, c)
    sb, ds = min(2 * FFT_B_PER_STEP, mm), min(2 * FFT_D_PER_STEP, pp)
    nb, nd = mm // sb, pp // ds
    taps, fk_spec, tw_spec = _phase_specs(k, cb, sb, nb)
    vm = (_nbytes((2 * pp, mm // SUBLANES, SUBLANES, 2 * cb), F32) + 4 * _nbytes((ph, sb, cb), F32)
          + 2 * _nbytes(k["fk"].shape, BF16) + 2 * _nbytes((ds, 2 * mm, cb), BF16)
          + 5 * _nbytes((2 * pp * SUBLANES, 2 * cb), F32))
    return pl.pallas_call(
        functools.partial(_fft_spec_body, nb=nb),
        grid=(c // cb, nb + nd),
        in_specs=[taps, taps, fk_spec, tw_spec, tw_spec, pl.BlockSpec(k["w2"].shape, lambda j, t: (0, 0))],
        out_specs=pl.BlockSpec((ds, 2 * mm, cb), lambda j, t: (jnp.maximum(t - nb, 0), 0, j)),
        out_shape=jax.ShapeDtypeStruct((pp, 2 * mm, c), BF16),
        scratch_shapes=[pltpu.VMEM((2 * pp, mm // SUBLANES, SUBLANES, 2 * cb), F32)],
        compiler_params=_cparams(("parallel", "arbitrary"), vm),
        name="fft_filter_spectrum",
    )(hf.reshape(ph, mm, c), hg.reshape(ph, mm, c), k["fk"], k["twc"], k["tws"], k["w2"])


def _fft_mid_body(z_ref, fk_ref, twc_ref, tws_ref, h_ref, w2_ref, v2_ref, o_ref, a_ref, *, nb):
    t = pl.program_id(1)
    mm = a_ref.shape[1] * SUBLANES

    @pl.when(t < nb)
    def _():
        _stage1_to_scratch(a_ref, fk_ref, twc_ref, tws_ref, lambda g: _group_of_8(z_ref, g).astype(BF16), t)

    @pl.when(t >= nb)
    def _():
        nd_step = o_ref.shape[0]
        for dl in range(nd_step):
            x = jnp.dot(w2_ref[...], _scratch_rows(a_ref, (t - nb) * nd_step + dl), preferred_element_type=F32)
            xr, xi = x[:mm], x[mm:]
            hr, hi = h_ref[dl, :mm, :].astype(F32), h_ref[dl, mm:, :].astype(F32)
            y = jnp.concatenate([xr * hr - xi * hi, xr * hi + xi * hr], axis=0).astype(BF16)
            bd = jnp.dot(v2_ref[...], y, preferred_element_type=F32)
            o_ref[dl] = _pack_ri(bd[:mm], bd[mm:])


def _fft_mid(z, h, k, cb=256):
    l, c = z.shape
    pp, mm = k["pp"], k["mm"]
    cb = min(cb, c)
    sb, ds = min(FFT_B_PER_STEP, mm), min(FFT_D_PER_STEP, pp)
    nb, nd = mm // sb, pp // ds
    sig, fk_spec, tw_spec = _phase_specs(k, cb, sb, nb)
    const = pl.BlockSpec(k["w2"].shape, lambda j, t: (0, 0))
    dstep = lambda t: jnp.maximum(t - nb, 0)
    vm = (_nbytes((2 * pp, mm // SUBLANES, SUBLANES, cb), F32) + 2 * _nbytes((pp // 2, sb, cb), F32)
          + 2 * _nbytes(k["fk"].shape, BF16) + 2 * _nbytes((ds, 2 * mm, cb), BF16)
          + 2 * _nbytes((ds, mm, cb), F32) + 5 * _nbytes((2 * pp * SUBLANES, cb), F32))
    return pl.pallas_call(
        functools.partial(_fft_mid_body, nb=nb),
        grid=(c // cb, nb + nd),
        in_specs=[sig, fk_spec, tw_spec, tw_spec,
                  pl.BlockSpec((ds, 2 * mm, cb), lambda j, t: (dstep(t), 0, j)), const, const],
        out_specs=pl.BlockSpec((ds, mm, cb), lambda j, t: (dstep(t), 0, j)),
        out_shape=jax.ShapeDtypeStruct((pp, mm, c), jnp.uint32),
        scratch_shapes=[pltpu.VMEM((2 * pp, mm // SUBLANES, SUBLANES, cb), F32)],
        compiler_params=_cparams(("parallel", "arbitrary"), vm),
        name="fft_mid",
    )(z.reshape(pp // 2, mm, c), k["fk"], k["twc"], k["tws"], h, k["w2"], k["v2"])


def _fft_last_body(b_ref, gk_ref, twc_ref, tws_ref, x0_ref, z_ref, ss_ref, bias_ref, o_ref):
    scale = lax.rsqrt(ss_ref[...] + EPS)
    ys = []
    for g in range(twc_ref.shape[0]):
        br, bi = _unpack_ri(_group_of_8(b_ref, g))
        c, s = twc_ref[g], tws_ref[g]
        rhs = jnp.concatenate([br * c - bi * s, br * s + bi * c], axis=0).astype(BF16)
        y = jnp.dot(gk_ref[...], rhs, preferred_element_type=F32)
        ys.append(y.reshape(o_ref.shape[0], SUBLANES, o_ref.shape[2]))
    y = jnp.concatenate(ys, axis=1)
    o_ref[...] = (x0_ref[...].astype(F32) * (y * scale + z_ref[...] * bias_ref[...])).astype(o_ref.dtype)


def _fft_last(bmat, k, x0, z, ss, bias, cb=256):
    pp, mm, c = bmat.shape
    ph = pp // 2
    cb = min(cb, c)
    sb = min(FFT_B_PER_STEP, mm)
    view = pl.BlockSpec((ph, sb, cb), lambda j, b: (0, b, j))
    row = pl.BlockSpec((1, cb), lambda j, b: (0, j))
    tw_spec = pl.BlockSpec((sb // SUBLANES, pp * SUBLANES, 1), lambda j, b: (b, 0, 0))
    vm = 2 * (_nbytes((pp, sb, cb), F32) + 3 * _nbytes((ph, sb, cb), F32)
              + _nbytes(k["gk"].shape, BF16)) + 6 * _nbytes((2 * pp * SUBLANES, cb), F32)
    out = pl.pallas_call(
        _fft_last_body,
        grid=(c // cb, mm // sb),
        in_specs=[pl.BlockSpec((pp, sb, cb), lambda j, b: (0, b, j)),
                  pl.BlockSpec(k["gk"].shape, lambda j, b: (0, 0)),
                  tw_spec, tw_spec, view, view, row, row],
        out_specs=view,
        out_shape=jax.ShapeDtypeStruct((ph, mm, c), BF16),
        compiler_params=_cparams(("parallel", "parallel"), vm),
        name="fft_last",
    )(bmat, k["gk"], k["twc"], k["tws"], x0.reshape(ph, mm, c), z.reshape(ph, mm, c), ss, bias.reshape(1, c))
    return out.reshape(ph * mm, c)


def _hyena_longconv(x0, z, hf, hg, ss, bias):
    k = _fft_consts(z.shape[0])
    h = _fft_filter_spectrum(hf, hg, k)
    return _fft_last(_fft_mid(z, h, k), k, x0, z, ss, bias)


def _router_body(h_ref, w_ref, o_ref):
    logits = jnp.dot(h_ref[...], w_ref[...], preferred_element_type=F32, precision=HIGHEST)
    lane = lax.broadcasted_iota(jnp.int32, logits.shape, 1)
    lg = jnp.where(lane < N_EXPERTS, logits, -jnp.inf)
    m1 = jnp.max(lg, axis=-1, keepdims=True)
    i1 = jnp.min(jnp.where(lg == m1, lane, LANES), axis=-1, keepdims=True)
    l2 = jnp.where(lane == i1, -jnp.inf, lg)
    m2 = jnp.max(l2, axis=-1, keepdims=True)
    i2 = jnp.min(jnp.where(l2 == m2, lane, LANES), axis=-1, keepdims=True)
    e = jnp.exp(m2 - m1)
    w1 = 1.0 / (1.0 + e)
    w2 = e * w1
    o_ref[...] = jnp.where(lane == 0, i1.astype(F32), jnp.where(lane == 1, i2.astype(F32),
                           jnp.where(lane == 2, w1, jnp.where(lane == 3, w2, 0.0))))


def _router(h, router_w, tm=256):
    m, d = h.shape
    tm = min(tm, m)
    w = _pad2(router_w, d, LANES)
    vm = 2 * (_nbytes((tm, d), F32) + _nbytes((d, LANES), F32)) + 8 * _nbytes((tm, LANES), F32)
    return pl.pallas_call(
        _router_body,
        grid=(m // tm,),
        in_specs=[pl.BlockSpec((tm, d), lambda i: (i, 0)), pl.BlockSpec((d, LANES), lambda i: (0, 0))],
        out_specs=pl.BlockSpec((tm, LANES), lambda i: (i, 0)),
        out_shape=jax.ShapeDtypeStruct((m, LANES), F32),
        compiler_params=_cparams(("parallel",), vm),
        name="moe_router",
    )(h, w)


def _routing_tables(route, tile):
    t = route.shape[0]
    e_flat = route[:, :TOP_K].astype(jnp.int32).reshape(-1)
    w_flat = route[:, TOP_K:2 * TOP_K].reshape(-1)
    onehot = (e_flat[:, None] == jnp.arange(N_EXPERTS, dtype=jnp.int32)[None, :]).astype(jnp.int32)
    csum = jnp.cumsum(onehot, axis=0)
    rank = jnp.take_along_axis(csum, e_flat[:, None], axis=1)[:, 0] - 1
    counts = csum[-1]
    padded = ((counts + tile - 1) // tile) * tile
    ends = jnp.cumsum(padded)
    pos = (ends - padded)[e_flat] + rank
    n_tiles = (t * TOP_K) // tile + N_EXPERTS
    rows = n_tiles * tile
    token = jnp.arange(t * TOP_K, dtype=jnp.int32) // TOP_K
    w_bits = lax.bitcast_convert_type(w_flat, jnp.int32)
    table = jnp.zeros((rows, 2), jnp.int32).at[pos].set(jnp.stack([token, w_bits], axis=1))
    row_token = table[:, 0]
    row_w = lax.bitcast_convert_type(table[:, 1], F32)
    start = jnp.arange(n_tiles, dtype=jnp.int32) * tile
    valid = start < ends[-1]
    expert = jnp.minimum(jnp.sum((start[:, None] >= ends[None, :]).astype(jnp.int32), axis=1), N_EXPERTS - 1)
    last_valid = jnp.max(jnp.where(valid, expert, 0))
    expert = jnp.where(valid, expert, last_valid)
    changed = jnp.concatenate([jnp.ones((1,), bool), expert[1:] != expert[:-1]])
    is_start = changed & valid
    idx = jnp.arange(n_tiles, dtype=jnp.int32)
    later_start = jnp.where(is_start[None, :] & (idx[None, :] > idx[:, None]), idx[None, :], n_tiles)
    nxt = jnp.min(later_start, axis=1)
    wrap = (nxt == n_tiles).astype(jnp.int32)
    next_expert = expert[jnp.where(nxt == n_tiles, 0, nxt)]
    tiles = (expert, is_start.astype(jnp.int32), valid.astype(jnp.int32), next_expert, wrap)
    return row_token, row_w, pos.astype(jnp.int32), tiles


def _row_copy(src_hbm, dst, sem, src_row, dst_row):
    return pltpu.make_async_copy(src_hbm.at[pl.ds(src_row, 1)], dst.at[pl.ds(dst_row, 1)], sem)


DMA_LOOP_UNROLL = 8


def _gather_rows_body(tok_ref, h_hbm, o_ref, buf, sems):
    i = pl.program_id(0)
    gt = buf.shape[1]
    slot = i % 2

    def issue(step, s):
        def body(r, carry):
            _row_copy(h_hbm, buf.at[s], sems.at[s], tok_ref[step * gt + r], r).start()
            return carry
        lax.fori_loop(0, gt, body, 0, unroll=DMA_LOOP_UNROLL)

    def drain(s):
        def body(r, carry):
            _row_copy(h_hbm, buf.at[s], sems.at[s], 0, r).wait()
            return carry
        lax.fori_loop(0, gt, body, 0, unroll=DMA_LOOP_UNROLL)

    @pl.when(i == 0)
    def _():
        issue(0, 0)

    @pl.when(i + 1 < pl.num_programs(0))
    def _():
        issue(i + 1, 1 - slot)

    drain(slot)
    first_half, second_half = _unpack_ri(buf[slot])
    o_ref[...] = jnp.concatenate([first_half, second_half], axis=-1).astype(o_ref.dtype)


def _gather_rows(h_pairs, row_token, gt=256):
    rows = row_token.shape[0]
    dh = h_pairs.shape[1]
    vm = 6 * _nbytes((gt, dh), F32)
    return pl.pallas_call(
        _gather_rows_body,
        grid_spec=pltpu.PrefetchScalarGridSpec(
            num_scalar_prefetch=1,
            grid=(rows // gt,),
            in_specs=[pl.BlockSpec(memory_space=pl.ANY)],
            out_specs=pl.BlockSpec((gt, 2 * dh), lambda i, tok: (i, 0)),
            scratch_shapes=[pltpu.VMEM((2, gt, dh), jnp.uint32), pltpu.SemaphoreType.DMA((2,))],
        ),
        out_shape=jax.ShapeDtypeStruct((rows, 2 * dh), BF16),
        compiler_params=_cparams(("arbitrary",), vm),
        name="moe_gather_rows",
    )(row_token, h_pairs)


def _group_weights(tiles, w_hbms, w_f32s, w_bf16s, sems):
    te_ref, ts_ref, _, ne_ref, wrap_ref = tiles
    j, i = pl.program_id(0), pl.program_id(1)
    tn = w_f32s[0].shape[1]

    def copies(expert, col_block):
        col = pl.multiple_of(col_block * tn, tn)
        return [pltpu.make_async_copy(w.at[expert, :, pl.ds(col, tn)], buf, sems.at[n])
                for n, (w, buf) in enumerate(zip(w_hbms, w_f32s))]

    @pl.when(ts_ref[i] == 1)
    def _():
        @pl.when((j == 0) & (i == 0))
        def _():
            for cp in copies(te_ref[i], j):
                cp.start()

        for cp in copies(te_ref[i], j):
            cp.wait()
        for src, dst in zip(w_f32s, w_bf16s):
            dst[...] = src[...].astype(BF16)
        nj = j + wrap_ref[i]

        @pl.when(nj < pl.num_programs(0))
        def _():
            for cp in copies(ne_ref[i], nj):
                cp.start()


def _moe_up_body(te, ts, tv, ne, wrap, x_ref, wg_hbm, wu_hbm, o_ref, wg32, wu32, wg_bf, wu_bf, sems):
    i = pl.program_id(1)
    _group_weights((te, ts, tv, ne, wrap), (wg_hbm, wu_hbm), (wg32, wu32), (wg_bf, wu_bf), sems)

    @pl.when(tv[i] == 1)
    def _():
        x = x_ref[...]
        g = jnp.dot(x, wg_bf[...], preferred_element_type=F32)
        u = jnp.dot(x, wu_bf[...], preferred_element_type=F32)
        o_ref[...] = (g * jax.nn.sigmoid(g) * u).astype(o_ref.dtype)

    @pl.when(tv[i] == 0)
    def _():
        o_ref[...] = jnp.zeros_like(o_ref)


def _moe_up(xs, wg, wu, tiles, tile, tn=1024):
    rows, d = xs.shape
    f = wg.shape[2]
    vm = 2 * _nbytes((d, tn), F32) + 2 * _nbytes((d, tn), BF16) + 2 * _nbytes((tile, d), BF16) + 6 * _nbytes((tile, tn), F32)
    return pl.pallas_call(
        _moe_up_body,
        grid_spec=pltpu.PrefetchScalarGridSpec(
            num_scalar_prefetch=len(tiles),
            grid=(f // tn, rows // tile),
            in_specs=[pl.BlockSpec((tile, d), lambda j, i, *_: (i, 0)),
                      pl.BlockSpec(memory_space=pl.ANY), pl.BlockSpec(memory_space=pl.ANY)],
            out_specs=pl.BlockSpec((tile, tn), lambda j, i, *_: (i, j)),
            scratch_shapes=[pltpu.VMEM((d, tn), F32), pltpu.VMEM((d, tn), F32),
                            pltpu.VMEM((d, tn), BF16), pltpu.VMEM((d, tn), BF16),
                            pltpu.SemaphoreType.DMA((2,))],
        ),
        out_shape=jax.ShapeDtypeStruct((rows, f), BF16),
        compiler_params=_cparams(("arbitrary", "arbitrary"), vm),
        name="moe_up",
    )(*tiles, xs, wg, wu)


def _moe_down_body(te, ts, tv, ne, wrap, a_ref, wd_hbm, rw_ref, o_ref, wd32, wd_bf, sems):
    i = pl.program_id(1)
    _group_weights((te, ts, tv, ne, wrap), (wd_hbm,), (wd32,), (wd_bf,), sems)

    @pl.when(tv[i] == 1)
    def _():
        o_ref[...] = jnp.dot(a_ref[...], wd_bf[...], preferred_element_type=F32) * rw_ref[...]

    @pl.when(tv[i] == 0)
    def _():
        o_ref[...] = jnp.zeros_like(o_ref)


def _moe_down(act, wd, row_w, tiles, tile, tn=512):
    rows, f = act.shape
    d = wd.shape[2]
    vm = _nbytes((f, tn), F32) + _nbytes((f, tn), BF16) + 2 * _nbytes((tile, f), BF16) + 4 * _nbytes((tile, tn), F32)
    return pl.pallas_call(
        _moe_down_body,
        grid_spec=pltpu.PrefetchScalarGridSpec(
            num_scalar_prefetch=len(tiles),
            grid=(d // tn, rows // tile),
            in_specs=[pl.BlockSpec((tile, f), lambda j, i, *_: (i, 0)),
                      pl.BlockSpec(memory_space=pl.ANY),
                      pl.BlockSpec((tile, 1), lambda j, i, *_: (i, 0))],
            out_specs=pl.BlockSpec((tile, tn), lambda j, i, *_: (i, j)),
            scratch_shapes=[pltpu.VMEM((f, tn), F32), pltpu.VMEM((f, tn), BF16), pltpu.SemaphoreType.DMA((1,))],
        ),
        out_shape=jax.ShapeDtypeStruct((rows, d), F32),
        compiler_params=_cparams(("arbitrary", "arbitrary"), vm),
        name="moe_down",
    )(*tiles, act, wd, row_w.reshape(rows, 1))


def _moe_combine_body(pos_ref, ys_hbm, x_ref, gate_ref, g_ref, o_ref, buf, sems, *, final_norm):
    i = pl.program_id(0)
    gt = x_ref.shape[0]
    slot = i % 2

    def issue(step, s):
        def body(r, carry):
            for k in range(TOP_K):
                _row_copy(ys_hbm, buf.at[s, k], sems.at[s], pos_ref[TOP_K * (step * gt + r) + k], r).start()
            return carry
        lax.fori_loop(0, gt, body, 0, unroll=DMA_LOOP_UNROLL)

    def drain(s):
        def body(r, carry):
            for k in range(TOP_K):
                _row_copy(ys_hbm, buf.at[s, k], sems.at[s], 0, r).wait()
            return carry
        lax.fori_loop(0, gt, body, 0, unroll=DMA_LOOP_UNROLL)

    @pl.when(i == 0)
    def _():
        issue(0, 0)

    @pl.when(i + 1 < pl.num_programs(0))
    def _():
        issue(i + 1, 1 - slot)

    drain(slot)
    y = buf[slot, 0]
    for k in range(1, TOP_K):
        y = y + buf[slot, k]
    x = x_ref[...] + gate_ref[...] * y
    if final_norm:
        x = x * lax.rsqrt(jnp.mean(x * x, axis=-1, keepdims=True) + EPS) * g_ref[...]
    o_ref[...] = x


def _moe_combine(ys, pos, x, gate, norm_g, final_norm, gt=128):
    t, d = x.shape
    gt = min(gt, t)
    row = pl.BlockSpec((1, d), lambda i, p: (0, 0))
    tilespec = pl.BlockSpec((gt, d), lambda i, p: (i, 0))
    vm = (2 * TOP_K + 6) * _nbytes((gt, d), F32)
    return pl.pallas_call(
        functools.partial(_moe_combine_body, final_norm=final_norm),
        grid_spec=pltpu.PrefetchScalarGridSpec(
            num_scalar_prefetch=1,
            grid=(t // gt,),
            in_specs=[pl.BlockSpec(memory_space=pl.ANY), tilespec, row, row],
            out_specs=tilespec,
            scratch_shapes=[pltpu.VMEM((2, TOP_K, gt, d), F32), pltpu.SemaphoreType.DMA((2,))],
        ),
        out_shape=jax.ShapeDtypeStruct((t, d), F32),
        compiler_params=_cparams(("arbitrary",), vm),
        name="moe_combine",
    )(pos, ys, x, gate.reshape(1, d), norm_g.reshape(1, d))


MOE_ROW_TILE = 512


def _moe_ffn(h32, h_pairs, x, gate, p, norm_g, final_norm):
    route = _router(h32, p["router_w"])
    row_token, row_w, pos, tiles = _routing_tables(route, MOE_ROW_TILE)
    xs = _gather_rows(h_pairs, row_token)
    act = _moe_up(xs, p["exp_w_gate"], p["exp_w_up"], tiles, MOE_ROW_TILE)
    ys = _moe_down(act, p["exp_w_down"], row_w, tiles, MOE_ROW_TILE)
    return _moe_combine(ys, pos, x, gate, norm_g, final_norm)


def _silu(v):
    return v * jax.nn.sigmoid(v)


def _in_weights(w_in_all, layer, d):
    qk, vw, r = d // 2, d, GLA_GATE_RANK
    col_v = qk
    col_a = col_v + vw
    col_q = col_a + 2 * r
    col_g = col_q + qk
    col_hy = col_g + vw
    col_gate = col_hy + 3 * d
    return dict(wt=jnp.swapaxes(w_in_all, 1, 2), layer=layer, k=(0, qk), v=(col_v, vw), a=(col_a, 2 * r),
                q=(col_q, qk), g=(col_g, vw), hy=(col_hy, 3 * d), gate=(col_gate, 2 * d))


def _proj_body(x_ref, wt_hbm, o_ref, w32, wbf, sem, *, layer, row0, epi):
    j, i = pl.program_id(0), pl.program_id(1)
    rows = w32.shape[0]

    def fetch(block):
        start = pl.multiple_of(row0 + block * rows, SUBLANES)
        return pltpu.make_async_copy(wt_hbm.at[layer, pl.ds(start, rows), :], w32, sem)

    @pl.when(i == 0)
    def _():
        @pl.when(j == 0)
        def _():
            fetch(j).start()

        fetch(j).wait()
        if rows == wbf.shape[0]:
            wbf[...] = w32[...].astype(BF16)
        else:
            wbf[...] = jnp.zeros_like(wbf)
            wbf[:rows, :] = w32[...].astype(BF16)

        @pl.when(j + 1 < pl.num_programs(0))
        def _():
            fetch(j + 1).start()

    o_ref[...] = epi(_dot_nt(x_ref[...], wbf[...])).astype(o_ref.dtype)


def _proj(h, wi, group, epi, out_dtype, tm=1024, tn=1024):
    m, d = h.shape
    col, width = wi[group]
    rows = min(tn, width)
    tn = max(rows, LANES)
    tm = min(tm, m)
    assert m % tm == 0 and width % rows == 0 and col % SUBLANES == 0 and rows % SUBLANES == 0
    n_out = width // rows * tn
    vm = _nbytes((rows, d), F32) + _nbytes((tn, d), BF16) + 2 * _nbytes((tm, d), BF16) + 4 * _nbytes((tm, tn), F32)
    out, = pl.pallas_call(
        functools.partial(_proj_body, layer=wi["layer"], row0=col, epi=epi),
        grid=(width // rows, m // tm),
        in_specs=[pl.BlockSpec((tm, d), lambda j, i: (i, 0)), pl.BlockSpec(memory_space=pl.ANY)],
        out_specs=[pl.BlockSpec((tm, tn), lambda j, i: (i, j))],
        out_shape=[jax.ShapeDtypeStruct((m, n_out), out_dtype)],
        scratch_shapes=[pltpu.VMEM((rows, d), F32), pltpu.VMEM((tn, d), BF16), pltpu.SemaphoreType.DMA(())],
        compiler_params=_cparams(("arbitrary", "arbitrary"), vm),
        name="proj_" + group,
    )(h, wi["wt"])
    return out


def _gate_matrices(p, d):
    r = GLA_GATE_RANK
    qk = d // 2
    awf = jnp.zeros((LANES, qk), F32).at[:r].set(p["gla_aw_f"])
    awb = jnp.zeros((LANES, qk), F32).at[r:2 * r].set(p["gla_aw_b"])
    return awf, awb, p["gla_ab_f"].reshape(1, qk), p["gla_ab_b"].reshape(1, qk)


GLA_CHUNK = 256


def _gla_states_only(h, wi, p, d):
    ident = lambda acc: acc
    k = _proj(h, wi, "k", ident, BF16)
    v = _proj(h, wi, "v", ident, BF16)
    a = _proj(h, wi, "a", ident, F32)
    ef, eb = _gla_decay(a, *_gate_matrices(p, d), chunk=min(GLA_CHUNK, h.shape[0]))
    dk, dv = d // 2 // GLA_HEADS, d // GLA_HEADS
    zero = jnp.zeros((GLA_HEADS, dv, dk), F32)
    _, s_f = _gla_scan(k, k, v, ef, zero, fwd=True, chunk=GLA_CHUNK)
    _, s_b = _gla_scan(k, k, v, eb, zero, fwd=False, chunk=GLA_CHUNK)
    return s_f, s_b


def _token_mixer(x, h, mod_gate, s0_f, s0_b, wi, p, d):
    l = h.shape[0]
    dk = d // 2 // GLA_HEADS
    ident = lambda acc: acc
    k = _proj(h, wi, "k", ident, BF16)
    v = _proj(h, wi, "v", ident, BF16)
    a = _proj(h, wi, "a", ident, F32)
    q = _proj(h, wi, "q", lambda acc: acc * (dk ** -0.5), BF16)
    sg = _proj(h, wi, "g", _silu, BF16)
    hy = _proj(h, wi, "hy", ident, BF16)
    gates = _proj(h, wi, "gate", jax.nn.sigmoid, BF16)

    ef, eb = _gla_decay(a, *_gate_matrices(p, d), chunk=min(GLA_CHUNK, l))
    o_f, s_f = _gla_scan(q, k, v, ef, s0_f, fwd=True, chunk=GLA_CHUNK)
    o_gla, s_b = _gla_scan(q, k, v, eb, s0_b, fwd=False, chunk=GLA_CHUNK, combine=(o_f, sg, p["gla_norm_g"]))

    x0, z = _hyena_shortconv(hy, p["hy_conv_w"], p["hy_conv_b"])
    hf, hg, ss = _hyena_filter(l, p)
    o_hy = _hyena_longconv(x0, z, hf, hg, ss, p["hy_bias"])

    layer = wi["layer"]
    t1, = _mm(o_hy, [wi["w_up_hy"]], lambda accs, ex: [ex[0].astype(F32) * accs[0]], (F32,),
              extras=[(gates, "tile", 0)], w_layer=layer, name="up_hy")
    merged, = _mm(o_gla, [wi["w_up_gla"]], lambda accs, ex: [ex[1] + ex[0].astype(F32) * accs[0]], (BF16,),
                  extras=[(gates, "tile", d), (t1, "tile", 0)], w_layer=layer, name="up_gla_merge")
    x_new, = _mm(merged, [wi["w_out"]], lambda accs, ex: [ex[0] + ex[1] * accs[0]], (F32,),
                 extras=[(x, "tile", 0), (mod_gate.reshape(1, d), "row", 0)], w_layer=layer, name="out_proj")
    return x_new, s_f, s_b


def _dense_ffn(x, h, mod_gate, wg, wu, wd):
    d, f = wg.shape
    fp = -(-f // 512) * 512
    wg_b = _pad2(wg.astype(BF16), d, fp)
    wu_b = _pad2(wu.astype(BF16), d, fp)
    wd_b = _pad2(wd.astype(BF16), fp, d)
    act, = _mm(h, [wg_b, wu_b], lambda accs, ex: [_silu(accs[0]) * accs[1]], (BF16,), tn=512, name="ffn_up")
    x_new, = _mm(act, [wd_b], lambda accs, ex: [ex[0] + ex[1] * accs[0]], (F32,), tm=512,
                 extras=[(x, "tile", 0), (mod_gate.reshape(1, d), "row", 0)], name="ffn_down")
    return x_new


def kernel(x, c, ctx, c_ctx, ada_w, ada_b, norm_mix_g, norm_ffn_g, w_in, hy_conv_w, hy_conv_b, hy_w1, hy_b1, hy_w2, hy_b2, hy_w3, hy_b3, hy_w4, hy_freq, hy_bias, gla_aw_f, gla_ab_f, gla_aw_b, gla_ab_b, gla_norm_g, w_up_hy, w_up_gla, w_out, ffn_w_gate, ffn_w_up, ffn_w_down, router_w, exp_w_gate, exp_w_up, exp_w_down, final_norm_g):
    assert x.shape[0] == 1 and c.shape[0] == 1, "batch size 1 only"
    depth, d = norm_mix_g.shape
    x_lat, x_ctx = x[0], ctx[0]
    mods = _ada_modulation(jnp.concatenate([c, c_ctx.reshape(1, d)], axis=0), ada_w, ada_b)
    dk, dv = d // 2 // GLA_HEADS, d // GLA_HEADS
    per_layer = dict(hy_conv_w=hy_conv_w, hy_conv_b=hy_conv_b, hy_w1=hy_w1, hy_b1=hy_b1, hy_w2=hy_w2, hy_b2=hy_b2,
                     hy_w3=hy_w3, hy_b3=hy_b3, hy_w4=hy_w4, hy_freq=hy_freq, hy_bias=hy_bias,
                     gla_aw_f=gla_aw_f, gla_ab_f=gla_ab_f, gla_aw_b=gla_aw_b, gla_ab_b=gla_ab_b,
                     gla_norm_g=gla_norm_g)
    for l in range(depth):
        last = l == depth - 1
        p = {name: arr[l] for name, arr in per_layer.items()}
        wi = dict(_in_weights(w_in, l, d), w_up_hy=w_up_hy, w_up_gla=w_up_gla, w_out=w_out)
        lat = [mods[l, 0, i * d:(i + 1) * d] for i in range(N_ADA)]
        cxm = [mods[l, 1, i * d:(i + 1) * d] for i in range(N_ADA)]

        h_ctx, = _modnorm(x_ctx, norm_mix_g[l], cxm[0], cxm[1], (BF16,))
        if last:
            s_f, s_b = _gla_states_only(h_ctx, wi, p, d)
        else:
            zero = jnp.zeros((GLA_HEADS, dv, dk), F32)
            x_ctx, s_f, s_b = _token_mixer(x_ctx, h_ctx, cxm[2], zero, zero, wi, p, d)
        h_lat, = _modnorm(x_lat, norm_mix_g[l], lat[0], lat[1], (BF16,))
        x_lat, _, _ = _token_mixer(x_lat, h_lat, lat[2], s_f, s_b, wi, p, d)

        i = l // 2
        if l % 2 == 0:
            h2, = _modnorm(x_lat, norm_ffn_g[l], lat[3], lat[4], (BF16,))
            x_lat = _dense_ffn(x_lat, h2, lat[5], ffn_w_gate[i], ffn_w_up[i], ffn_w_down[i])
            if not last:
                h2c, = _modnorm(x_ctx, norm_ffn_g[l], cxm[3], cxm[4], (BF16,))
                x_ctx = _dense_ffn(x_ctx, h2c, cxm[5], ffn_w_gate[i], ffn_w_up[i], ffn_w_down[i])
        else:
            pm = dict(router_w=router_w[i], exp_w_gate=exp_w_gate[i], exp_w_up=exp_w_up[i], exp_w_down=exp_w_down[i])
            h2, h2p = _modnorm(x_lat, norm_ffn_g[l], lat[3], lat[4], (F32, jnp.uint32))
            x_lat = _moe_ffn(h2, h2p, x_lat, lat[5], pm, final_norm_g, final_norm=last)
            if not last:
                h2c, h2cp = _modnorm(x_ctx, norm_ffn_g[l], cxm[3], cxm[4], (F32, jnp.uint32))
                x_ctx = _moe_ffn(h2c, h2cp, x_ctx, cxm[5], pm, final_norm_g, final_norm=False)
    if depth % 2 == 1:
        x_lat, = _modnorm(x_lat, final_norm_g, jnp.zeros((d,), F32), jnp.zeros((d,), F32), (F32,))
    return x_lat[None]
```

```python
import functools
import math

import numpy as np
import jax
import jax.numpy as jnp
from jax import lax
from jax.experimental import pallas as pl
from jax.experimental.pallas import tpu as pltpu

F32 = jnp.float32
BF16 = jnp.bfloat16
HIGHEST = lax.Precision.HIGHEST

EPS = 1e-6
N_ADA = 6
LANES = 128
SUBLANES = 8
VMEM_LIMIT_CAP = 60 * 1024 * 1024

GLA_HEADS = 4
GLA_GATE_RANK = 16
GLA_GATE_TEMP = 16.0
HY_SHORT = 3
HY_EMB_BANDS = 16
HY_FILTER_HIDDEN = 64
HY_DECAY_TARGET = 1e-2
HY_FAST_DECAY = 0.3
HY_SLOW_DECAY = 1.5
N_EXPERTS = 8
TOP_K = 2


def _cparams(sem, vmem_bytes):
    limit = int(min(max(vmem_bytes * 5 // 4 + (2 << 20), 16 << 20), VMEM_LIMIT_CAP))
    return pltpu.CompilerParams(dimension_semantics=sem, vmem_limit_bytes=limit)


def _nbytes(shape, dtype):
    return int(np.prod(shape)) * jnp.dtype(dtype).itemsize


def _ada_body(c_ref, w_ref, b_ref, o_ref):
    tn = o_ref.shape[-1]
    d = c_ref.shape[1]
    rows = []
    for r in range(2):
        s = c_ref[r]
        s = s * jax.nn.sigmoid(s)
        chunks = []
        for n0 in range(0, tn, LANES):
            p = w_ref[:, n0:n0 + LANES] * s
            acc = p.reshape(d // SUBLANES, SUBLANES, LANES).sum(axis=0)
            chunks.append(acc.sum(axis=0, keepdims=True))
        rows.append(jnp.concatenate(chunks, axis=1))
    o_ref[...] = jnp.concatenate(rows, axis=0) + b_ref[...]


def _ada_modulation(cond2, ada_w, ada_b):
    depth, d, n = ada_w.shape
    tn = 1536 if n % 1536 == 0 else LANES
    cb = jnp.broadcast_to(cond2[:, :, None], (2, d, LANES))
    vm = 2 * _nbytes((d, tn), F32) + 2 * _nbytes((2, d, LANES), F32)
    return pl.pallas_call(
        _ada_body,
        grid=(depth, n // tn),
        in_specs=[
            pl.BlockSpec((2, d, LANES), lambda l, j: (0, 0, 0)),
            pl.BlockSpec((None, d, tn), lambda l, j: (l, 0, j)),
            pl.BlockSpec((None, 1, tn), lambda l, j: (l, 0, j)),
        ],
        out_specs=pl.BlockSpec((None, 2, tn), lambda l, j: (l, 0, j)),
        out_shape=jax.ShapeDtypeStruct((depth, 2, n), F32),
        compiler_params=_cparams(("parallel", "parallel"), vm),
        name="ada_modulation",
    )(cb, ada_w, ada_b.reshape(depth, 1, n))


def _modnorm_body(x_ref, g_ref, sh_ref, sc_ref, *o_refs):
    x = x_ref[...]
    ms = jnp.mean(x * x, axis=-1, keepdims=True)
    y = x * lax.rsqrt(ms + EPS) * g_ref[...]
    y = y * (1.0 + sc_ref[...]) + sh_ref[...]
    half = y.shape[1] // 2
    for o in o_refs:
        if o.dtype == jnp.uint32:
            o[...] = _pack_ri(y[:, :half], y[:, half:])
        else:
            o[...] = y.astype(o.dtype)


def _modnorm(x, g, shift, scale, out_dtypes, tm=256):
    m, d = x.shape
    tm = min(tm, m)
    row = pl.BlockSpec((1, d), lambda i: (0, 0))
    tile = pl.BlockSpec((tm, d), lambda i: (i, 0))
    width = lambda dt: d // 2 if dt == jnp.uint32 else d
    vm = 2 * _nbytes((tm, d), F32) * (1 + len(out_dtypes))
    outs = pl.pallas_call(
        _modnorm_body,
        grid=(m // tm,),
        in_specs=[tile, row, row, row],
        out_specs=[pl.BlockSpec((tm, width(dt)), lambda i: (i, 0)) for dt in out_dtypes],
        out_shape=[jax.ShapeDtypeStruct((m, width(dt)), dt) for dt in out_dtypes],
        compiler_params=_cparams(("parallel",), vm),
        name="modnorm",
    )(x, g.reshape(1, d), shift.reshape(1, d), scale.reshape(1, d))
    return outs


def _mm_body(*refs, n_w, n_e, n_o, epi, cast_w):
    x_ref = refs[0]
    w_refs = refs[1:1 + n_w]
    e_refs = refs[1 + n_w:1 + n_w + n_e]
    o_refs = refs[1 + n_w + n_e:1 + n_w + n_e + n_o]
    w_bf16 = refs[1 + n_w + n_e + n_o:]
    x = x_ref[...].astype(BF16)
    if cast_w:
        @pl.when(pl.program_id(1) == 0)
        def _():
            for src, dst in zip(w_refs, w_bf16):
                dst[...] = src[...].astype(BF16)
        w_refs = w_bf16
    accs = [jnp.dot(x, w[...], preferred_element_type=F32) for w in w_refs]
    outs = epi(accs, [e[...] for e in e_refs])
    for o, v in zip(o_refs, outs):
        o[...] = v.astype(o.dtype)


def _mm(x, ws, epi, out_dtypes, extras=(), w_layer=None, tm=1024, tn=1024, name="mm"):
    m, k = x.shape
    n = ws[0].shape[-1]
    tm = min(tm, m)
    tn = min(tn, n)
    assert m % tm == 0 and n % tn == 0, (m, tm, n, tn)
    cast_w = ws[0].dtype == F32
    in_specs = [pl.BlockSpec((tm, k), lambda j, i: (i, 0))]
    if w_layer is None:
        in_specs += [pl.BlockSpec((k, tn), lambda j, i: (0, j))] * len(ws)
    else:
        in_specs += [pl.BlockSpec((None, k, tn), lambda j, i: (w_layer, 0, j))] * len(ws)
    vm = 2 * _nbytes((tm, k), x.dtype) + 2 * len(ws) * _nbytes((k, tn), ws[0].dtype)
    for arr, kind, col in extras:
        assert col % tn == 0
        if kind == "tile":
            in_specs.append(pl.BlockSpec((tm, tn), lambda j, i, c=col // tn: (i, j + c)))
            vm += 2 * _nbytes((tm, tn), arr.dtype)
        else:
            in_specs.append(pl.BlockSpec((1, tn), lambda j, i, c=col // tn: (0, j + c)))
    vm += sum(2 * _nbytes((tm, tn), dt) for dt in out_dtypes) + (1 + len(ws)) * _nbytes((tm, tn), F32)
    scratch = [pltpu.VMEM((k, tn), BF16)] * len(ws) if cast_w else []
    vm += len(scratch) * _nbytes((k, tn), BF16)
    return pl.pallas_call(
        functools.partial(_mm_body, n_w=len(ws), n_e=len(extras), n_o=len(out_dtypes), epi=epi, cast_w=cast_w),
        grid=(n // tn, m // tm),
        in_specs=in_specs,
        out_specs=[pl.BlockSpec((tm, tn), lambda j, i: (i, j))] * len(out_dtypes),
        out_shape=[jax.ShapeDtypeStruct((m, n), dt) for dt in out_dtypes],
        scratch_shapes=scratch,
        compiler_params=_cparams(("parallel", "arbitrary" if cast_w else "parallel"), vm),
        name=name,
    )(x, *ws, *[a for a, _, _ in extras])


def _log_sigmoid(z):
    return jnp.minimum(z, 0.0) - jnp.log1p(jnp.exp(-jnp.abs(z)))


def _split_bf16(x, terms):
    parts = []
    for _ in range(terms):
        p = x.astype(BF16)
        parts.append(p)
        x = x - p.astype(F32)
    return parts


def _gla_decay_body(a_ref, awf_ref, awb_ref, abf_ref, abb_ref, ef_ref, eb_ref):
    c = a_ref.shape[0]
    a = a_ref[...]
    r = lax.broadcasted_iota(jnp.int32, (c, c), 0)
    s = lax.broadcasted_iota(jnp.int32, (c, c), 1)
    lower = (s <= r).astype(BF16)
    upper = (s >= r).astype(BF16)
    a_parts = _split_bf16(a, 2)

    def gate_logits(w_ref, b_ref):
        w_hi, w_lo = _split_bf16(w_ref[...], 2)
        z = jnp.dot(a_parts[0], w_hi, preferred_element_type=F32)
        z = z + jnp.dot(a_parts[1], w_hi, preferred_element_type=F32)
        z = z + jnp.dot(a_parts[0], w_lo, preferred_element_type=F32)
        return z + b_ref[...]

    def chunk_sums(tri, g):
        return sum(jnp.dot(tri, part, preferred_element_type=F32) for part in _split_bf16(g, 3))

    gf = _log_sigmoid(gate_logits(awf_ref, abf_ref)) * (1.0 / GLA_GATE_TEMP)
    gb = _log_sigmoid(gate_logits(awb_ref, abb_ref)) * (1.0 / GLA_GATE_TEMP)
    ef_ref[...] = chunk_sums(lower, gf)
    eb_ref[...] = chunk_sums(upper, gb)


def _gla_decay(a, awf, awb, abf, abb, chunk, tn=512):
    l = a.shape[0]
    n = awf.shape[1]
    tn = min(tn, n)
    col = pl.BlockSpec((a.shape[1], tn), lambda i, j: (0, j))
    row = pl.BlockSpec((1, tn), lambda i, j: (0, j))
    out = pl.BlockSpec((chunk, tn), lambda i, j: (i, j))
    vm = 4 * _nbytes((chunk, tn), F32) * 3 + 4 * _nbytes((a.shape[1], tn), F32)
    return pl.pallas_call(
        _gla_decay_body,
        grid=(l // chunk, n // tn),
        in_specs=[pl.BlockSpec((chunk, a.shape[1]), lambda i, j: (i, 0)), col, col, row, row],
        out_specs=[out, out],
        out_shape=[jax.ShapeDtypeStruct((l, n), F32)] * 2,
        compiler_params=_cparams(("parallel", "parallel"), vm),
        name="gla_decay",
    )(a, awf, awb, abf, abb)


def _dot_nt(a, b):
    return lax.dot_general(a, b, (((1,), (1,)), ((), ())), preferred_element_type=F32)


def _dot_tn(a, b):
    return lax.dot_general(a, b, (((0,), (0,)), ((), ())), preferred_element_type=F32)


def _bcast_rows(e, group, row):
    c, w = e.shape
    e3 = e.reshape(c // group, group, w)
    return jnp.broadcast_to(e3[:, row:row + 1, :], e3.shape).reshape(c, w)


def _gla_masks(c, base, fwd):
    i = lax.broadcasted_iota(jnp.int32, (c, c), 0)
    j = lax.broadcasted_iota(jnp.int32, (c, c), 1)
    sh = int(math.log2(base))
    order = (j <= i) if fwd else (j >= i)
    masks = [((i >> sh) == (j >> sh)) & order]
    s = base
    while 2 * s <= c:
        sh += 1
        masks.append((i >> sh) == (j >> sh))
        s *= 2
    return masks


def _gla_chunk_head(q, k, v, e, st, masks, *, fwd, base):
    c, dk = q.shape
    row = lax.broadcasted_iota(jnp.int32, (c, 1), 0)
    d0 = e - _bcast_rows(e, base, base // 2 - 1 if fwd else base // 2)
    q0 = (q * jnp.exp(d0)).astype(BF16)
    k0 = (k * jnp.exp(-d0)).astype(BF16)
    att = jnp.where(masks[0], _dot_nt(q0, k0), 0.0)
    s, lvl = base, 1
    while 2 * s <= c:
        d = e - _bcast_rows(e, 2 * s, s - 1 if fwd else s)
        later = ((row >> int(math.log2(s))) & 1) == (1 if fwd else 0)
        x = jnp.exp(jnp.where(later, d, -d))
        ql = jnp.where(later, q * x, 0.0).astype(BF16)
        kl = jnp.where(later, 0.0, k * x).astype(BF16)
        att = att + jnp.where(masks[lvl], _dot_nt(ql, kl), 0.0)
        s *= 2
        lvl += 1
    e_edge = e[c - 1:c] if fwd else e[0:1]
    qs = (q * jnp.exp(e)).astype(BF16)
    ks = (k * jnp.exp(e_edge - e)).astype(BF16)
    o = jnp.dot(att.astype(BF16), v, preferred_element_type=F32) + _dot_nt(qs, st.astype(BF16))
    st_new = st * jnp.exp(e_edge) + _dot_tn(v, ks)
    return o, st_new


def _gla_scan_body(*refs, fwd, base, combine):
    if combine:
        q_ref, k_ref, v_ref, e_ref, s0_ref, of_ref, sg_ref, gn_ref, o_ref, sfin_ref, st_ref = refs
    else:
        q_ref, k_ref, v_ref, e_ref, s0_ref, o_ref, sfin_ref, st_ref = refs
    step = pl.program_id(0)
    c = q_ref.shape[0]
    dk = q_ref.shape[1] // GLA_HEADS
    dv = v_ref.shape[1] // GLA_HEADS

    @pl.when(step == 0)
    def _():
        st_ref[...] = s0_ref[...]

    masks = _gla_masks(c, base, fwd)
    for h in range(GLA_HEADS):
        ks = slice(h * dk, (h + 1) * dk)
        vs = slice(h * dv, (h + 1) * dv)
        o, st_new = _gla_chunk_head(
            q_ref[:, ks].astype(F32), k_ref[:, ks].astype(F32), v_ref[:, vs], e_ref[:, ks], st_ref[h],
            masks, fwd=fwd, base=base)
        st_ref[h] = st_new
        if combine:
            t = o + of_ref[:, vs]
            t = t * lax.rsqrt(jnp.mean(t * t, axis=-1, keepdims=True) + EPS) * gn_ref[...]
            o_ref[:, vs] = (t * sg_ref[:, vs].astype(F32)).astype(o_ref.dtype)
        else:
            o_ref[:, vs] = o.astype(o_ref.dtype)

    @pl.when(step == pl.num_programs(0) - 1)
    def _():
        sfin_ref[...] = st_ref[...]


def _gla_scan(q, k, v, e, s0, *, fwd, chunk, base=32, combine=None):
    l, hdk = q.shape
    hdv = v.shape[1]
    dk, dv = hdk // GLA_HEADS, hdv // GLA_HEADS
    chunk = min(chunk, l)
    n = l // chunk
    idx = (lambda i: (i, 0)) if fwd else (lambda i: (n - 1 - i, 0))
    st_spec = pl.BlockSpec((GLA_HEADS, dv, dk), lambda i: (0, 0, 0))
    in_specs = [pl.BlockSpec((chunk, hdk), idx), pl.BlockSpec((chunk, hdk), idx),
                pl.BlockSpec((chunk, hdv), idx), pl.BlockSpec((chunk, hdk), idx), st_spec]
    args = [q, k, v, e, s0]
    vm = 2 * (2 * _nbytes((chunk, hdk), BF16) + _nbytes((chunk, hdv), BF16) + _nbytes((chunk, hdk), F32))
    vm += 3 * _nbytes((GLA_HEADS, dv, dk), F32) * 2 + 2 * _nbytes((chunk, hdv), F32)
    if combine is not None:
        o_other, gate, norm_g = combine
        in_specs += [pl.BlockSpec((chunk, hdv), idx), pl.BlockSpec((chunk, hdv), idx),
                     pl.BlockSpec((1, dv), lambda i: (0, 0))]
        args += [o_other, gate, norm_g.reshape(1, dv)]
        vm += 2 * (_nbytes((chunk, hdv), F32) + _nbytes((chunk, hdv), BF16))
    vm += 24 * _nbytes((chunk, max(dk, chunk)), F32)
    return pl.pallas_call(
        functools.partial(_gla_scan_body, fwd=fwd, base=base, combine=combine is not None),
        grid=(n,),
        in_specs=in_specs,
        out_specs=[pl.BlockSpec((chunk, hdv), idx), st_spec],
        out_shape=[jax.ShapeDtypeStruct((l, hdv), BF16 if combine is not None else F32),
                   jax.ShapeDtypeStruct((GLA_HEADS, dv, dk), F32)],
        scratch_shapes=[pltpu.VMEM((GLA_HEADS, dv, dk), F32)],
        compiler_params=_cparams(("arbitrary",), vm),
        name="gla_scan_fwd" if fwd else "gla_scan_bwd",
    )(*args)


def _shortconv_body(u0, u1, u2, p0, p1, p2, n0, n1, n2, w0, w1, w2, b0, b1, b2, x0_ref, z_ref):
    i = pl.program_id(0)
    last = pl.num_programs(0) - 1
    tm = u0.shape[0]
    row = lax.broadcasted_iota(jnp.int32, (tm, 1), 0)

    def conv(u_ref, p_ref, n_ref, w_ref, b_ref):
        u = u_ref[...].astype(F32)
        halo = p_ref.shape[0]
        prev_row = jnp.where(i == 0, 0.0, p_ref[...].astype(F32)[halo - 1:halo, :])
        next_row = jnp.where(i == last, 0.0, n_ref[...].astype(F32)[0:1, :])
        before = jnp.where(row == 0, prev_row, pltpu.roll(u, 1, axis=0))
        after = jnp.where(row == tm - 1, next_row, pltpu.roll(u, tm - 1, axis=0))
        return b_ref[...] + before * w_ref[0:1, :] + u * w_ref[1:2, :] + after * w_ref[2:3, :]

    x0_ref[...] = conv(u0, p0, n0, w0, b0).astype(x0_ref.dtype)
    z_ref[...] = conv(u1, p1, n1, w1, b1) * conv(u2, p2, n2, w2, b2)


def _hyena_shortconv(hy, conv_w, conv_b, tm=512, cb=512):
    l, w3 = hy.shape
    w = w3 // 3
    tm = min(tm, l)
    cb = min(cb, w)
    nb = w // cb
    halo = SUBLANES * 4 // jnp.dtype(hy.dtype).itemsize
    hb = tm // halo
    n_halo = l // halo
    cur = [pl.BlockSpec((tm, cb), lambda i, j, g=g: (i, g * nb + j)) for g in range(3)]
    prv = [pl.BlockSpec((halo, cb), lambda i, j, g=g: (jnp.maximum(i * hb - 1, 0), g * nb + j)) for g in range(3)]
    nxt = [pl.BlockSpec((halo, cb), lambda i, j, g=g: (jnp.minimum((i + 1) * hb, n_halo - 1), g * nb + j))
           for g in range(3)]
    wsp = [pl.BlockSpec((HY_SHORT, cb), lambda i, j, g=g: (0, g * nb + j)) for g in range(3)]
    bsp = [pl.BlockSpec((1, cb), lambda i, j, g=g: (0, g * nb + j)) for g in range(3)]
    out = pl.BlockSpec((tm, cb), lambda i, j: (i, j))
    vm = 2 * 5 * _nbytes((tm, cb), F32) + 8 * _nbytes((tm, cb), F32)
    return pl.pallas_call(
        _shortconv_body,
        grid=(l // tm, nb),
        in_specs=cur + prv + nxt + wsp + bsp,
        out_specs=[out, out],
        out_shape=[jax.ShapeDtypeStruct((l, w), BF16), jax.ShapeDtypeStruct((l, w), F32)],
        compiler_params=_cparams(("parallel", "parallel"), vm),
        name="hyena_shortconv",
    )(hy, hy, hy, hy, hy, hy, hy, hy, hy, conv_w, conv_w, conv_w,
      conv_b.reshape(1, w3), conv_b.reshape(1, w3), conv_b.reshape(1, w3))


def _filter_body(w1_ref, b1_ref, w2_ref, b2_ref, w3_ref, b3_ref, fr_ref, w4f_ref, w4b_ref,
                 hf_ref, hg_ref, ss_ref, *, seq_len):
    i = pl.program_id(0)
    tr = hf_ref.shape[0]
    wdt = hf_ref.shape[1]
    half = tr // 2
    side_w = LANES // 2
    lane = lax.broadcasted_iota(jnp.int32, (1, LANES), 1)
    local = lane & (side_w - 1)
    row = lax.broadcasted_iota(jnp.int32, (half, 1), 0) + i * tr
    pos = (row + jnp.where(lane >= side_w, half, 0)).astype(F32)
    last_pos = float(max(seq_len - 1, 1))
    band = ((local - 1) & (HY_EMB_BANDS - 1)).astype(F32)
    bands = 1e-4 + band * ((HY_EMB_BANDS - 1 - 1e-4) / (HY_EMB_BANDS - 1))
    ang = ((2.0 * math.pi / seq_len) * pos) * bands
    trig = jnp.cos(ang + jnp.where(local > HY_EMB_BANDS, 0.5 * math.pi, 0.0))
    emb = jnp.where(local == 0, pos / last_pos, jnp.where(local <= 2 * HY_EMB_BANDS, trig, 0.0))
    fr = fr_ref[...]
    h = jnp.sin(fr * (jnp.dot(emb, w1_ref[...], preferred_element_type=F32, precision=HIGHEST) + b1_ref[...]))
    h = jnp.sin(fr * (jnp.dot(h, w2_ref[...], preferred_element_type=F32, precision=HIGHEST) + b2_ref[...]))
    h = jnp.sin(fr * (jnp.dot(h, w3_ref[...], preferred_element_type=F32, precision=HIGHEST) + b3_ref[...]))
    hb = h.astype(BF16)
    ch = lax.broadcasted_iota(jnp.int32, (1, wdt), 1).astype(F32)
    lo = math.log(HY_DECAY_TARGET) / HY_SLOW_DECAY
    hi = math.log(HY_DECAY_TARGET) / HY_FAST_DECAY
    deltas = jnp.abs(lo + ch * ((hi - lo) / (wdt - 1)))
    ss = jnp.zeros((1, wdt), F32)
    for side in range(2):
        rows = slice(side * half, (side + 1) * half)
        p_side = (lax.broadcasted_iota(jnp.int32, (half, 1), 0) + i * tr + side * half).astype(F32)
        window = jnp.exp(-(p_side / last_pos) * deltas)
        hf = jnp.dot(hb, w4f_ref[side].astype(BF16), preferred_element_type=F32) * window
        hg = jnp.dot(hb, w4b_ref[side].astype(BF16), preferred_element_type=F32) * window
        hg = jnp.where(p_side == 0.0, 0.0, hg)
        hf_ref[rows, :] = hf
        hg_ref[rows, :] = hg
        ss = ss + jnp.sum(hf * hf + hg * hg, axis=0, keepdims=True)

    @pl.when(i == 0)
    def _():
        ss_ref[...] = jnp.zeros_like(ss_ref)

    ss_ref[...] += ss


def _pad2(a, rows, cols):
    return jnp.zeros((rows, cols), a.dtype).at[:a.shape[0], :a.shape[1]].set(a)


def _hyena_filter(seq_len, p, tr=256):
    wdt = p["hy_w4"].shape[1] // 2
    tr = min(tr, seq_len)
    side = LANES // 2
    hidden = p["hy_w2"].shape[0]
    assert hidden <= side and p["hy_w1"].shape[0] <= side and tr % (2 * SUBLANES) == 0

    def two_sided(w):
        blk = _pad2(w, side, side)
        zero = jnp.zeros_like(blk)
        return jnp.concatenate([jnp.concatenate([blk, zero], axis=1), jnp.concatenate([zero, blk], axis=1)], axis=0)

    def one_side(w, s):
        blk = _pad2(w, side, w.shape[1])
        zero = jnp.zeros_like(blk)
        return jnp.concatenate([blk, zero] if s == 0 else [zero, blk], axis=0)

    w1, w2, w3 = (two_sided(p[k]) for k in ("hy_w1", "hy_w2", "hy_w3"))
    b1, b2, b3, fr = (jnp.tile(_pad2(p[k].reshape(1, -1), 1, side), (1, 2))
                      for k in ("hy_b1", "hy_b2", "hy_b3", "hy_freq"))
    w4f = jnp.stack([one_side(p["hy_w4"][:, :wdt], s) for s in range(2)])
    w4b = jnp.stack([one_side(p["hy_w4"][:, wdt:], s) for s in range(2)])
    full = lambda a: pl.BlockSpec(a.shape, lambda i: (0,) * a.ndim)
    out = pl.BlockSpec((tr, wdt), lambda i: (i, 0))
    args = (w1, b1, w2, b2, w3, b3, fr, w4f, w4b)
    vm = 8 * _nbytes((LANES, wdt), F32) + 8 * _nbytes((tr, wdt), F32)
    return pl.pallas_call(
        functools.partial(_filter_body, seq_len=seq_len),
        grid=(seq_len // tr,),
        in_specs=[full(a) for a in args],
        out_specs=[out, out, pl.BlockSpec((1, wdt), lambda i: (0, 0))],
        out_shape=[jax.ShapeDtypeStruct((seq_len, wdt), F32)] * 2 + [jax.ShapeDtypeStruct((1, wdt), F32)],
        compiler_params=_cparams(("arbitrary",), vm),
        name="hyena_filter",
    )(*args)


def _fft_dims(seq_len):
    n = 2 * seq_len
    p = 1 << ((n.bit_length() - 1) // 2)
    return p, n // p


@functools.lru_cache(maxsize=None)
def _fft_consts(seq_len):
    pp, mm = _fft_dims(seq_len)
    n = pp * mm
    ph = pp // 2
    a = np.arange(ph)
    b = np.arange(mm)
    d = np.arange(pp)
    eye = np.eye(SUBLANES)
    ang = 2 * np.pi * np.outer(d, a) / pp
    fk = np.kron(np.concatenate([np.cos(ang), -np.sin(ang)], axis=0), eye)
    bt = b.reshape(mm // SUBLANES, 1, SUBLANES)
    angt = (2 * np.pi * d[None, :, None] * bt / n).reshape(mm // SUBLANES, pp * SUBLANES, 1)
    angc = -2 * np.pi * np.outer(b, b) / mm
    cr, ci = np.cos(angc), np.sin(angc)
    w2 = np.block([[cr, -ci], [ci, cr]])
    v2 = np.block([[cr, ci], [-ci, cr]])
    gk = np.kron(np.concatenate([np.cos(ang.T), -np.sin(ang.T)], axis=1) / n, eye)
    as_bf16 = lambda x: np.asarray(x, dtype=np.float32).astype(BF16)
    return dict(fk=as_bf16(fk), w2=as_bf16(w2), v2=as_bf16(v2), gk=as_bf16(gk),
                twc=np.cos(angt).astype(np.float32), tws=np.sin(angt).astype(np.float32), pp=pp, mm=mm)


def _stage1_to_scratch(a_ref, fk_ref, twc_ref, tws_ref, group_rows, t):
    pp = a_ref.shape[0] // 2
    groups = twc_ref.shape[0]
    for g in range(groups):
        zt = group_rows(g)
        r = jnp.dot(fk_ref[...], zt, preferred_element_type=F32)
        rr, ri = r[:pp * SUBLANES], r[pp * SUBLANES:]
        c, s = twc_ref[g], tws_ref[g]
        a_ref[:pp, t * groups + g] = (rr * c + ri * s).reshape(pp, SUBLANES, zt.shape[-1])
        a_ref[pp:, t * groups + g] = (ri * c - rr * s).reshape(pp, SUBLANES, zt.shape[-1])


def _group_of_8(ref, g):
    blk = ref[:, g * SUBLANES:(g + 1) * SUBLANES, :]
    return blk.reshape(blk.shape[0] * SUBLANES, blk.shape[-1])


def _scratch_rows(a_ref, d):
    pp = a_ref.shape[0] // 2
    mm = a_ref.shape[1] * SUBLANES
    lanes = a_ref.shape[-1]
    return jnp.concatenate([a_ref[d].reshape(mm, lanes), a_ref[pp + d].reshape(mm, lanes)], axis=0).astype(BF16)


def _bf16_bits(x):
    u = lax.bitcast_convert_type(x, jnp.uint32)
    u = u + jnp.uint32(0x7FFF) + ((u >> 16) & jnp.uint32(1))
    return u & jnp.uint32(0xFFFF0000)


def _pack_ri(re, im):
    return _bf16_bits(re) | (_bf16_bits(im) >> 16)


def _unpack_ri(word):
    re = lax.bitcast_convert_type(word & jnp.uint32(0xFFFF0000), F32)
    im = lax.bitcast_convert_type(word << 16, F32)
    return re, im


def _fft_spec_body(hf_ref, hg_ref, fk_ref, twc_ref, tws_ref, w2_ref, o_ref, a_ref, *, nb):
    t = pl.program_id(1)
    cb = hf_ref.shape[-1]
    mm = a_ref.shape[1] * SUBLANES

    @pl.when(t < nb)
    def _():
        rows = lambda g: jnp.concatenate([_group_of_8(hf_ref, g), _group_of_8(hg_ref, g)], axis=-1).astype(BF16)
        _stage1_to_scratch(a_ref, fk_ref, twc_ref, tws_ref, rows, t)

    @pl.when(t >= nb)
    def _():
        nd_step = o_ref.shape[0]
        for dl in range(nd_step):
            h = jnp.dot(w2_ref[...], _scratch_rows(a_ref, (t - nb) * nd_step + dl), preferred_element_type=F32)
            o_ref[dl, :mm, :] = (h[:mm, :cb] + h[:mm, cb:]).astype(o_ref.dtype)
            o_ref[dl, mm:, :] = (h[mm:, :cb] - h[mm:, cb:]).astype(o_ref.dtype)


def _phase_specs(k, cb, sb, nb):
    pp = k["pp"]
    ph = pp // 2
    step = lambda t: jnp.minimum(t, nb - 1)
    return (pl.BlockSpec((ph, sb, cb), lambda j, t: (0, step(t), j)),
            pl.BlockSpec(k["fk"].shape, lambda j, t: (0, 0)),
            pl.BlockSpec((sb // SUBLANES, pp * SUBLANES, 1), lambda j, t: (step(t), 0, 0)))


FFT_B_PER_STEP = 16
FFT_D_PER_STEP = 16


def _fft_filter_spectrum(hf, hg, k, cb=LANES):
    l, c = hf.shape
    pp, mm = k["pp"], k["mm"]
    ph = pp // 2
    cb = min(cb, c)
    sb, ds = min(2 * FFT_B_PER_STEP, mm), min(2 * FFT_D_PER_STEP, pp)
    nb, nd = mm // sb, pp // ds
    taps, fk_spec, tw_spec = _phase_specs(k, cb, sb, nb)
    vm = (_nbytes((2 * pp, mm // SUBLANES, SUBLANES, 2 * cb), F32) + 4 * _nbytes((ph, sb, cb), F32)
          + 2 * _nbytes(k["fk"].shape, BF16) + 2 * _nbytes((ds, 2 * mm, cb), BF16)
          + 5 * _nbytes((2 * pp * SUBLANES, 2 * cb), F32))
    return pl.pallas_call(
        functools.partial(_fft_spec_body, nb=nb),
        grid=(c // cb, nb + nd),
        in_specs=[taps, taps, fk_spec, tw_spec, tw_spec, pl.BlockSpec(k["w2"].shape, lambda j, t: (0, 0))],
        out_specs=pl.BlockSpec((ds, 2 * mm, cb), lambda j, t: (jnp.maximum(t - nb, 0), 0, j)),
        out_shape=jax.ShapeDtypeStruct((pp, 2 * mm, c), BF16),
        scratch_shapes=[pltpu.VMEM((2 * pp, mm // SUBLANES, SUBLANES, 2 * cb), F32)],
        compiler_params=_cparams(("parallel", "arbitrary"), vm),
        name="fft_filter_spectrum",
    )(hf.reshape(ph, mm, c), hg.reshape(ph, mm, c), k["fk"], k["twc"], k["tws"], k["w2"])


def _fft_mid_body(z_ref, fk_ref, twc_ref, tws_ref, h_ref, w2_ref, v2_ref, o_ref, a_ref, *, nb):
    t = pl.program_id(1)
    mm = a_ref.shape[1] * SUBLANES

    @pl.when(t < nb)
    def _():
        _stage1_to_scratch(a_ref, fk_ref, twc_ref, tws_ref, lambda g: _group_of_8(z_ref, g).astype(BF16), t)

    @pl.when(t >= nb)
    def _():
        nd_step = o_ref.shape[0]
        for dl in range(nd_step):
            x = jnp.dot(w2_ref[...], _scratch_rows(a_ref, (t - nb) * nd_step + dl), preferred_element_type=F32)
            xr, xi = x[:mm], x[mm:]
            hr, hi = h_ref[dl, :mm, :].astype(F32), h_ref[dl, mm:, :].astype(F32)
            y = jnp.concatenate([xr * hr - xi * hi, xr * hi + xi * hr], axis=0).astype(BF16)
            bd = jnp.dot(v2_ref[...], y, preferred_element_type=F32)
            o_ref[dl] = _pack_ri(bd[:mm], bd[mm:])


def _fft_mid(z, h, k, cb=256):
    l, c = z.shape
    pp, mm = k["pp"], k["mm"]
    cb = min(cb, c)
    sb, ds = min(FFT_B_PER_STEP, mm), min(FFT_D_PER_STEP, pp)
    nb, nd = mm // sb, pp // ds
    sig, fk_spec, tw_spec = _phase_specs(k, cb, sb, nb)
    const = pl.BlockSpec(k["w2"].shape, lambda j, t: (0, 0))
    dstep = lambda t: jnp.maximum(t - nb, 0)
    vm = (_nbytes((2 * pp, mm // SUBLANES, SUBLANES, cb), F32) + 2 * _nbytes((pp // 2, sb, cb), F32)
          + 2 * _nbytes(k["fk"].shape, BF16) + 2 * _nbytes((ds, 2 * mm, cb), BF16)
          + 2 * _nbytes((ds, mm, cb), F32) + 5 * _nbytes((2 * pp * SUBLANES, cb), F32))
    return pl.pallas_call(
        functools.partial(_fft_mid_body, nb=nb),
        grid=(c // cb, nb + nd),
        in_specs=[sig, fk_spec, tw_spec, tw_spec,
                  pl.BlockSpec((ds, 2 * mm, cb), lambda j, t: (dstep(t), 0, j)), const, const],
        out_specs=pl.BlockSpec((ds, mm, cb), lambda j, t: (dstep(t), 0, j)),
        out_shape=jax.ShapeDtypeStruct((pp, mm, c), jnp.uint32),
        scratch_shapes=[pltpu.VMEM((2 * pp, mm // SUBLANES, SUBLANES, cb), F32)],
        compiler_params=_cparams(("parallel", "arbitrary"), vm),
        name="fft_mid",
    )(z.reshape(pp // 2, mm, c), k["fk"], k["twc"], k["tws"], h, k["w2"], k["v2"])


def _fft_last_body(b_ref, gk_ref, twc_ref, tws_ref, x0_ref, z_ref, ss_ref, bias_ref, o_ref):
    scale = lax.rsqrt(ss_ref[...] + EPS)
    ys = []
    for g in range(twc_ref.shape[0]):
        br, bi = _unpack_ri(_group_of_8(b_ref, g))
        c, s = twc_ref[g], tws_ref[g]
        rhs = jnp.concatenate([br * c - bi * s, br * s + bi * c], axis=0).astype(BF16)
        y = jnp.dot(gk_ref[...], rhs, preferred_element_type=F32)
        ys.append(y.reshape(o_ref.shape[0], SUBLANES, o_ref.shape[2]))
    y = jnp.concatenate(ys, axis=1)
    o_ref[...] = (x0_ref[...].astype(F32) * (y * scale + z_ref[...] * bias_ref[...])).astype(o_ref.dtype)


def _fft_last(bmat, k, x0, z, ss, bias, cb=256):
    pp, mm, c = bmat.shape
    ph = pp // 2
    cb = min(cb, c)
    sb = min(FFT_B_PER_STEP, mm)
    view = pl.BlockSpec((ph, sb, cb), lambda j, b: (0, b, j))
    row = pl.BlockSpec((1, cb), lambda j, b: (0, j))
    tw_spec = pl.BlockSpec((sb // SUBLANES, pp * SUBLANES, 1), lambda j, b: (b, 0, 0))
    vm = 2 * (_nbytes((pp, sb, cb), F32) + 3 * _nbytes((ph, sb, cb), F32)
              + _nbytes(k["gk"].shape, BF16)) + 6 * _nbytes((2 * pp * SUBLANES, cb), F32)
    out = pl.pallas_call(
        _fft_last_body,
        grid=(c // cb, mm // sb),
        in_specs=[pl.BlockSpec((pp, sb, cb), lambda j, b: (0, b, j)),
                  pl.BlockSpec(k["gk"].shape, lambda j, b: (0, 0)),
                  tw_spec, tw_spec, view, view, row, row],
        out_specs=view,
        out_shape=jax.ShapeDtypeStruct((ph, mm, c), BF16),
        compiler_params=_cparams(("parallel", "parallel"), vm),
        name="fft_last",
    )(bmat, k["gk"], k["twc"], k["tws"], x0.reshape(ph, mm, c), z.reshape(ph, mm, c), ss, bias.reshape(1, c))
    return out.reshape(ph * mm, c)


def _hyena_longconv(x0, z, hf, hg, ss, bias):
    k = _fft_consts(z.shape[0])
    h = _fft_filter_spectrum(hf, hg, k)
    return _fft_last(_fft_mid(z, h, k), k, x0, z, ss, bias)


def _router_body(h_ref, w_ref, o_ref):
    logits = jnp.dot(h_ref[...], w_ref[...], preferred_element_type=F32, precision=HIGHEST)
    lane = lax.broadcasted_iota(jnp.int32, logits.shape, 1)
    lg = jnp.where(lane < N_EXPERTS, logits, -jnp.inf)
    m1 = jnp.max(lg, axis=-1, keepdims=True)
    i1 = jnp.min(jnp.where(lg == m1, lane, LANES), axis=-1, keepdims=True)
    l2 = jnp.where(lane == i1, -jnp.inf, lg)
    m2 = jnp.max(l2, axis=-1, keepdims=True)
    i2 = jnp.min(jnp.where(l2 == m2, lane, LANES), axis=-1, keepdims=True)
    e = jnp.exp(m2 - m1)
    w1 = 1.0 / (1.0 + e)
    w2 = e * w1
    o_ref[...] = jnp.where(lane == 0, i1.astype(F32), jnp.where(lane == 1, i2.astype(F32),
                           jnp.where(lane == 2, w1, jnp.where(lane == 3, w2, 0.0))))


def _router(h, router_w, tm=256):
    m, d = h.shape
    tm = min(tm, m)
    w = _pad2(router_w, d, LANES)
    vm = 2 * (_nbytes((tm, d), F32) + _nbytes((d, LANES), F32)) + 8 * _nbytes((tm, LANES), F32)
    return pl.pallas_call(
        _router_body,
        grid=(m // tm,),
        in_specs=[pl.BlockSpec((tm, d), lambda i: (i, 0)), pl.BlockSpec((d, LANES), lambda i: (0, 0))],
        out_specs=pl.BlockSpec((tm, LANES), lambda i: (i, 0)),
        out_shape=jax.ShapeDtypeStruct((m, LANES), F32),
        compiler_params=_cparams(("parallel",), vm),
        name="moe_router",
    )(h, w)


def _routing_tables(route, tile):
    t = route.shape[0]
    e_flat = route[:, :TOP_K].astype(jnp.int32).reshape(-1)
    w_flat = route[:, TOP_K:2 * TOP_K].reshape(-1)
    onehot = (e_flat[:, None] == jnp.arange(N_EXPERTS, dtype=jnp.int32)[None, :]).astype(jnp.int32)
    csum = jnp.cumsum(onehot, axis=0)
    rank = jnp.take_along_axis(csum, e_flat[:, None], axis=1)[:, 0] - 1
    counts = csum[-1]
    padded = ((counts + tile - 1) // tile) * tile
    ends = jnp.cumsum(padded)
    pos = (ends - padded)[e_flat] + rank
    n_tiles = (t * TOP_K) // tile + N_EXPERTS
    rows = n_tiles * tile
    token = jnp.arange(t * TOP_K, dtype=jnp.int32) // TOP_K
    w_bits = lax.bitcast_convert_type(w_flat, jnp.int32)
    table = jnp.zeros((rows, 2), jnp.int32).at[pos].set(jnp.stack([token, w_bits], axis=1))
    row_token = table[:, 0]
    row_w = lax.bitcast_convert_type(table[:, 1], F32)
    start = jnp.arange(n_tiles, dtype=jnp.int32) * tile
    valid = start < ends[-1]
    expert = jnp.minimum(jnp.sum((start[:, None] >= ends[None, :]).astype(jnp.int32), axis=1), N_EXPERTS - 1)
    last_valid = jnp.max(jnp.where(valid, expert, 0))
    expert = jnp.where(valid, expert, last_valid)
    changed = jnp.concatenate([jnp.ones((1,), bool), expert[1:] != expert[:-1]])
    is_start = changed & valid
    idx = jnp.arange(n_tiles, dtype=jnp.int32)
    later_start = jnp.where(is_start[None, :] & (idx[None, :] > idx[:, None]), idx[None, :], n_tiles)
    nxt = jnp.min(later_start, axis=1)
    wrap = (nxt == n_tiles).astype(jnp.int32)
    next_expert = expert[jnp.where(nxt == n_tiles, 0, nxt)]
    tiles = (expert, is_start.astype(jnp.int32), valid.astype(jnp.int32), next_expert, wrap)
    return row_token, row_w, pos.astype(jnp.int32), tiles


def _row_copy(src_hbm, dst, sem, src_row, dst_row):
    return pltpu.make_async_copy(src_hbm.at[pl.ds(src_row, 1)], dst.at[pl.ds(dst_row, 1)], sem)


DMA_LOOP_UNROLL = 8
DMA_PRIORITIES = 2


def _gather_rows_body(tok_ref, h_hbm, o_ref, buf, sems):
    i = pl.program_id(0)
    gt = buf.shape[1]
    slot = i % 2

    def issue(step, s):
        def body(pair, carry):
            for prio in range(DMA_PRIORITIES):
                r = DMA_PRIORITIES * pair + prio
                _row_copy(h_hbm, buf.at[s], sems.at[s], tok_ref[step * gt + r], r).start(priority=prio)
            return carry
        lax.fori_loop(0, gt // DMA_PRIORITIES, body, 0, unroll=DMA_LOOP_UNROLL // DMA_PRIORITIES)

    def drain(s):
        def body(r, carry):
            _row_copy(h_hbm, buf.at[s], sems.at[s], 0, r).wait()
            return carry
        lax.fori_loop(0, gt, body, 0, unroll=DMA_LOOP_UNROLL)

    @pl.when(i == 0)
    def _():
        issue(0, 0)

    @pl.when(i + 1 < pl.num_programs(0))
    def _():
        issue(i + 1, 1 - slot)

    drain(slot)
    first_half, second_half = _unpack_ri(buf[slot])
    o_ref[...] = jnp.concatenate([first_half, second_half], axis=-1).astype(o_ref.dtype)


def _gather_rows(h_pairs, row_token, gt=256):
    rows = row_token.shape[0]
    dh = h_pairs.shape[1]
    vm = 6 * _nbytes((gt, dh), F32)
    return pl.pallas_call(
        _gather_rows_body,
        grid_spec=pltpu.PrefetchScalarGridSpec(
            num_scalar_prefetch=1,
            grid=(rows // gt,),
            in_specs=[pl.BlockSpec(memory_space=pl.ANY)],
            out_specs=pl.BlockSpec((gt, 2 * dh), lambda i, tok: (i, 0)),
            scratch_shapes=[pltpu.VMEM((2, gt, dh), jnp.uint32), pltpu.SemaphoreType.DMA((2,))],
        ),
        out_shape=jax.ShapeDtypeStruct((rows, 2 * dh), BF16),
        compiler_params=_cparams(("arbitrary",), vm),
        name="moe_gather_rows",
    )(row_token, h_pairs)


def _group_weights(tiles, w_hbms, w_f32s, w_bf16s, sems):
    te_ref, ts_ref, _, ne_ref, wrap_ref = tiles
    j, i = pl.program_id(0), pl.program_id(1)
    tn = w_f32s[0].shape[1]

    def copies(expert, col_block):
        col = pl.multiple_of(col_block * tn, tn)
        return [pltpu.make_async_copy(w.at[expert, :, pl.ds(col, tn)], buf, sems.at[n])
                for n, (w, buf) in enumerate(zip(w_hbms, w_f32s))]

    @pl.when(ts_ref[i] == 1)
    def _():
        @pl.when((j == 0) & (i == 0))
        def _():
            for cp in copies(te_ref[i], j):
                cp.start()

        for cp in copies(te_ref[i], j):
            cp.wait()
        for src, dst in zip(w_f32s, w_bf16s):
            dst[...] = src[...].astype(BF16)
        nj = j + wrap_ref[i]

        @pl.when(nj < pl.num_programs(0))
        def _():
            for cp in copies(ne_ref[i], nj):
                cp.start()


def _moe_up_body(te, ts, tv, ne, wrap, x_ref, wg_hbm, wu_hbm, o_ref, wg32, wu32, wg_bf, wu_bf, sems):
    i = pl.program_id(1)
    _group_weights((te, ts, tv, ne, wrap), (wg_hbm, wu_hbm), (wg32, wu32), (wg_bf, wu_bf), sems)

    @pl.when(tv[i] == 1)
    def _():
        x = x_ref[...]
        g = jnp.dot(x, wg_bf[...], preferred_element_type=F32)
        u = jnp.dot(x, wu_bf[...], preferred_element_type=F32)
        o_ref[...] = (g * jax.nn.sigmoid(g) * u).astype(o_ref.dtype)

    @pl.when(tv[i] == 0)
    def _():
        o_ref[...] = jnp.zeros_like(o_ref)


def _moe_up(xs, wg, wu, tiles, tile, tn=1024):
    rows, d = xs.shape
    f = wg.shape[2]
    vm = 2 * _nbytes((d, tn), F32) + 2 * _nbytes((d, tn), BF16) + 2 * _nbytes((tile, d), BF16) + 6 * _nbytes((tile, tn), F32)
    return pl.pallas_call(
        _moe_up_body,
        grid_spec=pltpu.PrefetchScalarGridSpec(
            num_scalar_prefetch=len(tiles),
            grid=(f // tn, rows // tile),
            in_specs=[pl.BlockSpec((tile, d), lambda j, i, *_: (i, 0)),
                      pl.BlockSpec(memory_space=pl.ANY), pl.BlockSpec(memory_space=pl.ANY)],
            out_specs=pl.BlockSpec((tile, tn), lambda j, i, *_: (i, j)),
            scratch_shapes=[pltpu.VMEM((d, tn), F32), pltpu.VMEM((d, tn), F32),
                            pltpu.VMEM((d, tn), BF16), pltpu.VMEM((d, tn), BF16),
                            pltpu.SemaphoreType.DMA((2,))],
        ),
        out_shape=jax.ShapeDtypeStruct((rows, f), BF16),
        compiler_params=_cparams(("arbitrary", "arbitrary"), vm),
        name="moe_up",
    )(*tiles, xs, wg, wu)


def _moe_down_body(te, ts, tv, ne, wrap, a_ref, wd_hbm, rw_ref, o_ref, wd32, wd_bf, sems):
    i = pl.program_id(1)
    _group_weights((te, ts, tv, ne, wrap), (wd_hbm,), (wd32,), (wd_bf,), sems)

    @pl.when(tv[i] == 1)
    def _():
        o_ref[...] = jnp.dot(a_ref[...], wd_bf[...], preferred_element_type=F32) * rw_ref[...]

    @pl.when(tv[i] == 0)
    def _():
        o_ref[...] = jnp.zeros_like(o_ref)


def _moe_down(act, wd, row_w, tiles, tile, tn=512):
    rows, f = act.shape
    d = wd.shape[2]
    vm = _nbytes((f, tn), F32) + _nbytes((f, tn), BF16) + 2 * _nbytes((tile, f), BF16) + 4 * _nbytes((tile, tn), F32)
    return pl.pallas_call(
        _moe_down_body,
        grid_spec=pltpu.PrefetchScalarGridSpec(
            num_scalar_prefetch=len(tiles),
            grid=(d // tn, rows // tile),
            in_specs=[pl.BlockSpec((tile, f), lambda j, i, *_: (i, 0)),
                      pl.BlockSpec(memory_space=pl.ANY),
                      pl.BlockSpec((tile, 1), lambda j, i, *_: (i, 0))],
            out_specs=pl.BlockSpec((tile, tn), lambda j, i, *_: (i, j)),
            scratch_shapes=[pltpu.VMEM((f, tn), F32), pltpu.VMEM((f, tn), BF16), pltpu.SemaphoreType.DMA((1,))],
        ),
        out_shape=jax.ShapeDtypeStruct((rows, d), F32),
        compiler_params=_cparams(("arbitrary", "arbitrary"), vm),
        name="moe_down",
    )(*tiles, act, wd, row_w.reshape(rows, 1))


def _moe_combine_body(pos_ref, ys_hbm, x_ref, gate_ref, g_ref, o_ref, buf, sems, *, final_norm):
    i = pl.program_id(0)
    gt = x_ref.shape[0]
    slot = i % 2

    def issue(step, s):
        def body(r, carry):
            for k in range(TOP_K):
                _row_copy(ys_hbm, buf.at[s, k], sems.at[s], pos_ref[TOP_K * (step * gt + r) + k], r).start(
                    priority=k % DMA_PRIORITIES)
            return carry
        lax.fori_loop(0, gt, body, 0, unroll=DMA_LOOP_UNROLL)

    def drain(s):
        def body(r, carry):
            for k in range(TOP_K):
                _row_copy(ys_hbm, buf.at[s, k], sems.at[s], 0, r).wait()
            return carry
        lax.fori_loop(0, gt, body, 0, unroll=DMA_LOOP_UNROLL)

    @pl.when(i == 0)
    def _():
        issue(0, 0)

    @pl.when(i + 1 < pl.num_programs(0))
    def _():
        issue(i + 1, 1 - slot)

    drain(slot)
    y = buf[slot, 0]
    for k in range(1, TOP_K):
        y = y + buf[slot, k]
    x = x_ref[...] + gate_ref[...] * y
    if final_norm:
        x = x * lax.rsqrt(jnp.mean(x * x, axis=-1, keepdims=True) + EPS) * g_ref[...]
    o_ref[...] = x


def _moe_combine(ys, pos, x, gate, norm_g, final_norm, gt=128):
    t, d = x.shape
    gt = min(gt, t)
    row = pl.BlockSpec((1, d), lambda i, p: (0, 0))
    tilespec = pl.BlockSpec((gt, d), lambda i, p: (i, 0))
    vm = (2 * TOP_K + 6) * _nbytes((gt, d), F32)
    return pl.pallas_call(
        functools.partial(_moe_combine_body, final_norm=final_norm),
        grid_spec=pltpu.PrefetchScalarGridSpec(
            num_scalar_prefetch=1,
            grid=(t // gt,),
            in_specs=[pl.BlockSpec(memory_space=pl.ANY), tilespec, row, row],
            out_specs=tilespec,
            scratch_shapes=[pltpu.VMEM((2, TOP_K, gt, d), F32), pltpu.SemaphoreType.DMA((2,))],
        ),
        out_shape=jax.ShapeDtypeStruct((t, d), F32),
        compiler_params=_cparams(("arbitrary",), vm),
        name="moe_combine",
    )(pos, ys, x, gate.reshape(1, d), norm_g.reshape(1, d))


MOE_ROW_TILE = 512


def _moe_ffn(h32, h_pairs, x, gate, p, norm_g, final_norm):
    route = _router(h32, p["router_w"])
    row_token, row_w, pos, tiles = _routing_tables(route, MOE_ROW_TILE)
    xs = _gather_rows(h_pairs, row_token)
    act = _moe_up(xs, p["exp_w_gate"], p["exp_w_up"], tiles, MOE_ROW_TILE)
    ys = _moe_down(act, p["exp_w_down"], row_w, tiles, MOE_ROW_TILE)
    return _moe_combine(ys, pos, x, gate, norm_g, final_norm)


def _silu(v):
    return v * jax.nn.sigmoid(v)


def _in_weights(w_in_all, layer, d):
    qk, vw, r = d // 2, d, GLA_GATE_RANK
    col_v = qk
    col_a = col_v + vw
    col_q = col_a + 2 * r
    col_g = col_q + qk
    col_hy = col_g + vw
    col_gate = col_hy + 3 * d
    return dict(wt=jnp.swapaxes(w_in_all, 1, 2), layer=layer, k=(0, qk), v=(col_v, vw), a=(col_a, 2 * r),
                q=(col_q, qk), g=(col_g, vw), hy=(col_hy, 3 * d), gate=(col_gate, 2 * d))


def _proj_body(x_ref, wt_hbm, o_ref, w32, wbf, sem, *, layer, row0, epi):
    j, i = pl.program_id(0), pl.program_id(1)
    rows = w32.shape[0]

    def fetch(block):
        start = pl.multiple_of(row0 + block * rows, SUBLANES)
        return pltpu.make_async_copy(wt_hbm.at[layer, pl.ds(start, rows), :], w32, sem)

    @pl.when(i == 0)
    def _():
        @pl.when(j == 0)
        def _():
            fetch(j).start()

        fetch(j).wait()
        if rows == wbf.shape[0]:
            wbf[...] = w32[...].astype(BF16)
        else:
            wbf[...] = jnp.zeros_like(wbf)
            wbf[:rows, :] = w32[...].astype(BF16)

        @pl.when(j + 1 < pl.num_programs(0))
        def _():
            fetch(j + 1).start()

    o_ref[...] = epi(_dot_nt(x_ref[...], wbf[...])).astype(o_ref.dtype)


def _proj(h, wi, group, epi, out_dtype, tm=1024, tn=1024):
    m, d = h.shape
    col, width = wi[group]
    rows = min(tn, width)
    tn = max(rows, LANES)
    tm = min(tm, m)
    assert m % tm == 0 and width % rows == 0 and col % SUBLANES == 0 and rows % SUBLANES == 0
    n_out = width // rows * tn
    vm = _nbytes((rows, d), F32) + _nbytes((tn, d), BF16) + 2 * _nbytes((tm, d), BF16) + 4 * _nbytes((tm, tn), F32)
    out, = pl.pallas_call(
        functools.partial(_proj_body, layer=wi["layer"], row0=col, epi=epi),
        grid=(width // rows, m // tm),
        in_specs=[pl.BlockSpec((tm, d), lambda j, i: (i, 0)), pl.BlockSpec(memory_space=pl.ANY)],
        out_specs=[pl.BlockSpec((tm, tn), lambda j, i: (i, j))],
        out_shape=[jax.ShapeDtypeStruct((m, n_out), out_dtype)],
        scratch_shapes=[pltpu.VMEM((rows, d), F32), pltpu.VMEM((tn, d), BF16), pltpu.SemaphoreType.DMA(())],
        compiler_params=_cparams(("arbitrary", "arbitrary"), vm),
        name="proj_" + group,
    )(h, wi["wt"])
    return out


def _gate_matrices(p, d):
    r = GLA_GATE_RANK
    qk = d // 2
    awf = jnp.zeros((LANES, qk), F32).at[:r].set(p["gla_aw_f"])
    awb = jnp.zeros((LANES, qk), F32).at[r:2 * r].set(p["gla_aw_b"])
    return awf, awb, p["gla_ab_f"].reshape(1, qk), p["gla_ab_b"].reshape(1, qk)


GLA_CHUNK = 256


def _gla_states_only(h, wi, p, d):
    ident = lambda acc: acc
    k = _proj(h, wi, "k", ident, BF16)
    v = _proj(h, wi, "v", ident, BF16)
    a = _proj(h, wi, "a", ident, F32)
    ef, eb = _gla_decay(a, *_gate_matrices(p, d), chunk=min(GLA_CHUNK, h.shape[0]))
    dk, dv = d // 2 // GLA_HEADS, d // GLA_HEADS
    zero = jnp.zeros((GLA_HEADS, dv, dk), F32)
    _, s_f = _gla_scan(k, k, v, ef, zero, fwd=True, chunk=GLA_CHUNK)
    _, s_b = _gla_scan(k, k, v, eb, zero, fwd=False, chunk=GLA_CHUNK)
    return s_f, s_b


def _token_mixer(x, h, mod_gate, s0_f, s0_b, wi, p, d):
    l = h.shape[0]
    dk = d // 2 // GLA_HEADS
    ident = lambda acc: acc
    k = _proj(h, wi, "k", ident, BF16)
    v = _proj(h, wi, "v", ident, BF16)
    a = _proj(h, wi, "a", ident, F32)
    q = _proj(h, wi, "q", lambda acc: acc * (dk ** -0.5), BF16)
    sg = _proj(h, wi, "g", _silu, BF16)
    hy = _proj(h, wi, "hy", ident, BF16)
    gates = _proj(h, wi, "gate", jax.nn.sigmoid, BF16)

    ef, eb = _gla_decay(a, *_gate_matrices(p, d), chunk=min(GLA_CHUNK, l))
    o_f, s_f = _gla_scan(q, k, v, ef, s0_f, fwd=True, chunk=GLA_CHUNK)
    o_gla, s_b = _gla_scan(q, k, v, eb, s0_b, fwd=False, chunk=GLA_CHUNK, combine=(o_f, sg, p["gla_norm_g"]))

    x0, z = _hyena_shortconv(hy, p["hy_conv_w"], p["hy_conv_b"])
    hf, hg, ss = _hyena_filter(l, p)
    o_hy = _hyena_longconv(x0, z, hf, hg, ss, p["hy_bias"])

    layer = wi["layer"]
    t1, = _mm(o_hy, [wi["w_up_hy"]], lambda accs, ex: [ex[0].astype(F32) * accs[0]], (F32,),
              extras=[(gates, "tile", 0)], w_layer=layer, name="up_hy")
    merged, = _mm(o_gla, [wi["w_up_gla"]], lambda accs, ex: [ex[1] + ex[0].astype(F32) * accs[0]], (BF16,),
                  extras=[(gates, "tile", d), (t1, "tile", 0)], w_layer=layer, name="up_gla_merge")
    x_new, = _mm(merged, [wi["w_out"]], lambda accs, ex: [ex[0] + ex[1] * accs[0]], (F32,),
                 extras=[(x, "tile", 0), (mod_gate.reshape(1, d), "row", 0)], w_layer=layer, name="out_proj")
    return x_new, s_f, s_b


def _dense_ffn(x, h, mod_gate, wg, wu, wd):
    d, f = wg.shape
    fp = -(-f // 512) * 512
    wg_b = _pad2(wg.astype(BF16), d, fp)
    wu_b = _pad2(wu.astype(BF16), d, fp)
    wd_b = _pad2(wd.astype(BF16), fp, d)
    act, = _mm(h, [wg_b, wu_b], lambda accs, ex: [_silu(accs[0]) * accs[1]], (BF16,), tn=512, name="ffn_up")
    x_new, = _mm(act, [wd_b], lambda accs, ex: [ex[0] + ex[1] * accs[0]], (F32,), tm=512,
                 extras=[(x, "tile", 0), (mod_gate.reshape(1, d), "row", 0)], name="ffn_down")
    return x_new


def kernel(x, c, ctx, c_ctx, ada_w, ada_b, norm_mix_g, norm_ffn_g, w_in, hy_conv_w, hy_conv_b, hy_w1, hy_b1, hy_w2, hy_b2, hy_w3, hy_b3, hy_w4, hy_freq, hy_bias, gla_aw_f, gla_ab_f, gla_aw_b, gla_ab_b, gla_norm_g, w_up_hy, w_up_gla, w_out, ffn_w_gate, ffn_w_up, ffn_w_down, router_w, exp_w_gate, exp_w_up, exp_w_down, final_norm_g):
    assert x.shape[0] == 1 and c.shape[0] == 1, "batch size 1 only"
    depth, d = norm_mix_g.shape
    x_lat, x_ctx = x[0], ctx[0]
    mods = _ada_modulation(jnp.concatenate([c, c_ctx.reshape(1, d)], axis=0), ada_w, ada_b)
    dk, dv = d // 2 // GLA_HEADS, d // GLA_HEADS
    per_layer = dict(hy_conv_w=hy_conv_w, hy_conv_b=hy_conv_b, hy_w1=hy_w1, hy_b1=hy_b1, hy_w2=hy_w2, hy_b2=hy_b2,
                     hy_w3=hy_w3, hy_b3=hy_b3, hy_w4=hy_w4, hy_freq=hy_freq, hy_bias=hy_bias,
                     gla_aw_f=gla_aw_f, gla_ab_f=gla_ab_f, gla_aw_b=gla_aw_b, gla_ab_b=gla_ab_b,
                     gla_norm_g=gla_norm_g)
    for l in range(depth):
        last = l == depth - 1
        p = {name: arr[l] for name, arr in per_layer.items()}
        wi = dict(_in_weights(w_in, l, d), w_up_hy=w_up_hy, w_up_gla=w_up_gla, w_out=w_out)
        lat = [mods[l, 0, i * d:(i + 1) * d] for i in range(N_ADA)]
        cxm = [mods[l, 1, i * d:(i + 1) * d] for i in range(N_ADA)]

        h_ctx, = _modnorm(x_ctx, norm_mix_g[l], cxm[0], cxm[1], (BF16,))
        if last:
            s_f, s_b = _gla_states_only(h_ctx, wi, p, d)
        else:
            zero = jnp.zeros((GLA_HEADS, dv, dk), F32)
            x_ctx, s_f, s_b = _token_mixer(x_ctx, h_ctx, cxm[2], zero, zero, wi, p, d)
        h_lat, = _modnorm(x_lat, norm_mix_g[l], lat[0], lat[1], (BF16,))
        x_lat, _, _ = _token_mixer(x_lat, h_lat, lat[2], s_f, s_b, wi, p, d)

        i = l // 2
        if l % 2 == 0:
            h2, = _modnorm(x_lat, norm_ffn_g[l], lat[3], lat[4], (BF16,))
            x_lat = _dense_ffn(x_lat, h2, lat[5], ffn_w_gate[i], ffn_w_up[i], ffn_w_down[i])
            if not last:
                h2c, = _modnorm(x_ctx, norm_ffn_g[l], cxm[3], cxm[4], (BF16,))
                x_ctx = _dense_ffn(x_ctx, h2c, cxm[5], ffn_w_gate[i], ffn_w_up[i], ffn_w_down[i])
        else:
            pm = dict(router_w=router_w[i], exp_w_gate=exp_w_gate[i], exp_w_up=exp_w_up[i], exp_w_down=exp_w_down[i])
            h2, h2p = _modnorm(x_lat, norm_ffn_g[l], lat[3], lat[4], (F32, jnp.uint32))
            x_lat = _moe_ffn(h2, h2p, x_lat, lat[5], pm, final_norm_g, final_norm=last)
            if not last:
                h2c, h2cp = _modnorm(x_ctx, norm_ffn_g[l], cxm[3], cxm[4], (F32, jnp.uint32))
                x_ctx = _moe_ffn(h2c, h2cp, x_ctx, cxm[5], pm, final_norm_g, final_norm=False)
    if depth % 2 == 1:
        x_lat, = _modnorm(x_lat, final_norm_g, jnp.zeros((d,), F32), jnp.zeros((d,), F32), (F32,))
    return x_lat[None]
```
